```python
import math
import jax, jax.numpy as jnp
from jax import lax
import numpy as np

D_MODEL = 1024
BATCH = 16
SEQ = 2048
DEPTH = 4

GRID_W = 64
CTX_LEN = 256
HEAD_DIM = 64
A_HEADS = 4
A_KV_HEADS = 2
Q_BLOCK = 128
ROPE_THETA = 10000.0
B_HEADS = 4
WIN_R = 8
WIN_C = 16
C_HEADS = 4
C_CONV = 3
D_HEADS = 4
D_KDIM = 32
D_GATE_RANK = 16
GLA_TAU = 16.0
CHUNK = 64
N_BRANCH = 4
N_GROUPS = 4
EXP_PER_GROUP = 8
N_EXPERTS = N_GROUPS * EXP_PER_GROUP
EXP_HIDDEN = 512
TOP_K = 2
EPS = 1e-6
DN_ALPHA = (2.0 * DEPTH) ** 0.25
DN_BETA = (8.0 * DEPTH) ** -0.25

A_QW = A_HEADS * HEAD_DIM
A_KVW = A_KV_HEADS * HEAD_DIM
B_W = B_HEADS * HEAD_DIM
C_W = C_HEADS * HEAD_DIM
D_KW = D_HEADS * D_KDIM
D_VW = D_HEADS * HEAD_DIM
IN_SPLITS = (
    ('a_q', A_QW), ('a_k', A_KVW), ('a_v', A_KVW),
    ('b_q', B_W), ('b_k', B_W), ('b_v', B_W),
    ('c_qkv', 3 * C_W), ('c_beta', 2 * C_HEADS), ('c_a', 2 * C_HEADS), ('c_g', C_W),
    ('d_q', D_KW), ('d_k', D_KW), ('d_v', D_VW), ('d_lr', 2 * D_GATE_RANK), ('d_g', D_VW),
    ('gates', N_BRANCH * D_MODEL),
)
IN_W = sum(w for _, w in IN_SPLITS)

kernel_name = 'hybrid_latent_diffusion_block'

F32 = jnp.float32


def split_in(proj):
    offs = np.cumsum([w for _, w in IN_SPLITS])[:-1].tolist()
    return dict(zip([n for n, _ in IN_SPLITS], jnp.split(proj, offs, axis=-1)))


def split_heads(x, h):
    b, n, _ = x.shape
    return x.reshape(b, n, h, -1).transpose(0, 2, 1, 3)


def merge_heads(x):
    b, h, n, d = x.shape
    return x.transpose(0, 2, 1, 3).reshape(b, n, h * d)


def rms_norm(x, g):
    xf = x.astype(F32)
    y = xf * lax.rsqrt(jnp.mean(xf * xf, axis=-1, keepdims=True) + EPS)
    return (y * g.astype(F32)).astype(x.dtype)


def layer_norm(x, g, b):
    xf = x.astype(F32)
    mu = jnp.mean(xf, axis=-1, keepdims=True)
    var = jnp.mean(jnp.square(xf - mu), axis=-1, keepdims=True)
    return ((xf - mu) * lax.rsqrt(var + EPS) * g.astype(F32) + b.astype(F32)).astype(x.dtype)


def l2_normalize(x):
    return x * lax.rsqrt(jnp.sum(x * x, axis=-1, keepdims=True) + EPS)


def axial_rope_tables(n, dtype):
    t = jnp.arange(n, dtype=jnp.int32)
    row = (t // GRID_W).astype(F32)
    col = (t % GRID_W).astype(F32)
    nf = HEAD_DIM // 4
    inv = ROPE_THETA ** (-jnp.arange(nf, dtype=F32) / nf)
    ang_r = row[:, None] * inv
    ang_c = col[:, None] * inv
    return tuple(a.astype(dtype) for a in (jnp.cos(ang_r), jnp.sin(ang_r), jnp.cos(ang_c), jnp.sin(ang_c)))


def _rope_half(x, cos, sin):
    x1, x2 = jnp.split(x, 2, axis=-1)
    return jnp.concatenate([x1 * cos - x2 * sin, x1 * sin + x2 * cos], axis=-1)


def axial_rope(x, rope):
    cos_r, sin_r, cos_c, sin_c = rope
    xr, xc = jnp.split(x, 2, axis=-1)
    return jnp.concatenate([_rope_half(xr, cos_r, sin_r), _rope_half(xc, cos_c, sin_c)], axis=-1)


def gqa_mixer(q, k, v, qc, kc, vc, q_gain, k_gain, rope):
    scale = HEAD_DIM ** -0.5
    rep = A_HEADS // A_KV_HEADS
    q = axial_rope(rms_norm(split_heads(q, A_HEADS), q_gain), rope)
    k = axial_rope(rms_norm(split_heads(k, A_KV_HEADS), k_gain), rope)
    v = split_heads(v, A_KV_HEADS)
    qc = rms_norm(split_heads(qc, A_HEADS), q_gain)
    kc = rms_norm(split_heads(kc, A_KV_HEADS), k_gain)
    vc = split_heads(vc, A_KV_HEADS)
    b, _, s, _ = q.shape
    nc = qc.shape[2]
    k_all = jnp.concatenate([k, kc], axis=2)
    v_all = jnp.concatenate([v, vc], axis=2)
    nb = s // Q_BLOCK
    qb = jnp.moveaxis(q.reshape(b, A_KV_HEADS, rep, nb, Q_BLOCK, HEAD_DIM), 3, 0)

    def block(qi):
        sc = jnp.einsum('bgrqd,bgkd->bgrqk', qi, k_all).astype(F32) * scale
        p = jax.nn.softmax(sc, axis=-1).astype(v_all.dtype)
        return jnp.einsum('bgrqk,bgkd->bgrqd', p, v_all)

    o = lax.map(block, qb)
    o = jnp.moveaxis(o, 0, 3).reshape(b, A_HEADS, s, HEAD_DIM)
    qcg = qc.reshape(b, A_KV_HEADS, rep, nc, HEAD_DIM)
    scc = jnp.einsum('bgrqd,bgkd->bgrqk', qcg, kc).astype(F32) * scale
    pc = jax.nn.softmax(scc, axis=-1).astype(vc.dtype)
    oc = jnp.einsum('bgrqk,bgkd->bgrqd', pc, vc).reshape(b, A_HEADS, nc, HEAD_DIM)
    return merge_heads(o), merge_heads(oc)


def na_mixer(q, k, v, qc, kc, vc, rpb):
    scale = HEAD_DIM ** -0.5
    q, k, v = split_heads(q, B_HEADS), split_heads(k, B_HEADS), split_heads(v, B_HEADS)
    qc, kc, vc = split_heads(qc, B_HEADS), split_heads(kc, B_HEADS), split_heads(vc, B_HEADS)
    b, h, s, d = q.shape
    rows = s // GRID_W
    wr = min(WIN_R, rows)
    nl = wr * GRID_W
    qg = q.reshape(b, h, rows, GRID_W, d)
    kg = k.reshape(b, h, rows, GRID_W, d)
    vg = v.reshape(b, h, rows, GRID_W, d)
    r_idx = jnp.arange(rows, dtype=jnp.int32)
    row_start = jnp.clip(r_idx - wr // 2, 0, rows - wr)
    cols = jnp.arange(GRID_W, dtype=jnp.int32)
    col_start = jnp.clip(cols - WIN_C // 2, 0, GRID_W - WIN_C)
    col_ok = (cols[None, :] >= col_start[:, None]) & (cols[None, :] < col_start[:, None] + WIN_C)
    dc_idx = jnp.clip(cols[None, :] - cols[:, None] + WIN_C - 1, 0, 2 * WIN_C - 2)
    mask = jnp.broadcast_to(col_ok[:, None, :], (GRID_W, wr, GRID_W)).reshape(GRID_W, nl)

    def row_block(inp):
        q_r, r, rs = inp
        k_r = lax.dynamic_slice_in_dim(kg, rs, wr, axis=2).reshape(b, h, nl, d)
        v_r = lax.dynamic_slice_in_dim(vg, rs, wr, axis=2).reshape(b, h, nl, d)
        dr_idx = rs + jnp.arange(wr, dtype=jnp.int32) - r + WIN_R - 1
        bias = rpb[:, dr_idx[None, :, None], dc_idx[:, None, :]].reshape(h, GRID_W, nl)
        s_loc = jnp.einsum('bhqd,bhkd->bhqk', q_r, k_r).astype(F32) * scale + bias.astype(F32)
        s_loc = jnp.where(mask, s_loc, -jnp.inf)
        s_ctx = jnp.einsum('bhqd,bhkd->bhqk', q_r, kc).astype(F32) * scale
        p = jax.nn.softmax(jnp.concatenate([s_loc, s_ctx], axis=-1), axis=-1).astype(v.dtype)
        return (jnp.einsum('bhqk,bhkd->bhqd', p[..., :nl], v_r)
                + jnp.einsum('bhqk,bhkd->bhqd', p[..., nl:], vc))

    o = lax.map(row_block, (jnp.moveaxis(qg, 2, 0), r_idx, row_start))
    o = jnp.moveaxis(o, 0, 2).reshape(b, h, s, d)
    scc = jnp.einsum('bhqd,bhkd->bhqk', qc, kc).astype(F32) * scale
    oc = jnp.einsum('bhqk,bhkd->bhqd', jax.nn.softmax(scc, axis=-1).astype(vc.dtype), vc)
    return merge_heads(o), merge_heads(oc)


def _to_chunks(x):
    b, h, n = x.shape[:3]
    return jnp.moveaxis(x.reshape(b, h, n // CHUNK, CHUNK, *x.shape[3:]), 2, 0)


def _from_chunks(o):
    nc, b, h, c, dv = o.shape
    return jnp.moveaxis(o, 0, 2).reshape(b, h, nc * c, dv)


def gated_delta_chunked(q, k, v, beta, log_a, s0):
    dk = q.shape[-1]
    tri_incl = jnp.tril(jnp.ones((CHUNK, CHUNK), bool))
    tri_strict = jnp.tril(jnp.ones((CHUNK, CHUNK), bool), -1)
    eye = jnp.eye(CHUNK, dtype=F32)

    def step(s, inp):
        qc, kc, vc, bc, gc = inp
        g = jnp.cumsum(gc, axis=-1)
        decay = jnp.exp(jnp.where(tri_incl, g[..., :, None] - g[..., None, :], -jnp.inf))
        kk = jnp.einsum('bhid,bhjd->bhij', kc, kc)
        lmat = jnp.where(tri_strict, bc[..., :, None] * kk * decay, 0.0)
        rhs = jnp.concatenate([kc * (bc * jnp.exp(g))[..., None], vc * bc[..., None]], axis=-1)
        sol = lax.linalg.triangular_solve(eye + lmat, rhs, left_side=True, lower=True, unit_diagonal=True)
        w, u0 = sol[..., :dk], sol[..., dk:]
        u = u0 - jnp.einsum('bhcd,bhde->bhce', w, s)
        qk = jnp.einsum('bhid,bhjd->bhij', qc, kc) * decay
        o = (jnp.einsum('bhcd,bhde->bhce', qc * jnp.exp(g)[..., None], s)
             + jnp.einsum('bhij,bhje->bhie', qk, u))
        g_last = g[..., -1:]
        s_new = (s * jnp.exp(g_last)[..., None]
                 + jnp.einsum('bhcd,bhce->bhde', kc * jnp.exp(g_last - g)[..., None], u))
        return s_new, o

    s_fin, o = lax.scan(step, s0, tuple(_to_chunks(t) for t in (q, k, v, beta, log_a)))
    return _from_chunks(o), s_fin


def gla_chunked(q, k, v, log_a, s0):
    tri_incl = jnp.tril(jnp.ones((CHUNK, CHUNK), bool))[:, :, None]

    def step(s, inp):
        qc, kc, vc, gc = inp
        g = jnp.cumsum(gc, axis=2)
        diff = g[:, :, :, None, :] - g[:, :, None, :, :]
        decay = jnp.exp(jnp.where(tri_incl, diff, -jnp.inf))
        a = jnp.einsum('bhid,bhijd,bhjd->bhij', qc, decay, kc)
        o = (jnp.einsum('bhcd,bhde->bhce', qc * jnp.exp(g), s)
             + jnp.einsum('bhij,bhje->bhie', a, vc))
        g_last = g[:, :, -1:, :]
        s_new = (s * jnp.exp(g_last[:, :, 0, :])[..., None]
                 + jnp.einsum('bhcd,bhce->bhde', kc * jnp.exp(g_last - g), vc))
        return s_new, o

    s_fin, o = lax.scan(step, s0, tuple(_to_chunks(t) for t in (q, k, v, log_a)))
    return _from_chunks(o), s_fin


def _orient(t, d):
    return t if d == 0 else jnp.flip(t, axis=2)


def bidirectional_scan(scan_fn, lat, lat_dir, ctx, ctx_dir, s0):
    out_l, out_c = 0.0, 0.0
    for d in range(2):
        oc, sc = scan_fn(*[_orient(t, d) for t in ctx], *[_orient(g[d], d) for g in ctx_dir], s0)
        ol, _ = scan_fn(*[_orient(t, d) for t in lat], *[_orient(g[d], d) for g in lat_dir], sc)
        out_l = out_l + _orient(ol, d)
        out_c = out_c + _orient(oc, d)
    return out_l, out_c


def _gated_out(o, gate, gain, n_heads):
    o = rms_norm(o, gain) * jax.nn.silu(split_heads(gate.astype(F32), n_heads))
    return merge_heads(o).astype(gate.dtype)


def short_conv(x, w):
    return lax.conv_general_dilated(x, w[:, None, :].astype(x.dtype), window_strides=(1,),
                                    padding=((C_CONV // 2, C_CONV // 2),),
                                    dimension_numbers=('NWC', 'WIO', 'NWC'),
                                    feature_group_count=x.shape[-1])


def _gdn_prep(qkv, beta_l, a_l, conv_w, a_log, dt_bias):
    b, n, _ = qkv.shape
    qkv = jax.nn.silu(short_conv(qkv, conv_w)).astype(F32)
    q, k, v = jnp.split(qkv, 3, axis=-1)
    q = l2_normalize(split_heads(q, C_HEADS)) * HEAD_DIM ** -0.5
    k = l2_normalize(split_heads(k, C_HEADS))
    v = split_heads(v, C_HEADS)
    beta = jax.nn.sigmoid(beta_l.astype(F32)).reshape(b, n, 2, C_HEADS).transpose(2, 0, 3, 1)
    a = a_l.astype(F32).reshape(b, n, 2, C_HEADS).transpose(2, 0, 3, 1)
    log_a = -jnp.exp(a_log.astype(F32))[:, None, :, None] * jax.nn.softplus(a + dt_bias.astype(F32)[:, None, :, None])
    return q, k, v, beta, log_a


def gdn_mixer(lat, ctx, conv_w, a_log, dt_bias, out_gain):
    ql, kl, vl, bl, gl = _gdn_prep(lat[0], lat[1], lat[2], conv_w, a_log, dt_bias)
    qc, kc, vc, bc, gc = _gdn_prep(ctx[0], ctx[1], ctx[2], conv_w, a_log, dt_bias)
    s0 = jnp.zeros((ql.shape[0], C_HEADS, HEAD_DIM, HEAD_DIM), F32)
    ol, oc = bidirectional_scan(gated_delta_chunked, (ql, kl, vl), (bl, gl), (qc, kc, vc), (bc, gc), s0)
    return _gated_out(ol, lat[3], out_gain, C_HEADS), _gated_out(oc, ctx[3], out_gain, C_HEADS)


def _gla_prep(q, k, v, lr, gw, gb):
    b, n, _ = q.shape
    q = split_heads(q.astype(F32), D_HEADS) * D_KDIM ** -0.5
    k = split_heads(k.astype(F32), D_HEADS)
    v = split_heads(v.astype(F32), D_HEADS)
    z = jnp.einsum('bnzr,zrk->bnzk', lr.astype(F32).reshape(b, n, 2, D_GATE_RANK), gw.astype(F32)) + gb.astype(F32)
    log_a = (jax.nn.log_sigmoid(z) / GLA_TAU).reshape(b, n, 2, D_HEADS, D_KDIM).transpose(2, 0, 3, 1, 4)
    return q, k, v, log_a


def gla_mixer(lat, ctx, gw, gb, out_gain):
    ql, kl, vl, gl = _gla_prep(lat[0], lat[1], lat[2], lat[3], gw, gb)
    qc, kc, vc, gc = _gla_prep(ctx[0], ctx[1], ctx[2], ctx[3], gw, gb)
    s0 = jnp.zeros((ql.shape[0], D_HEADS, D_KDIM, HEAD_DIM), F32)
    ol, oc = bidirectional_scan(gla_chunked, (ql, kl, vl), (gl,), (qc, kc, vc), (gc,), s0)
    return _gated_out(ol, lat[4], out_gain, D_HEADS), _gated_out(oc, ctx[4], out_gain, D_HEADS)


def merge_branches(branches, gates, w_br, w_o):
    o = jnp.stack(branches, axis=2)
    up = jnp.einsum('bnzc,zcd->bnzd', o, w_br)
    g = jax.nn.sigmoid(gates.reshape(*gates.shape[:-1], N_BRANCH, D_MODEL))
    return jnp.sum(g * up, axis=2) @ w_o


def mixer_sublayer(h, hc, w_in, w_br, w_o, a_qg, a_kg, rpb, c_conv, c_a_log, c_dt_bias, c_og,
                   d_gw, d_gb, d_og, rope, with_ctx):
    p = split_in(h @ w_in)
    pc = split_in(hc @ w_in)
    oa, oca = gqa_mixer(p['a_q'], p['a_k'], p['a_v'], pc['a_q'], pc['a_k'], pc['a_v'], a_qg, a_kg, rope)
    ob, ocb = na_mixer(p['b_q'], p['b_k'], p['b_v'], pc['b_q'], pc['b_k'], pc['b_v'], rpb)
    oc_, occ = gdn_mixer((p['c_qkv'], p['c_beta'], p['c_a'], p['c_g']),
                         (pc['c_qkv'], pc['c_beta'], pc['c_a'], pc['c_g']), c_conv, c_a_log, c_dt_bias, c_og)
    od, ocd = gla_mixer((p['d_q'], p['d_k'], p['d_v'], p['d_lr'], p['d_g']),
                        (pc['d_q'], pc['d_k'], pc['d_v'], pc['d_lr'], pc['d_g']), d_gw, d_gb, d_og)
    y = merge_branches((oa, ob, oc_, od), p['gates'], w_br, w_o)
    yc = merge_branches((oca, ocb, occ, ocd), pc['gates'], w_br, w_o) if with_ctx else None
    return y, yc


def hier_moe(h, w_rg, b_rg, w_re, b_re, w_up, w_gate, w_down):
    shp = h.shape
    t = h.reshape(-1, shp[-1])
    lg = (t @ w_rg).astype(F32) + b_rg.astype(F32)
    grp = jnp.argmax(lg, axis=-1)
    p_grp = jnp.take_along_axis(jax.nn.softmax(lg, axis=-1), grp[:, None], axis=-1)
    le = jnp.einsum('td,gde->tge', t, w_re).astype(F32) + b_re.astype(F32)
    le = jnp.take_along_axis(le, grp[:, None, None], axis=1)[:, 0]
    top_p, top_i = lax.top_k(jax.nn.softmax(le, axis=-1), TOP_K)
    wts = p_grp * top_p / jnp.sum(top_p, axis=-1, keepdims=True)
    eid = grp[:, None] * EXP_PER_GROUP + top_i
    dense = jnp.einsum('tk,tke->te', wts, jax.nn.one_hot(eid, N_EXPERTS, dtype=F32)).astype(h.dtype)
    y = jnp.zeros_like(t)
    for e in range(N_EXPERTS):
        a = jax.nn.silu(t @ w_gate[e]) * (t @ w_up[e])
        y = y + dense[:, e:e + 1] * (a @ w_down[e])
    return y.reshape(shp)


def setup_inputs(seed: int = 0) -> dict:
    key = jax.random.key(seed)
    ks = jax.random.split(key, 32)
    L, D = DEPTH, D_MODEL

    def nrm(k, shape, s):
        return jax.random.normal(k, shape, F32) * s

    dt = jnp.exp(jax.random.uniform(ks[12], (L, 2, C_HEADS), F32, math.log(1e-3), math.log(1e-1)))
    return {
        'x': nrm(ks[0], (BATCH, SEQ, D), 1.0),
        'c': nrm(ks[1], (BATCH, D), 1.0),
        'ctx': nrm(ks[2], (BATCH, CTX_LEN, D), 1.0),
        'c_ctx': nrm(ks[3], (D,), 1.0),
        'w_ada': nrm(ks[4], (L, D, 6 * D), 0.5 * D ** -0.5),
        'b_ada': nrm(ks[5], (L, 6 * D), 0.02),
        'w_in': nrm(ks[6], (L, D, IN_W), D ** -0.5),
        'a_q_gain': 1.0 + nrm(ks[7], (L, HEAD_DIM), 0.02),
        'a_k_gain': 1.0 + nrm(ks[8], (L, HEAD_DIM), 0.02),
        'b_rpb': nrm(ks[9], (L, B_HEADS, 2 * WIN_R - 1, 2 * WIN_C - 1), 0.1),
        'c_conv': nrm(ks[10], (L, C_CONV, 3 * C_W), C_CONV ** -0.5),
        'c_a_log': jnp.log(jax.random.uniform(ks[11], (L, 2, C_HEADS), F32, 1.0, 16.0)),
        'c_dt_bias': dt + jnp.log(-jnp.expm1(-dt)),
        'c_out_gain': 1.0 + nrm(ks[13], (L, HEAD_DIM), 0.02),
        'd_gate_w': nrm(ks[14], (L, 2, D_GATE_RANK, D_KW), D_GATE_RANK ** -0.5),
        'd_gate_b': nrm(ks[15], (L, 2, D_KW), 0.1),
        'd_out_gain': 1.0 + nrm(ks[16], (L, HEAD_DIM), 0.02),
        'w_branch': nrm(ks[17], (L, N_BRANCH, A_QW, D), A_QW ** -0.5),
        'w_out': nrm(ks[18], (L, D, D), D ** -0.5 * DN_BETA),
        'ln1_g': 1.0 + nrm(ks[19], (L, D), 0.02),
        'ln1_b': nrm(ks[20], (L, D), 0.02),
        'ln2_g': 1.0 + nrm(ks[21], (L, D), 0.02),
        'ln2_b': nrm(ks[22], (L, D), 0.02),
        'w_router_g': nrm(ks[23], (L, D, N_GROUPS), D ** -0.5),
        'b_router_g': nrm(ks[24], (L, N_GROUPS), 0.01),
        'w_router_e': nrm(ks[25], (L, N_GROUPS, D, EXP_PER_GROUP), D ** -0.5),
        'b_router_e': nrm(ks[26], (L, N_GROUPS, EXP_PER_GROUP), 0.01),
        'w_up': nrm(ks[27], (L, N_EXPERTS, D, EXP_HIDDEN), D ** -0.5),
        'w_gate': nrm(ks[28], (L, N_EXPERTS, D, EXP_HIDDEN), D ** -0.5),
        'w_down': nrm(ks[29], (L, N_EXPERTS, EXP_HIDDEN, D), EXP_HIDDEN ** -0.5 * DN_BETA),
    }


def reference(x, c, ctx, c_ctx, w_ada, b_ada, w_in, a_q_gain, a_k_gain, b_rpb, c_conv, c_a_log, c_dt_bias,
              c_out_gain, d_gate_w, d_gate_b, d_out_gain, w_branch, w_out, ln1_g, ln1_b, ln2_g, ln2_b,
              w_router_g, b_router_g, w_router_e, b_router_e, w_up, w_gate, w_down):
    rope = axial_rope_tables(x.shape[1], x.dtype)
    sc = jax.nn.silu(c)
    scc = jax.nn.silu(c_ctx)
    xc = ctx
    for l in range(DEPTH):
        with_ctx = l < DEPTH - 1
        mod = (sc @ w_ada[l] + b_ada[l])[:, None, :]
        modc = scc @ w_ada[l] + b_ada[l]
        sh1, s1, g1, sh2, s2, g2 = jnp.split(mod, 6, axis=-1)
        sh1c, s1c, g1c, sh2c, s2c, g2c = jnp.split(modc, 6, axis=-1)
        h = x * (1.0 + s1) + sh1
        hc = xc * (1.0 + s1c) + sh1c
        y, yc = mixer_sublayer(h, hc, w_in[l], w_branch[l], w_out[l], a_q_gain[l], a_k_gain[l], b_rpb[l],
                               c_conv[l], c_a_log[l], c_dt_bias[l], c_out_gain[l],
                               d_gate_w[l], d_gate_b[l], d_out_gain[l], rope, with_ctx)
        x = layer_norm(DN_ALPHA * x + g1 * y, ln1_g[l], ln1_b[l])
        h = x * (1.0 + s2) + sh2
        x = layer_norm(DN_ALPHA * x + g2 * hier_moe(h, w_router_g[l], b_router_g[l], w_router_e[l], b_router_e[l],
                                                   w_up[l], w_gate[l], w_down[l]), ln2_g[l], ln2_b[l])
        if with_ctx:
            xc = layer_norm(DN_ALPHA * xc + g1c * yc, ln1_g[l], ln1_b[l])
            hc = xc * (1.0 + s2c) + sh2c
            xc = layer_norm(DN_ALPHA * xc + g2c * hier_moe(hc, w_router_g[l], b_router_g[l], w_router_e[l],
                                                          b_router_e[l], w_up[l], w_gate[l], w_down[l]),
                            ln2_g[l], ln2_b[l])
    return x
```

```python
import functools
import math

import numpy as np
import jax
import jax.numpy as jnp
from jax import lax
from jax.experimental import pallas as pl
from jax.experimental.pallas import tpu as pltpu

F32 = jnp.float32
BF16 = jnp.bfloat16

D_MODEL = 1024
DEPTH = 4
GRID_W = 64
CTX_LEN = 256
HEAD_DIM = 64
A_HEADS = 4
A_KV_HEADS = 2
ROPE_THETA = 10000.0
B_HEADS = 4
WIN_R = 8
WIN_C = 16
C_HEADS = 4
D_HEADS = 4
D_KDIM = 32
D_GATE_RANK = 16
GLA_TAU = 16.0
CHUNK = 64
N_BRANCH = 4
N_GROUPS = 4
EXP_PER_GROUP = 8
N_EXPERTS = N_GROUPS * EXP_PER_GROUP
EXP_HIDDEN = 512
EPS = 1e-6
DN_ALPHA = (2.0 * DEPTH) ** 0.25
NEG_BIG = -1e30

LANES = 128
SUBLANES = 8
TOKEN_TILE = 256
VMEM_LIMIT = 56 * 1024 * 1024

_IN_OFFS = {}
_off = 0
for _n, _w in (('a_q', 256), ('a_k', 128), ('a_v', 128), ('b_q', 256), ('b_k', 256), ('b_v', 256),
               ('c_qkv', 768), ('c_beta', 8), ('c_a', 8), ('c_g', 256), ('d_q', 128), ('d_k', 128),
               ('d_v', 256), ('d_lr', 32), ('d_g', 256), ('gates', 4096)):
    _IN_OFFS[_n] = (_off, _w)
    _off += _w
IN_GROUPS = (
    ('a_q', ('a_q',), 256, BF16),
    ('a_kv', ('a_k', 'a_v'), 256, BF16),
    ('b_q', ('b_q',), 256, BF16),
    ('b_k', ('b_k',), 256, BF16),
    ('b_v', ('b_v',), 256, BF16),
    ('c_qkv', ('c_qkv',), 768, F32),
    ('c_ba', ('c_beta', 'c_a'), 128, F32),
    ('c_g', ('c_g',), 256, F32),
    ('d_qk', ('d_q', 'd_k'), 256, F32),
    ('d_v', ('d_v',), 256, F32),
    ('d_lr', ('d_lr',), 128, F32),
    ('d_g', ('d_g',), 256, F32),
)
IN_TOTAL = sum(g[2] for g in IN_GROUPS)

MOE_CHUNK = 2048
MOE_ROWS = 128
MOE_STRIDE = MOE_ROWS + SUBLANES
ROW_VREGS = D_MODEL // LANES


def _dot(a, b):
    return jnp.dot(a.astype(BF16), b.astype(BF16), preferred_element_type=F32)


def _dot_nt(a, b):
    return lax.dot_general(a.astype(BF16), b.astype(BF16), (((1,), (1,)), ((), ())),
                           preferred_element_type=F32)


def _dot_tn(a, b):
    return lax.dot_general(a.astype(BF16), b.astype(BF16), (((0,), (0,)), ((), ())),
                           preferred_element_type=F32)


def _split(x):
    hi = x.astype(BF16)
    lo = (x - hi.astype(F32)).astype(BF16)
    return hi, lo


def _dot3(a, b):
    ah, al = _split(a)
    bh, bl = _split(b)
    return (jnp.dot(ah, bh, preferred_element_type=F32) + jnp.dot(al, bh, preferred_element_type=F32)
            + jnp.dot(ah, bl, preferred_element_type=F32))


def _dot_sel(a, m):
    ah, al = _split(a)
    return jnp.dot(ah, m, preferred_element_type=F32) + jnp.dot(al, m, preferred_element_type=F32)


def _silu(x):
    return x * jax.nn.sigmoid(x)


def _layer_norm(r, g, b):
    mu = jnp.mean(r, axis=-1, keepdims=True)
    d = r - mu
    var = jnp.mean(d * d, axis=-1, keepdims=True)
    return d * lax.rsqrt(var + EPS) * g + b


def _params(sem):
    return pltpu.CompilerParams(dimension_semantics=sem, vmem_limit_bytes=VMEM_LIMIT)


ADA_ROWS = 24
ADA_TILE = 1536


def _ada_kernel(cc_ref, w_ref, b_ref, o_ref):
    s = _silu(cc_ref[...])
    o_ref[0] = _dot3(s, w_ref[0]) + b_ref[0]


def _ada_call(cc, w_ada, b_ada):
    depth = w_ada.shape[0]
    n = w_ada.shape[2]
    return pl.pallas_call(
        _ada_kernel,
        grid=(depth, n // ADA_TILE),
        in_specs=[
            pl.BlockSpec((ADA_ROWS, D_MODEL), lambda l, j: (0, 0)),
            pl.BlockSpec((1, D_MODEL, ADA_TILE), lambda l, j: (l, 0, j)),
            pl.BlockSpec((1, 1, ADA_TILE), lambda l, j: (l, 0, j)),
        ],
        out_specs=pl.BlockSpec((1, ADA_ROWS, ADA_TILE), lambda l, j: (l, 0, j)),
        out_shape=jax.ShapeDtypeStruct((depth, ADA_ROWS, n), F32),
        compiler_params=_params(("arbitrary", "arbitrary")),
        name="ada_mod",
    )(cc, w_ada, b_ada.reshape(depth, 1, n))


def _mod_index(nb):
    return lambda b, t: (jnp.where(t == 0, nb, b), 0, 0)


def _in_kernel(x_ref, mod_ref, w_ref, *out_refs):
    x = x_ref[0]
    mod = mod_ref[0]
    h = (x * (1.0 + mod[1:2]) + mod[0:1]).astype(BF16)
    off = 0
    for (name, _, width, dt), o_ref in zip(IN_GROUPS, out_refs):
        o_ref[0] = jnp.dot(h, w_ref[:, off:off + width], preferred_element_type=F32).astype(dt)
        off += width


def _in_call(xa, mods, w_cat):
    nb, nt, _ = xa.shape
    tiles = nt // TOKEN_TILE
    return pl.pallas_call(
        _in_kernel,
        grid=(nb, tiles),
        in_specs=[
            pl.BlockSpec((1, TOKEN_TILE, D_MODEL), lambda b, t: (b, t, 0)),
            pl.BlockSpec((1, SUBLANES, D_MODEL), _mod_index(nb)),
            pl.BlockSpec((D_MODEL, IN_TOTAL), lambda b, t: (0, 0)),
        ],
        out_specs=[pl.BlockSpec((1, TOKEN_TILE, g[2]), lambda b, t: (b, t, 0)) for g in IN_GROUPS],
        out_shape=[jax.ShapeDtypeStruct((nb, nt, g[2]), g[3]) for g in IN_GROUPS],
        compiler_params=_params(("arbitrary", "arbitrary")),
        name="in_proj",
    )(xa, mods, w_cat)


def _head_rms(x, ones_bd, gain):
    ss = _dot_sel(x * x, ones_bd)
    return x * lax.rsqrt(ss * (1.0 / HEAD_DIM) + EPS) * gain


def _rope(x, rot, cos, sin):
    return x * cos + _dot_sel(x, rot) * sin


def _attn_a_kernel(q_ref, kv_ref, cos_ref, sin_ref, qg_ref, kg_ref, bd_ref, rot_ref, o_ref, kp_ref):
    t = pl.program_id(1)
    kvw = A_KV_HEADS * HEAD_DIM
    n_lat_tiles = (kv_ref.shape[1] - CTX_LEN) // TOKEN_TILE
    scale = HEAD_DIM ** -0.5

    @pl.when(t == 0)
    def _prep_keys():
        bd = bd_ref[0:kvw, 0:kvw]
        rot = rot_ref[0:kvw, 0:kvw]
        kg = kg_ref[...]
        kc = kv_ref[0, 0:CTX_LEN, 0:kvw].astype(F32)
        kp_ref[0:CTX_LEN, :] = _head_rms(kc, bd, kg).astype(BF16)

        def body(i, carry):
            r0 = pl.multiple_of(i * TOKEN_TILE, TOKEN_TILE)
            k = kv_ref[0, pl.ds(CTX_LEN + r0, TOKEN_TILE), 0:kvw].astype(F32)
            kn = _head_rms(k, bd, kg)
            kr = _rope(kn, rot, cos_ref[pl.ds(r0, TOKEN_TILE), 0:kvw], sin_ref[pl.ds(r0, TOKEN_TILE), 0:kvw])
            kp_ref[pl.ds(CTX_LEN + r0, TOKEN_TILE), :] = kr.astype(BF16)
            return carry

        lax.fori_loop(0, n_lat_tiles, body, 0)

    qn = _head_rms(q_ref[0].astype(F32), bd_ref[...], qg_ref[...])

    def attend(qh, nk):
        outs = []
        for g in range(A_KV_HEADS):
            kg_ = kp_ref[0:nk, g * HEAD_DIM:(g + 1) * HEAD_DIM]
            vg_ = kv_ref[0, 0:nk, kvw + g * HEAD_DIM:kvw + (g + 1) * HEAD_DIM]
            for r in range(A_HEADS // A_KV_HEADS):
                h = g * (A_HEADS // A_KV_HEADS) + r
                qq = (qh[:, h * HEAD_DIM:(h + 1) * HEAD_DIM] * scale).astype(BF16)
                s = _dot_nt(qq, kg_)
                m = jnp.max(s, axis=-1, keepdims=True)
                e = jnp.exp(s - m)
                l = jnp.sum(e, axis=-1, keepdims=True)
                outs.append(jnp.dot(e.astype(BF16), vg_, preferred_element_type=F32) / l)
        return jnp.concatenate(outs, axis=-1)

    @pl.when(t == 0)
    def _ctx_queries():
        o_ref[0] = attend(qn, CTX_LEN).astype(o_ref.dtype)

    @pl.when(t > 0)
    def _latent_queries():
        r0 = pl.multiple_of((t - 1) * TOKEN_TILE, TOKEN_TILE)
        qr = _rope(qn, rot_ref[...], cos_ref[pl.ds(r0, TOKEN_TILE), :], sin_ref[pl.ds(r0, TOKEN_TILE), :])
        o_ref[0] = attend(qr, kv_ref.shape[1]).astype(o_ref.dtype)


def _attn_a_call(a_q, a_kv, cos_t, sin_t, q_gain, k_gain, ones_bd, rot_m):
    nb, nt, _ = a_q.shape
    tiles = nt // TOKEN_TILE
    seq = nt - CTX_LEN
    qw = A_HEADS * HEAD_DIM
    return pl.pallas_call(
        _attn_a_kernel,
        grid=(nb, tiles),
        in_specs=[
            pl.BlockSpec((1, TOKEN_TILE, qw), lambda b, t: (b, t, 0)),
            pl.BlockSpec((1, nt, qw), lambda b, t: (b, 0, 0)),
            pl.BlockSpec((seq, qw), lambda b, t: (0, 0)),
            pl.BlockSpec((seq, qw), lambda b, t: (0, 0)),
            pl.BlockSpec((1, qw), lambda b, t: (0, 0)),
            pl.BlockSpec((1, A_KV_HEADS * HEAD_DIM), lambda b, t: (0, 0)),
            pl.BlockSpec((qw, qw), lambda b, t: (0, 0)),
            pl.BlockSpec((qw, qw), lambda b, t: (0, 0)),
        ],
        out_specs=pl.BlockSpec((1, TOKEN_TILE, qw), lambda b, t: (b, t, 0)),
        out_shape=jax.ShapeDtypeStruct((nb, nt, qw), BF16),
        scratch_shapes=[pltpu.VMEM((nt, A_KV_HEADS * HEAD_DIM), BF16)],
        compiler_params=_params(("arbitrary", "arbitrary")),
        name="mixer_a_gqa",
    )(a_q, a_kv, cos_t, sin_t, q_gain, k_gain, ones_bd, rot_m)


NB_QROWS = TOKEN_TILE // GRID_W
NB_KROWS = 12
NB_INVALID = 2 * WIN_R - 1


def _attn_b_kernel(q_ref, k_ref, v_ref, bt_ref, o_ref):
    t = pl.program_id(1)
    scale = HEAD_DIM ** -0.5
    rows = (k_ref.shape[1] - CTX_LEN) // GRID_W
    wr = min(WIN_R, rows)

    def softmax_pv(parts):
        m = None
        for s, _ in parts:
            mi = jnp.max(s, axis=-1, keepdims=True)
            m = mi if m is None else jnp.maximum(m, mi)
        acc, l = None, None
        for s, v in parts:
            e = jnp.exp(s - m)
            li = jnp.sum(e, axis=-1, keepdims=True)
            oi = jnp.dot(e.astype(BF16), v, preferred_element_type=F32)
            acc = oi if acc is None else acc + oi
            l = li if l is None else l + li
        return acc / l

    @pl.when(t == 0)
    def _ctx_queries():
        outs = []
        for h in range(B_HEADS):
            sl = slice(h * HEAD_DIM, (h + 1) * HEAD_DIM)
            qq = (q_ref[0, :, sl].astype(F32) * scale).astype(BF16)
            s = _dot_nt(qq, k_ref[0, 0:CTX_LEN, sl])
            outs.append(softmax_pv([(s, v_ref[0, 0:CTX_LEN, sl])]))
        o_ref[0] = jnp.concatenate(outs, axis=-1).astype(o_ref.dtype)

    @pl.when(t > 0)
    def _latent_queries():
        r0 = (t - 1) * NB_QROWS
        start = jnp.clip(r0 - wr // 2, 0, rows - NB_KROWS)
        k0 = pl.multiple_of(CTX_LEN + start * GRID_W, GRID_W)
        nk = NB_KROWS * GRID_W
        lane = lax.broadcasted_iota(jnp.int32, (GRID_W, 2 * GRID_W), 1)
        left = lane < GRID_W
        slots = []
        for i in range(NB_QROWS):
            r = r0 + i
            rs = jnp.clip(r - wr // 2, 0, rows - wr)
            row_slots = []
            for j in range(NB_KROWS):
                kr = start + j
                ok = jnp.logical_and(kr >= rs, kr < rs + wr)
                row_slots.append(jnp.where(ok, kr - r + WIN_R - 1, NB_INVALID))
            slots.append(row_slots)
        outs = []
        for h in range(B_HEADS):
            sl = slice(h * HEAD_DIM, (h + 1) * HEAD_DIM)
            qq = (q_ref[0, :, sl].astype(F32) * scale).astype(BF16)
            s_loc = _dot_nt(qq, k_ref[0, pl.ds(k0, nk), sl])
            s_ctx = _dot_nt(qq, k_ref[0, 0:CTX_LEN, sl])
            bias_rows = []
            for i in range(NB_QROWS):
                tiles = []
                for jp in range(NB_KROWS // 2):
                    b0 = bt_ref[h, slots[i][2 * jp]]
                    b1 = bt_ref[h, slots[i][2 * jp + 1]]
                    tiles.append(jnp.where(left, b0, b1))
                bias_rows.append(jnp.concatenate(tiles, axis=-1))
            s_loc = s_loc + jnp.concatenate(bias_rows, axis=0)
            outs.append(softmax_pv([(s_loc, v_ref[0, pl.ds(k0, nk), sl]), (s_ctx, v_ref[0, 0:CTX_LEN, sl])]))
        o_ref[0] = jnp.concatenate(outs, axis=-1).astype(o_ref.dtype)


def _attn_b_call(b_q, b_k, b_v, bias_tab):
    nb, nt, w = b_q.shape
    tiles = nt // TOKEN_TILE
    return pl.pallas_call(
        _attn_b_kernel,
        grid=(nb, tiles),
        in_specs=[
            pl.BlockSpec((1, TOKEN_TILE, w), lambda b, t: (b, t, 0)),
            pl.BlockSpec((1, nt, w), lambda b, t: (b, 0, 0)),
            pl.BlockSpec((1, nt, w), lambda b, t: (b, 0, 0)),
            pl.BlockSpec(bias_tab.shape, lambda b, t: (0, 0, 0, 0)),
        ],
        out_specs=pl.BlockSpec((1, TOKEN_TILE, w), lambda b, t: (b, t, 0)),
        out_shape=jax.ShapeDtypeStruct((nb, nt, w), BF16),
        compiler_params=_params(("arbitrary", "arbitrary")),
        name="mixer_b_neighbourhood",
    )(b_q, b_k, b_v, bias_tab)


def _nb_bias_table(rpb):
    cols = jnp.arange(GRID_W, dtype=jnp.int32)
    col_start = jnp.clip(cols - WIN_C // 2, 0, GRID_W - WIN_C)
    col_ok = (cols[None, :] >= col_start[:, None]) & (cols[None, :] < col_start[:, None] + WIN_C)
    dc_idx = jnp.clip(cols[None, :] - cols[:, None] + WIN_C - 1, 0, 2 * WIN_C - 2)
    tab = rpb.astype(F32)[:, :, dc_idx]
    tab = jnp.where(col_ok[None, None], tab, NEG_BIG)
    tab = jnp.concatenate([tab, jnp.full_like(tab[:, :1], NEG_BIG)], axis=1)
    return jnp.concatenate([tab, tab], axis=-1)


def _order_masks():
    i = lax.broadcasted_iota(jnp.int32, (CHUNK, CHUNK), 0)
    j = lax.broadcasted_iota(jnp.int32, (CHUNK, CHUNK), 1)
    return ((j <= i, j < i), (j >= i, j > i))


def _scan_chunk(step, direction, n_ctx_chunks, n_chunks):
    if direction == 0:
        return step
    return jnp.where(step < n_ctx_chunks, n_ctx_chunks - 1 - step, n_chunks + n_ctx_chunks - 1 - step)


def _sel_dot(m, a):
    ah, al = _split(a)
    return jnp.dot(m, ah, preferred_element_type=F32) + jnp.dot(m, al, preferred_element_type=F32)


def _gated_out_tiles(of_ref, ob_ref, gate_ref, gain_ref, bd_ref, o_ref):
    nt = of_ref.shape[0]
    bd = bd_ref[...]
    gain = gain_ref[...]

    def body(i, carry):
        r0 = pl.multiple_of(i * TOKEN_TILE, TOKEN_TILE)
        o = of_ref[pl.ds(r0, TOKEN_TILE), :] + ob_ref[pl.ds(r0, TOKEN_TILE), :]
        y = _head_rms(o, bd, gain) * _silu(gate_ref[0, pl.ds(r0, TOKEN_TILE), :])
        o_ref[0, pl.ds(r0, TOKEN_TILE), :] = y.astype(o_ref.dtype)
        return carry

    lax.fori_loop(0, nt // TOKEN_TILE, body, 0)


C_BETA_LANE = 0
C_A_LANE = 8
C_T_ROWS = 16


def _unit_lower_solve(lmat, rhs):
    powers = [lmat]
    span = 2
    while span < CHUNK:
        powers.append(_dot3(powers[-1], powers[-1]))
        span *= 2
    y = rhs
    for p in reversed(powers[1:]):
        y = y + _dot3(p, y)
    return y - _dot3(lmat, y)


def _gdn_kernel(qkv_ref, ba_ref, gate_ref, conv_ref, par_ref, gain_ref, bd_ref, o_ref,
                q_s, k_s, v_s, bl_s, blt_s, of_s, ob_s, st_s):
    nt = qkv_ref.shape[1]
    n_tiles = nt // TOKEN_TILE
    n_chunks = nt // CHUNK
    n_ctx_chunks = CTX_LEN // CHUNK
    w = C_HEADS * HEAD_DIM
    bd = bd_ref[...]
    lane = lax.broadcasted_iota(jnp.int32, (TOKEN_TILE, LANES), 1)
    row = lax.broadcasted_iota(jnp.int32, (TOKEN_TILE, 1), 0)
    neg_rate = -jnp.exp(par_ref[0:1, :])
    dt_bias = par_ref[1:2, :]
    w_prev, w_mid, w_next = conv_ref[0:1, :], conv_ref[1:2, :], conv_ref[2:3, :]

    def prep(i, carry):
        r0 = pl.multiple_of(i * TOKEN_TILE, TOKEN_TILE)
        x = qkv_ref[0, pl.ds(r0, TOKEN_TILE), :]
        before = qkv_ref[0, pl.ds(pl.multiple_of(jnp.maximum(r0 - SUBLANES, 0), SUBLANES), SUBLANES), :]
        after = qkv_ref[0, pl.ds(pl.multiple_of(jnp.minimum(r0 + TOKEN_TILE, nt - SUBLANES), SUBLANES), SUBLANES), :]
        first_of_seq = jnp.logical_or(i == 0, i == CTX_LEN // TOKEN_TILE)
        last_of_seq = jnp.logical_or(i == CTX_LEN // TOKEN_TILE - 1, i == n_tiles - 1)
        edge_prev = jnp.where(first_of_seq, 0.0, before[SUBLANES - 1:SUBLANES, :])
        edge_next = jnp.where(last_of_seq, 0.0, after[0:1, :])
        x_prev = jnp.where(row == 0, edge_prev, pltpu.roll(x, 1, axis=0))
        x_next = jnp.where(row == TOKEN_TILE - 1, edge_next, pltpu.roll(x, TOKEN_TILE - 1, axis=0))
        y = _silu(x_prev * w_prev + x * w_mid + x_next * w_next)
        q, k, v = y[:, 0:w], y[:, w:2 * w], y[:, 2 * w:3 * w]
        q_s[pl.ds(r0, TOKEN_TILE), :] = q * lax.rsqrt(_dot_sel(q * q, bd) + EPS) * (HEAD_DIM ** -0.5)
        k_s[pl.ds(r0, TOKEN_TILE), :] = k * lax.rsqrt(_dot_sel(k * k, bd) + EPS)
        v_s[pl.ds(r0, TOKEN_TILE), :] = v
        ba = ba_ref[0, pl.ds(r0, TOKEN_TILE), :]
        sp = ba + dt_bias
        softplus = jnp.maximum(sp, 0.0) + jnp.log1p(jnp.exp(-jnp.abs(sp)))
        bl = jnp.where(lane < C_A_LANE, jax.nn.sigmoid(ba), neg_rate * softplus)
        bl_s[pl.ds(r0, TOKEN_TILE), :] = bl
        for c in range(TOKEN_TILE // CHUNK):
            blt_s[i * (TOKEN_TILE // CHUNK) + c] = bl[c * CHUNK:(c + 1) * CHUNK, :].T[0:C_T_ROWS, :]
        return carry

    lax.fori_loop(0, n_tiles, prep, 0)

    st_s[...] = jnp.zeros(st_s.shape, F32)
    masks = _order_masks()
    mask_bf = [jnp.where(m[0], 1.0, 0.0).astype(BF16) for m in masks]
    mask_t_bf = [mask_bf[1], mask_bf[0]]

    def scan(step, carry):
        for d in range(2):
            c = _scan_chunk(step, d, n_ctx_chunks, n_chunks)
            r0 = pl.multiple_of(c * CHUNK, CHUNK)
            incl, strict = masks[d]
            last = CHUNK - 1 if d == 0 else 0
            qc = q_s[pl.ds(r0, CHUNK), :]
            kc = k_s[pl.ds(r0, CHUNK), :]
            vc = v_s[pl.ds(r0, CHUNK), :]
            bl = bl_s[pl.ds(r0, CHUNK), :]
            g_cols = _sel_dot(mask_bf[d], bl)
            blt = blt_s[c]
            bh, blo = _split(blt)
            g_rows = (jnp.dot(bh, mask_t_bf[d], preferred_element_type=F32)
                      + jnp.dot(blo, mask_t_bf[d], preferred_element_type=F32))
            outs = []
            for h in range(C_HEADS):
                sl = slice(h * HEAD_DIM, (h + 1) * HEAD_DIM)
                la = C_A_LANE + d * C_HEADS + h
                lb = C_BETA_LANE + d * C_HEADS + h
                g_col = g_cols[:, la:la + 1]
                g_row = g_rows[la:la + 1, :]
                beta = bl[:, lb:lb + 1]
                decay = jnp.exp(jnp.where(incl, g_col - g_row, NEG_BIG))
                q_h, k_h, v_h = qc[:, sl], kc[:, sl], vc[:, sl]
                kk = _dot_nt(k_h, k_h)
                lmat = jnp.where(strict, beta * kk * decay, 0.0)
                e_g = jnp.exp(g_col)
                rhs = jnp.concatenate([k_h * (beta * e_g), v_h * beta], axis=-1)
                sol = _unit_lower_solve(lmat, rhs)
                w_, u0 = sol[:, 0:HEAD_DIM], sol[:, HEAD_DIM:]
                s = st_s[d, h]
                u = u0 - _dot(w_, s)
                qk = _dot_nt(q_h, k_h) * decay
                outs.append(_dot(q_h * e_g, s) + _dot(qk, u))
                g_last = g_col[last:last + 1, :]
                st_s[d, h] = s * jnp.exp(g_last) + _dot_tn(k_h * jnp.exp(g_last - g_col), u)
            o = jnp.concatenate(outs, axis=-1)
            if d == 0:
                of_s[pl.ds(r0, CHUNK), :] = o
            else:
                ob_s[pl.ds(r0, CHUNK), :] = o
        return carry

    lax.fori_loop(0, n_chunks, scan, 0)
    _gated_out_tiles(of_s, ob_s, gate_ref, gain_ref, bd_ref, o_ref)


def _gdn_call(c_qkv, c_ba, c_g, conv_w, par, gain, ones_bd):
    nb, nt, _ = c_qkv.shape
    w = C_HEADS * HEAD_DIM
    return pl.pallas_call(
        _gdn_kernel,
        grid=(nb,),
        in_specs=[
            pl.BlockSpec((1, nt, 3 * w), lambda b: (b, 0, 0)),
            pl.BlockSpec((1, nt, LANES), lambda b: (b, 0, 0)),
            pl.BlockSpec((1, nt, w), lambda b: (b, 0, 0)),
            pl.BlockSpec((SUBLANES, 3 * w), lambda b: (0, 0)),
            pl.BlockSpec((SUBLANES, LANES), lambda b: (0, 0)),
            pl.BlockSpec((1, w), lambda b: (0, 0)),
            pl.BlockSpec((w, w), lambda b: (0, 0)),
        ],
        out_specs=pl.BlockSpec((1, nt, w), lambda b: (b, 0, 0)),
        out_shape=jax.ShapeDtypeStruct((nb, nt, w), BF16),
        scratch_shapes=[
            pltpu.VMEM((nt, w), F32), pltpu.VMEM((nt, w), F32), pltpu.VMEM((nt, w), F32),
            pltpu.VMEM((nt, LANES), F32),
            pltpu.VMEM((nt // CHUNK, C_T_ROWS, CHUNK), F32),
            pltpu.VMEM((nt, w), F32), pltpu.VMEM((nt, w), F32),
            pltpu.VMEM((2, C_HEADS, HEAD_DIM, HEAD_DIM), F32),
        ],
        compiler_params=_params(("arbitrary",)),
        name="mixer_c_gated_delta",
    )(c_qkv, c_ba, c_g, conv_w, par, gain, ones_bd)


GLA_EXP_CAP = 80.0


def _gla_kernel(qk_ref, v_ref, lr_ref, gate_ref, gw_ref, gb_ref, gain_ref, bd_ref, o_ref,
                la_s, of_s, ob_s, st_s):
    nt = qk_ref.shape[1]
    n_tiles = nt // TOKEN_TILE
    n_chunks = nt // CHUNK
    n_ctx_chunks = CTX_LEN // CHUNK
    kw = D_HEADS * D_KDIM
    gw = gw_ref[...]
    gb = gb_ref[...]

    def prep(i, carry):
        r0 = pl.multiple_of(i * TOKEN_TILE, TOKEN_TILE)
        z = _dot3(lr_ref[0, pl.ds(r0, TOKEN_TILE), :], gw) + gb
        log_sig = jnp.minimum(z, 0.0) - jnp.log1p(jnp.exp(-jnp.abs(z)))
        la_s[pl.ds(r0, TOKEN_TILE), :] = log_sig * (1.0 / GLA_TAU)
        return carry

    lax.fori_loop(0, n_tiles, prep, 0)

    st_s[...] = jnp.zeros(st_s.shape, F32)
    masks = _order_masks()
    mask_bf = [jnp.where(m[0], 1.0, 0.0).astype(BF16) for m in masks]

    def scan(step, carry):
        for d in range(2):
            c = _scan_chunk(step, d, n_ctx_chunks, n_chunks)
            r0 = pl.multiple_of(c * CHUNK, CHUNK)
            incl = masks[d][0]
            last = CHUNK - 1 if d == 0 else 0
            qk = qk_ref[0, pl.ds(r0, CHUNK), :]
            q = qk[:, 0:kw] * (D_KDIM ** -0.5)
            k = qk[:, kw:2 * kw]
            v = v_ref[0, pl.ds(r0, CHUNK), :]
            g = _sel_dot(mask_bf[d], la_s[pl.ds(r0, CHUNK), d * kw:(d + 1) * kw])
            g_mid = g[CHUNK // 2:CHUNK // 2 + 1, :]
            g_last = g[last:last + 1, :]
            q_t = q * jnp.exp(jnp.minimum(g - g_mid, GLA_EXP_CAP))
            k_t = k * jnp.exp(jnp.minimum(g_mid - g, GLA_EXP_CAP))
            q_g = q * jnp.exp(g)
            k_d = k * jnp.exp(g_last - g)
            e_last = jnp.exp(g_last)
            outs = []
            for h in range(D_HEADS):
                ks = slice(h * D_KDIM, (h + 1) * D_KDIM)
                vs = slice(h * HEAD_DIM, (h + 1) * HEAD_DIM)
                a = jnp.where(incl, _dot_nt(q_t[:, ks], k_t[:, ks]), 0.0)
                st = st_s[d, h]
                outs.append(_dot_nt(q_g[:, ks], st) + _dot(a, v[:, vs]))
                st_s[d, h] = st * e_last[:, ks] + _dot_tn(v[:, vs], k_d[:, ks])
            o = jnp.concatenate(outs, axis=-1)
            if d == 0:
                of_s[pl.ds(r0, CHUNK), :] = o
            else:
                ob_s[pl.ds(r0, CHUNK), :] = o
        return carry

    lax.fori_loop(0, n_chunks, scan, 0)
    _gated_out_tiles(of_s, ob_s, gate_ref, gain_ref, bd_ref, o_ref)


def _gla_call(d_qk, d_v, d_lr, d_g, gw_blk, gb_row, gain, ones_bd):
    nb, nt, _ = d_qk.shape
    vw = D_HEADS * HEAD_DIM
    kw2 = 2 * D_HEADS * D_KDIM
    return pl.pallas_call(
        _gla_kernel,
        grid=(nb,),
        in_specs=[
            pl.BlockSpec((1, nt, kw2), lambda b: (b, 0, 0)),
            pl.BlockSpec((1, nt, vw), lambda b: (b, 0, 0)),
            pl.BlockSpec((1, nt, LANES), lambda b: (b, 0, 0)),
            pl.BlockSpec((1, nt, vw), lambda b: (b, 0, 0)),
            pl.BlockSpec((LANES, kw2), lambda b: (0, 0)),
            pl.BlockSpec((1, kw2), lambda b: (0, 0)),
            pl.BlockSpec((1, vw), lambda b: (0, 0)),
            pl.BlockSpec((vw, vw), lambda b: (0, 0)),
        ],
        out_specs=pl.BlockSpec((1, nt, vw), lambda b: (b, 0, 0)),
        out_shape=jax.ShapeDtypeStruct((nb, nt, vw), BF16),
        scratch_shapes=[
            pltpu.VMEM((nt, kw2), F32),
            pltpu.VMEM((nt, vw), F32), pltpu.VMEM((nt, vw), F32),
            pltpu.VMEM((2, D_HEADS, HEAD_DIM, D_KDIM), F32),
        ],
        compiler_params=_params(("arbitrary",)),
        name="mixer_d_gla",
    )(d_qk, d_v, d_lr, d_g, gw_blk, gb_row, gain, ones_bd)


ROUTE_E1, ROUTE_E2, ROUTE_W1, ROUTE_W2 = 0, 1, 2, 3
ROUTER_EXPERT_LANE = N_GROUPS


def _route(logits):
    lane = lax.broadcasted_iota(jnp.int32, logits.shape, 1).astype(F32)
    far = float(LANES)
    in_grp = lane < N_GROUPS
    lg = jnp.where(in_grp, logits, NEG_BIG)
    mg = jnp.max(lg, axis=-1, keepdims=True)
    grp = jnp.min(jnp.where(lg == mg, lane, far), axis=-1, keepdims=True)
    p_grp = 1.0 / jnp.sum(jnp.where(in_grp, jnp.exp(lg - mg), 0.0), axis=-1, keepdims=True)
    lo = ROUTER_EXPERT_LANE + EXP_PER_GROUP * grp
    in_exp = jnp.logical_and(lane >= lo, lane < lo + EXP_PER_GROUP)
    le = jnp.where(in_exp, logits, NEG_BIG)
    m1 = jnp.max(le, axis=-1, keepdims=True)
    i1 = jnp.min(jnp.where(le == m1, lane, far), axis=-1, keepdims=True)
    le2 = jnp.where(lane == i1, NEG_BIG, le)
    m2 = jnp.max(le2, axis=-1, keepdims=True)
    i2 = jnp.min(jnp.where(le2 == m2, lane, far), axis=-1, keepdims=True)
    e2 = jnp.exp(m2 - m1)
    w1 = p_grp / (1.0 + e2)
    w2 = p_grp * e2 / (1.0 + e2)
    out = jnp.where(lane == ROUTE_E1, i1 - ROUTER_EXPERT_LANE, 0.0)
    out = jnp.where(lane == ROUTE_E2, i2 - ROUTER_EXPERT_LANE, out)
    out = jnp.where(lane == ROUTE_W1, w1, out)
    return jnp.where(lane == ROUTE_W2, w2, out)


def _merge_kernel(x_ref, mod_ref, oa_ref, ob_ref, oc_ref, od_ref, wg_ref, wbr_ref, wo_ref,
                  lng_ref, lnb_ref, wr_ref, br_ref, x1_ref, h2_ref, route_ref):
    x = x_ref[0]
    mod = mod_ref[0]
    h = (x * (1.0 + mod[1:2]) + mod[0:1]).astype(BF16)
    m = None
    for z, o_ref in enumerate((oa_ref, ob_ref, oc_ref, od_ref)):
        gate = jax.nn.sigmoid(jnp.dot(h, wg_ref[:, z * D_MODEL:(z + 1) * D_MODEL], preferred_element_type=F32))
        up = jnp.dot(o_ref[0], wbr_ref[z], preferred_element_type=F32)
        m = gate * up if m is None else m + gate * up
    y = jnp.dot(m.astype(BF16), wo_ref[...], preferred_element_type=F32)
    x1 = _layer_norm(DN_ALPHA * x + mod[2:3] * y, lng_ref[...], lnb_ref[...])
    x1_ref[0] = x1
    h2 = x1 * (1.0 + mod[4:5]) + mod[3:4]
    h2_ref[0] = h2
    route_ref[0] = _route(_dot3(h2, wr_ref[...]) + br_ref[...])


def _merge_call(xa, mods, oa, ob, oc, od, wg, wbr, wo, ln_g, ln_b, wr, br):
    nb, nt, d = xa.shape
    tiles = nt // TOKEN_TILE
    bw = oa.shape[-1]
    tok = lambda b, t: (b, t, 0)
    const2 = lambda b, t: (0, 0)
    return pl.pallas_call(
        _merge_kernel,
        grid=(nb, tiles),
        in_specs=[
            pl.BlockSpec((1, TOKEN_TILE, d), tok),
            pl.BlockSpec((1, SUBLANES, d), _mod_index(nb)),
            pl.BlockSpec((1, TOKEN_TILE, bw), tok), pl.BlockSpec((1, TOKEN_TILE, bw), tok),
            pl.BlockSpec((1, TOKEN_TILE, bw), tok), pl.BlockSpec((1, TOKEN_TILE, bw), tok),
            pl.BlockSpec((d, N_BRANCH * d), const2),
            pl.BlockSpec((N_BRANCH, bw, d), lambda b, t: (0, 0, 0)),
            pl.BlockSpec((d, d), const2),
            pl.BlockSpec((1, d), const2), pl.BlockSpec((1, d), const2),
            pl.BlockSpec((d, LANES), const2), pl.BlockSpec((1, LANES), const2),
        ],
        out_specs=[pl.BlockSpec((1, TOKEN_TILE, d), tok), pl.BlockSpec((1, TOKEN_TILE, d), tok),
                   pl.BlockSpec((1, TOKEN_TILE, LANES), tok)],
        out_shape=[jax.ShapeDtypeStruct((nb, nt, d), F32), jax.ShapeDtypeStruct((nb, nt, d), F32),
                   jax.ShapeDtypeStruct((nb, nt, LANES), F32)],
        compiler_params=_params(("arbitrary", "arbitrary")),
        name="merge_out_ln1_router",
    )(xa, mods, oa, ob, oc, od, wg, wbr, wo, ln_g, ln_b, wr, br)


MOE_UNROLL = 4


def _moe_chunk(total):
    return max(c for c in range(TOKEN_TILE, MOE_CHUNK + 1, TOKEN_TILE) if total % c == 0)


def _moe_plan_len(chunk):
    return chunk * 2 + N_EXPERTS * MOE_ROWS


def _moe_kernel(ntile_ref, poff_ref, idx_ref, wt_ref, x_ref, wg_ref, wu_ref, wd_ref, y_ref, xt_s, ot_s):
    c = pl.program_id(0)
    e = pl.program_id(1)
    chunk = x_ref.shape[1] // ROW_VREGS

    @pl.when(e == 0)
    def _zero():
        y_ref[...] = jnp.zeros(y_ref.shape, F32)

    n_tiles = ntile_ref[c * N_EXPERTS + e]
    poff = poff_ref[c * N_EXPERTS + e]

    def slab(tok):
        return pl.ds(pl.multiple_of(tok * ROW_VREGS, ROW_VREGS), ROW_VREGS)

    def tile_body(i, carry):
        base = poff + i * MOE_ROWS
        for mi in range(MOE_ROWS):
            tok = jnp.minimum(idx_ref[0, 0, base + mi], chunk - 1)
            xt_s[pl.ds(mi, ROW_VREGS, stride=MOE_STRIDE), :] = x_ref[0, slab(tok), :]
        xt = jnp.concatenate([xt_s[j * MOE_STRIDE:j * MOE_STRIDE + MOE_ROWS, :] for j in range(ROW_VREGS)],
                             axis=-1).astype(BF16)
        a = _silu(jnp.dot(xt, wg_ref[0], preferred_element_type=F32)) * jnp.dot(xt, wu_ref[0], preferred_element_type=F32)
        out = jnp.dot(a.astype(BF16), wd_ref[0], preferred_element_type=F32)
        for j in range(ROW_VREGS):
            ot_s[j * MOE_STRIDE:j * MOE_STRIDE + MOE_ROWS, :] = out[:, j * LANES:(j + 1) * LANES]
        for m0 in range(0, MOE_ROWS, MOE_UNROLL):
            pending = []
            for mi in range(m0, m0 + MOE_UNROLL):
                rows = slab(idx_ref[0, 0, base + mi])
                upd = y_ref[0, rows, :] + wt_ref[0, 0, base + mi] * ot_s[pl.ds(mi, ROW_VREGS, stride=MOE_STRIDE), :]
                pending.append((rows, upd))
            for rows, upd in pending:
                y_ref[0, rows, :] = upd
        return carry

    lax.fori_loop(0, n_tiles, tile_body, 0)


def _moe_call(h2_slab, n_tiles, poff, idx_pad, w_pad, wg, wu, wd):
    nch, rows_in, _ = h2_slab.shape
    plan = idx_pad.shape[2]
    rows_out = rows_in + SUBLANES * ROW_VREGS
    grid_spec = pltpu.PrefetchScalarGridSpec(
        num_scalar_prefetch=2,
        grid=(nch, N_EXPERTS),
        in_specs=[
            pl.BlockSpec((1, 1, plan), lambda c, e, *_: (c, 0, 0), memory_space=pltpu.SMEM),
            pl.BlockSpec((1, 1, plan), lambda c, e, *_: (c, 0, 0), memory_space=pltpu.SMEM),
            pl.BlockSpec((1, rows_in, LANES), lambda c, e, *_: (c, 0, 0)),
            pl.BlockSpec((1, D_MODEL, EXP_HIDDEN), lambda c, e, *_: (e, 0, 0)),
            pl.BlockSpec((1, D_MODEL, EXP_HIDDEN), lambda c, e, *_: (e, 0, 0)),
            pl.BlockSpec((1, EXP_HIDDEN, D_MODEL), lambda c, e, *_: (e, 0, 0)),
        ],
        out_specs=pl.BlockSpec((1, rows_out, LANES), lambda c, e, *_: (c, 0, 0)),
        scratch_shapes=[pltpu.VMEM((ROW_VREGS * MOE_STRIDE, LANES), F32),
                        pltpu.VMEM((ROW_VREGS * MOE_STRIDE, LANES), F32)],
    )
    return pl.pallas_call(
        _moe_kernel,
        grid_spec=grid_spec,
        out_shape=jax.ShapeDtypeStruct((nch, rows_out, LANES), F32),
        compiler_params=_params(("arbitrary", "arbitrary")),
        name="moe_experts",
    )(n_tiles, poff, idx_pad, w_pad, h2_slab, wg, wu, wd)


def _moe_plan(route, chunk):
    t = route.shape[0]
    nch = t // chunk
    per = chunk * 2
    plan = _moe_plan_len(chunk)
    eid = route[:, ROUTE_E1:ROUTE_E2 + 1].astype(jnp.int32).reshape(nch, per)
    wts = route[:, ROUTE_W1:ROUTE_W2 + 1].reshape(nch, per)
    tok = jnp.broadcast_to(jnp.arange(per, dtype=jnp.int32) // 2, (nch, per))
    onehot = (eid[..., None] == jnp.arange(N_EXPERTS, dtype=jnp.int32)).astype(jnp.int32)
    csum = jnp.cumsum(onehot, axis=1)
    rank = jnp.take_along_axis(csum, eid[..., None], axis=2)[..., 0] - 1
    n_tiles = (csum[:, -1, :] + MOE_ROWS - 1) // MOE_ROWS
    padded = n_tiles * MOE_ROWS
    poff = jnp.cumsum(padded, axis=1) - padded
    dest = jnp.take_along_axis(poff, eid, axis=1) + rank
    chunk_ix = jnp.arange(nch, dtype=jnp.int32)[:, None]
    idx_pad = jnp.full((nch, plan), chunk, jnp.int32).at[chunk_ix, dest].set(tok)
    w_pad = jnp.zeros((nch, plan), F32).at[chunk_ix, dest].set(wts)
    return (n_tiles.reshape(-1).astype(jnp.int32), poff.reshape(-1).astype(jnp.int32),
            idx_pad.reshape(nch, 1, plan), w_pad.reshape(nch, 1, plan))


def _ln2_kernel(x_ref, y_ref, mod_ref, g_ref, b_ref, o_ref):
    mod = mod_ref[0]
    o_ref[0] = _layer_norm(DN_ALPHA * x_ref[0] + mod[5:6] * y_ref[0], g_ref[...], b_ref[...])


def _ln2_call(x1, y, mods, ln_g, ln_b):
    nb, nt, d = x1.shape
    tok = lambda b, t: (b, t, 0)
    return pl.pallas_call(
        _ln2_kernel,
        grid=(nb, nt // TOKEN_TILE),
        in_specs=[pl.BlockSpec((1, TOKEN_TILE, d), tok), pl.BlockSpec((1, TOKEN_TILE, d), tok),
                  pl.BlockSpec((1, SUBLANES, d), _mod_index(nb)),
                  pl.BlockSpec((1, d), lambda b, t: (0, 0)), pl.BlockSpec((1, d), lambda b, t: (0, 0))],
        out_specs=pl.BlockSpec((1, TOKEN_TILE, d), tok),
        out_shape=jax.ShapeDtypeStruct((nb, nt, d), F32),
        compiler_params=_params(("arbitrary", "arbitrary")),
        name="moe_residual_ln2",
    )(x1, y, mods, ln_g, ln_b)


def _head_constants():
    w = A_HEADS * HEAD_DIM
    i = np.arange(w)
    ones_bd = (i[:, None] // HEAD_DIM == i[None, :] // HEAD_DIM).astype(np.float32)
    quarter = HEAD_DIM // 4
    rot = np.zeros((w, w), np.float32)
    first = (i % (2 * quarter)) < quarter
    rot[i[first] + quarter, i[first]] = -1.0
    rot[i[~first] - quarter, i[~first]] = 1.0
    return jnp.asarray(ones_bd, BF16), jnp.asarray(rot, BF16)


def _rope_tables(seq):
    t = jnp.arange(seq, dtype=jnp.int32)
    row = (t // GRID_W).astype(F32)
    col = (t % GRID_W).astype(F32)
    nf = HEAD_DIM // 4
    inv = ROPE_THETA ** (-jnp.arange(nf, dtype=F32) / nf)
    ang_r = row[:, None] * inv
    ang_c = col[:, None] * inv
    cos = jnp.concatenate([jnp.cos(ang_r), jnp.cos(ang_r), jnp.cos(ang_c), jnp.cos(ang_c)], axis=-1)
    sin = jnp.concatenate([jnp.sin(ang_r), jnp.sin(ang_r), jnp.sin(ang_c), jnp.sin(ang_c)], axis=-1)
    return jnp.tile(cos, (1, A_HEADS)), jnp.tile(sin, (1, A_HEADS))


def _in_weight(w_in_l):
    cols = []
    for _, parts, width, _ in IN_GROUPS:
        got = 0
        for p in parts:
            o, n = _IN_OFFS[p]
            cols.append(w_in_l[:, o:o + n])
            got += n
        if got < width:
            cols.append(jnp.zeros((w_in_l.shape[0], width - got), w_in_l.dtype))
    return jnp.concatenate(cols, axis=1).astype(BF16)


def _lane_row(vec, width, offset=0):
    return jnp.zeros((1, width), F32).at[0, offset:offset + vec.shape[0]].set(vec.astype(F32))


def kernel(x, c, ctx, c_ctx, w_ada, b_ada, w_in, a_q_gain, a_k_gain, b_rpb, c_conv, c_a_log, c_dt_bias, c_out_gain, d_gate_w, d_gate_b, d_out_gain, w_branch, w_out, ln1_g, ln1_b, ln2_g, ln2_b, w_router_g, b_router_g, w_router_e, b_router_e, w_up, w_gate, w_down):
    nb, seq, d = x.shape
    depth = w_ada.shape[0]
    nt = CTX_LEN + seq
    assert d == D_MODEL and ctx.shape[1] == CTX_LEN and nb + 1 <= ADA_ROWS
    assert seq % TOKEN_TILE == 0
    chunk = _moe_chunk(nb * nt)

    xa = jnp.concatenate([ctx, x], axis=1)
    cc = jnp.zeros((ADA_ROWS, d), F32).at[:nb].set(c).at[nb].set(c_ctx)
    mods = _ada_call(cc, w_ada, b_ada).reshape(depth, ADA_ROWS, 6, d)[:, :nb + 1]
    mods = jnp.pad(mods, ((0, 0), (0, 0), (0, SUBLANES - 6), (0, 0)))

    ones_bd, rot_m = _head_constants()
    cos_t, sin_t = _rope_tables(seq)
    gates_off = _IN_OFFS['gates'][0]

    for l in range(depth):
        proj = dict(zip([g[0] for g in IN_GROUPS], _in_call(xa, mods[l], _in_weight(w_in[l]))))

        oa = _attn_a_call(proj['a_q'], proj['a_kv'], cos_t, sin_t,
                          jnp.tile(a_q_gain[l], A_HEADS)[None, :], jnp.tile(a_k_gain[l], A_KV_HEADS)[None, :],
                          ones_bd, rot_m)
        ob = _attn_b_call(proj['b_q'], proj['b_k'], proj['b_v'], _nb_bias_table(b_rpb[l]))
        conv_w = jnp.pad(c_conv[l], ((0, SUBLANES - c_conv.shape[1]), (0, 0)))
        par = jnp.concatenate([_lane_row(c_a_log[l].reshape(-1), LANES, C_A_LANE),
                               _lane_row(c_dt_bias[l].reshape(-1), LANES, C_A_LANE),
                               jnp.zeros((SUBLANES - 2, LANES), F32)], axis=0)
        oc = _gdn_call(proj['c_qkv'], proj['c_ba'], proj['c_g'], conv_w, par,
                       jnp.tile(c_out_gain[l], C_HEADS)[None, :], ones_bd)
        kw = D_HEADS * D_KDIM
        gw_blk = jnp.zeros((LANES, 2 * kw), F32)
        gw_blk = gw_blk.at[0:D_GATE_RANK, 0:kw].set(d_gate_w[l, 0])
        gw_blk = gw_blk.at[D_GATE_RANK:2 * D_GATE_RANK, kw:2 * kw].set(d_gate_w[l, 1])
        od = _gla_call(proj['d_qk'], proj['d_v'], proj['d_lr'], proj['d_g'], gw_blk,
                       d_gate_b[l].reshape(1, 2 * kw), jnp.tile(d_out_gain[l], D_HEADS)[None, :], ones_bd)

        wr = jnp.concatenate([w_router_g[l], jnp.transpose(w_router_e[l], (1, 0, 2)).reshape(d, N_EXPERTS)], axis=1)
        wr = jnp.pad(wr, ((0, 0), (0, LANES - wr.shape[1])))
        br = _lane_row(jnp.concatenate([b_router_g[l], b_router_e[l].reshape(-1)]), LANES)
        x1, h2, route = _merge_call(
            xa, mods[l], oa, ob, oc, od, w_in[l][:, gates_off:].astype(BF16), w_branch[l].astype(BF16),
            w_out[l].astype(BF16), ln1_g[l][None, :], ln1_b[l][None, :], wr, br)

        n_tiles, poff, idx_pad, w_pad = _moe_plan(route.reshape(nb * nt, LANES), chunk)
        h2_slab = h2.reshape((nb * nt) // chunk, chunk * ROW_VREGS, LANES)
        y = _moe_call(h2_slab, n_tiles, poff, idx_pad, w_pad,
                      w_gate[l].astype(BF16), w_up[l].astype(BF16), w_down[l].astype(BF16))
        y = y[:, :chunk * ROW_VREGS].reshape(nb, nt, d)
        xa = _ln2_call(x1, y, mods[l], ln2_g[l][None, :], ln2_b[l][None, :])

    return xa[:, CTX_LEN:, :]
```

```python
import functools
import math

import numpy as np
import jax
import jax.numpy as jnp
from jax import lax
from jax.experimental import pallas as pl
from jax.experimental.pallas import tpu as pltpu

F32 = jnp.float32
BF16 = jnp.bfloat16

D_MODEL = 1024
DEPTH = 4
GRID_W = 64
CTX_LEN = 256
HEAD_DIM = 64
A_HEADS = 4
A_KV_HEADS = 2
ROPE_THETA = 10000.0
B_HEADS = 4
WIN_R = 8
WIN_C = 16
C_HEADS = 4
D_HEADS = 4
D_KDIM = 32
D_GATE_RANK = 16
GLA_TAU = 16.0
CHUNK = 64
N_BRANCH = 4
N_GROUPS = 4
EXP_PER_GROUP = 8
N_EXPERTS = N_GROUPS * EXP_PER_GROUP
EXP_HIDDEN = 512
EPS = 1e-6
DN_ALPHA = (2.0 * DEPTH) ** 0.25
NEG_BIG = -1e30

LANES = 128
SUBLANES = 8
TOKEN_TILE = 256
VMEM_LIMIT = 56 * 1024 * 1024

_IN_OFFS = {}
_off = 0
for _n, _w in (('a_q', 256), ('a_k', 128), ('a_v', 128), ('b_q', 256), ('b_k', 256), ('b_v', 256),
               ('c_qkv', 768), ('c_beta', 8), ('c_a', 8), ('c_g', 256), ('d_q', 128), ('d_k', 128),
               ('d_v', 256), ('d_lr', 32), ('d_g', 256), ('gates', 4096)):
    _IN_OFFS[_n] = (_off, _w)
    _off += _w
IN_GROUPS = (
    ('a_q', ('a_q',), 256, BF16),
    ('a_kv', ('a_k', 'a_v'), 256, BF16),
    ('b_q', ('b_q',), 256, BF16),
    ('b_k', ('b_k',), 256, BF16),
    ('b_v', ('b_v',), 256, BF16),
    ('c_qkv', ('c_qkv',), 768, BF16),
    ('c_ba', ('c_beta', 'c_a'), 128, F32),
    ('c_g', ('c_g',), 256, F32),
    ('d_qk', ('d_q', 'd_k'), 256, F32),
    ('d_v', ('d_v',), 256, F32),
    ('d_lr', ('d_lr',), 128, F32),
    ('d_g', ('d_g',), 256, F32),
)
IN_TOTAL = sum(g[2] for g in IN_GROUPS)

MOE_CHUNK = 2048
MOE_ROWS = 128
MOE_STRIDE = MOE_ROWS + SUBLANES
ROW_VREGS = D_MODEL // LANES


def _dot(a, b):
    return jnp.dot(a.astype(BF16), b.astype(BF16), preferred_element_type=F32)


def _dot_nt(a, b):
    return lax.dot_general(a.astype(BF16), b.astype(BF16), (((1,), (1,)), ((), ())),
                           preferred_element_type=F32)


def _dot_tn(a, b):
    return lax.dot_general(a.astype(BF16), b.astype(BF16), (((0,), (0,)), ((), ())),
                           preferred_element_type=F32)


def _split(x):
    hi = x.astype(BF16)
    lo = (x - hi.astype(F32)).astype(BF16)
    return hi, lo


def _dot3(a, b):
    ah, al = _split(a)
    bh, bl = _split(b)
    return (jnp.dot(ah, bh, preferred_element_type=F32) + jnp.dot(al, bh, preferred_element_type=F32)
            + jnp.dot(ah, bl, preferred_element_type=F32))


def _dot_sel(a, m):
    ah, al = _split(a)
    return jnp.dot(ah, m, preferred_element_type=F32) + jnp.dot(al, m, preferred_element_type=F32)


def _silu(x):
    return x * jax.nn.sigmoid(x)


def _layer_norm(r, g, b):
    mu = jnp.mean(r, axis=-1, keepdims=True)
    d = r - mu
    var = jnp.mean(d * d, axis=-1, keepdims=True)
    return d * lax.rsqrt(var + EPS) * g + b


def _params(sem):
    return pltpu.CompilerParams(dimension_semantics=sem, vmem_limit_bytes=VMEM_LIMIT)


ADA_ROWS = 24
ADA_TILE = 1536


def _ada_kernel(cc_ref, w_ref, b_ref, o_ref):
    s = _silu(cc_ref[...])
    o_ref[0] = _dot3(s, w_ref[0]) + b_ref[0]


def _ada_call(cc, w_ada, b_ada):
    depth = w_ada.shape[0]
    n = w_ada.shape[2]
    return pl.pallas_call(
        _ada_kernel,
        grid=(depth, n // ADA_TILE),
        in_specs=[
            pl.BlockSpec((ADA_ROWS, D_MODEL), lambda l, j: (0, 0)),
            pl.BlockSpec((1, D_MODEL, ADA_TILE), lambda l, j: (l, 0, j)),
            pl.BlockSpec((1, 1, ADA_TILE), lambda l, j: (l, 0, j)),
        ],
        out_specs=pl.BlockSpec((1, ADA_ROWS, ADA_TILE), lambda l, j: (l, 0, j)),
        out_shape=jax.ShapeDtypeStruct((depth, ADA_ROWS, n), F32),
        compiler_params=_params(("arbitrary", "arbitrary")),
        name="ada_mod",
    )(cc, w_ada, b_ada.reshape(depth, 1, n))


def _mod_index(nb):
    return lambda b, t: (jnp.where(t == 0, nb, b), 0, 0)


def _in_kernel(x_ref, mod_ref, w_ref, *out_refs):
    x = x_ref[0]
    mod = mod_ref[0]
    h = (x * (1.0 + mod[1:2]) + mod[0:1]).astype(BF16)
    off = 0
    for (name, _, width, dt), o_ref in zip(IN_GROUPS, out_refs):
        o_ref[0] = jnp.dot(h, w_ref[:, off:off + width], preferred_element_type=F32).astype(dt)
        off += width


def _in_call(xa, mods, w_cat):
    nb, nt, _ = xa.shape
    tiles = nt // TOKEN_TILE
    return pl.pallas_call(
        _in_kernel,
        grid=(nb, tiles),
        in_specs=[
            pl.BlockSpec((1, TOKEN_TILE, D_MODEL), lambda b, t: (b, t, 0)),
            pl.BlockSpec((1, SUBLANES, D_MODEL), _mod_index(nb)),
            pl.BlockSpec((D_MODEL, IN_TOTAL), lambda b, t: (0, 0)),
        ],
        out_specs=[pl.BlockSpec((1, TOKEN_TILE, g[2]), lambda b, t: (b, t, 0)) for g in IN_GROUPS],
        out_shape=[jax.ShapeDtypeStruct((nb, nt, g[2]), g[3]) for g in IN_GROUPS],
        compiler_params=_params(("arbitrary", "arbitrary")),
        name="in_proj",
    )(xa, mods, w_cat)


def _head_rms(x, ones_bd, gain):
    ss = _dot_sel(x * x, ones_bd)
    return x * lax.rsqrt(ss * (1.0 / HEAD_DIM) + EPS) * gain


def _rope(x, rot, cos, sin):
    return x * cos + _dot_sel(x, rot) * sin


def _attn_a_kernel(q_ref, kv_ref, cos_ref, sin_ref, qg_ref, kg_ref, bd_ref, rot_ref, o_ref, kp_ref):
    t = pl.program_id(1)
    kvw = A_KV_HEADS * HEAD_DIM
    n_lat_tiles = (kv_ref.shape[1] - CTX_LEN) // TOKEN_TILE
    scale = HEAD_DIM ** -0.5

    @pl.when(t == 0)
    def _prep_keys():
        bd = bd_ref[0:kvw, 0:kvw]
        rot = rot_ref[0:kvw, 0:kvw]
        kg = kg_ref[...]
        kc = kv_ref[0, 0:CTX_LEN, 0:kvw].astype(F32)
        kp_ref[0:CTX_LEN, :] = _head_rms(kc, bd, kg).astype(BF16)

        def body(i, carry):
            r0 = pl.multiple_of(i * TOKEN_TILE, TOKEN_TILE)
            k = kv_ref[0, pl.ds(CTX_LEN + r0, TOKEN_TILE), 0:kvw].astype(F32)
            kn = _head_rms(k, bd, kg)
            kr = _rope(kn, rot, cos_ref[pl.ds(r0, TOKEN_TILE), 0:kvw], sin_ref[pl.ds(r0, TOKEN_TILE), 0:kvw])
            kp_ref[pl.ds(CTX_LEN + r0, TOKEN_TILE), :] = kr.astype(BF16)
            return carry

        lax.fori_loop(0, n_lat_tiles, body, 0)

    qn = _head_rms(q_ref[0].astype(F32), bd_ref[...], qg_ref[...])

    def attend(qh, nk):
        outs = []
        for g in range(A_KV_HEADS):
            kg_ = kp_ref[0:nk, g * HEAD_DIM:(g + 1) * HEAD_DIM]
            vg_ = kv_ref[0, 0:nk, kvw + g * HEAD_DIM:kvw + (g + 1) * HEAD_DIM]
            for r in range(A_HEADS // A_KV_HEADS):
                h = g * (A_HEADS // A_KV_HEADS) + r
                qq = (qh[:, h * HEAD_DIM:(h + 1) * HEAD_DIM] * scale).astype(BF16)
                s = _dot_nt(qq, kg_)
                m = jnp.max(s, axis=-1, keepdims=True)
                e = jnp.exp(s - m)
                l = jnp.sum(e, axis=-1, keepdims=True)
                outs.append(jnp.dot(e.astype(BF16), vg_, preferred_element_type=F32) / l)
        return jnp.concatenate(outs, axis=-1)

    @pl.when(t == 0)
    def _ctx_queries():
        o_ref[0] = attend(qn, CTX_LEN).astype(o_ref.dtype)

    @pl.when(t > 0)
    def _latent_queries():
        r0 = pl.multiple_of((t - 1) * TOKEN_TILE, TOKEN_TILE)
        qr = _rope(qn, rot_ref[...], cos_ref[pl.ds(r0, TOKEN_TILE), :], sin_ref[pl.ds(r0, TOKEN_TILE), :])
        o_ref[0] = attend(qr, kv_ref.shape[1]).astype(o_ref.dtype)


def _attn_a_call(a_q, a_kv, cos_t, sin_t, q_gain, k_gain, ones_bd, rot_m):
    nb, nt, _ = a_q.shape
    tiles = nt // TOKEN_TILE
    seq = nt - CTX_LEN
    qw = A_HEADS * HEAD_DIM
    return pl.pallas_call(
        _attn_a_kernel,
        grid=(nb, tiles),
        in_specs=[
            pl.BlockSpec((1, TOKEN_TILE, qw), lambda b, t: (b, t, 0)),
            pl.BlockSpec((1, nt, qw), lambda b, t: (b, 0, 0)),
            pl.BlockSpec((seq, qw), lambda b, t: (0, 0)),
            pl.BlockSpec((seq, qw), lambda b, t: (0, 0)),
            pl.BlockSpec((1, qw), lambda b, t: (0, 0)),
            pl.BlockSpec((1, A_KV_HEADS * HEAD_DIM), lambda b, t: (0, 0)),
            pl.BlockSpec((qw, qw), lambda b, t: (0, 0)),
            pl.BlockSpec((qw, qw), lambda b, t: (0, 0)),
        ],
        out_specs=pl.BlockSpec((1, TOKEN_TILE, qw), lambda b, t: (b, t, 0)),
        out_shape=jax.ShapeDtypeStruct((nb, nt, qw), BF16),
        scratch_shapes=[pltpu.VMEM((nt, A_KV_HEADS * HEAD_DIM), BF16)],
        compiler_params=_params(("arbitrary", "arbitrary")),
        name="mixer_a_gqa",
    )(a_q, a_kv, cos_t, sin_t, q_gain, k_gain, ones_bd, rot_m)


NB_QROWS = TOKEN_TILE // GRID_W
NB_KROWS = 12
NB_INVALID = 2 * WIN_R - 1


def _attn_b_kernel(q_ref, k_ref, v_ref, bt_ref, o_ref):
    t = pl.program_id(1)
    scale = HEAD_DIM ** -0.5
    rows = (k_ref.shape[1] - CTX_LEN) // GRID_W
    wr = min(WIN_R, rows)

    def softmax_pv(parts):
        m = None
        for s, _ in parts:
            mi = jnp.max(s, axis=-1, keepdims=True)
            m = mi if m is None else jnp.maximum(m, mi)
        acc, l = None, None
        for s, v in parts:
            e = jnp.exp(s - m)
            li = jnp.sum(e, axis=-1, keepdims=True)
            oi = jnp.dot(e.astype(BF16), v, preferred_element_type=F32)
            acc = oi if acc is None else acc + oi
            l = li if l is None else l + li
        return acc / l

    @pl.when(t == 0)
    def _ctx_queries():
        outs = []
        for h in range(B_HEADS):
            sl = slice(h * HEAD_DIM, (h + 1) * HEAD_DIM)
            qq = (q_ref[0, :, sl].astype(F32) * scale).astype(BF16)
            s = _dot_nt(qq, k_ref[0, 0:CTX_LEN, sl])
            outs.append(softmax_pv([(s, v_ref[0, 0:CTX_LEN, sl])]))
        o_ref[0] = jnp.concatenate(outs, axis=-1).astype(o_ref.dtype)

    @pl.when(t > 0)
    def _latent_queries():
        r0 = (t - 1) * NB_QROWS
        start = jnp.clip(r0 - wr // 2, 0, rows - NB_KROWS)
        k0 = pl.multiple_of(CTX_LEN + start * GRID_W, GRID_W)
        nk = NB_KROWS * GRID_W
        lane = lax.broadcasted_iota(jnp.int32, (GRID_W, 2 * GRID_W), 1)
        left = lane < GRID_W
        slots = []
        for i in range(NB_QROWS):
            r = r0 + i
            rs = jnp.clip(r - wr // 2, 0, rows - wr)
            row_slots = []
            for j in range(NB_KROWS):
                kr = start + j
                ok = jnp.logical_and(kr >= rs, kr < rs + wr)
                row_slots.append(jnp.where(ok, kr - r + WIN_R - 1, NB_INVALID))
            slots.append(row_slots)
        outs = []
        for h in range(B_HEADS):
            sl = slice(h * HEAD_DIM, (h + 1) * HEAD_DIM)
            qq = (q_ref[0, :, sl].astype(F32) * scale).astype(BF16)
            s_loc = _dot_nt(qq, k_ref[0, pl.ds(k0, nk), sl])
            s_ctx = _dot_nt(qq, k_ref[0, 0:CTX_LEN, sl])
            bias_rows = []
            for i in range(NB_QROWS):
                tiles = []
                for jp in range(NB_KROWS // 2):
                    b0 = bt_ref[h, slots[i][2 * jp]]
                    b1 = bt_ref[h, slots[i][2 * jp + 1]]
                    tiles.append(jnp.where(left, b0, b1))
                bias_rows.append(jnp.concatenate(tiles, axis=-1))
            s_loc = s_loc + jnp.concatenate(bias_rows, axis=0)
            outs.append(softmax_pv([(s_loc, v_ref[0, pl.ds(k0, nk), sl]), (s_ctx, v_ref[0, 0:CTX_LEN, sl])]))
        o_ref[0] = jnp.concatenate(outs, axis=-1).astype(o_ref.dtype)


def _attn_b_call(b_q, b_k, b_v, bias_tab):
    nb, nt, w = b_q.shape
    tiles = nt // TOKEN_TILE
    return pl.pallas_call(
        _attn_b_kernel,
        grid=(nb, tiles),
        in_specs=[
            pl.BlockSpec((1, TOKEN_TILE, w), lambda b, t: (b, t, 0)),
            pl.BlockSpec((1, nt, w), lambda b, t: (b, 0, 0)),
            pl.BlockSpec((1, nt, w), lambda b, t: (b, 0, 0)),
            pl.BlockSpec(bias_tab.shape, lambda b, t: (0, 0, 0, 0)),
        ],
        out_specs=pl.BlockSpec((1, TOKEN_TILE, w), lambda b, t: (b, t, 0)),
        out_shape=jax.ShapeDtypeStruct((nb, nt, w), BF16),
        compiler_params=_params(("arbitrary", "arbitrary")),
        name="mixer_b_neighbourhood",
    )(b_q, b_k, b_v, bias_tab)


def _nb_bias_table(rpb):
    cols = jnp.arange(GRID_W, dtype=jnp.int32)
    col_start = jnp.clip(cols - WIN_C // 2, 0, GRID_W - WIN_C)
    col_ok = (cols[None, :] >= col_start[:, None]) & (cols[None, :] < col_start[:, None] + WIN_C)
    dc_idx = jnp.clip(cols[None, :] - cols[:, None] + WIN_C - 1, 0, 2 * WIN_C - 2)
    tab = rpb.astype(F32)[:, :, dc_idx]
    tab = jnp.where(col_ok[None, None], tab, NEG_BIG)
    tab = jnp.concatenate([tab, jnp.full_like(tab[:, :1], NEG_BIG)], axis=1)
    return jnp.concatenate([tab, tab], axis=-1)


def _order_masks():
    i = lax.broadcasted_iota(jnp.int32, (CHUNK, CHUNK), 0)
    j = lax.broadcasted_iota(jnp.int32, (CHUNK, CHUNK), 1)
    return ((j <= i, j < i), (j >= i, j > i))


def _scan_chunk(step, direction, n_ctx_chunks, n_chunks):
    if direction == 0:
        return step
    return jnp.where(step < n_ctx_chunks, n_ctx_chunks - 1 - step, n_chunks + n_ctx_chunks - 1 - step)


def _sel_dot(m, a):
    ah, al = _split(a)
    return jnp.dot(m, ah, preferred_element_type=F32) + jnp.dot(m, al, preferred_element_type=F32)


def _gated_out_tiles(part_refs, gate_ref, gain_ref, bd_ref, o_ref):
    nt = part_refs[0].shape[0]
    bd = bd_ref[...]
    gain = gain_ref[...]

    def body(i, carry):
        r0 = pl.multiple_of(i * TOKEN_TILE, TOKEN_TILE)
        o = sum(p[pl.ds(r0, TOKEN_TILE), :] for p in part_refs)
        y = _head_rms(o, bd, gain) * _silu(gate_ref[0, pl.ds(r0, TOKEN_TILE), :])
        o_ref[0, pl.ds(r0, TOKEN_TILE), :] = y.astype(o_ref.dtype)
        return carry

    lax.fori_loop(0, nt // TOKEN_TILE, body, 0)


C_BETA_LANE = 0
C_A_LANE = 8
C_T_ROWS = 16


SOLVE_BLOCK = 16


def _unit_lower_solve(lmats, rhss, same_block, eye):
    n = range(len(lmats))
    lds = [jnp.where(same_block, l, 0.0) for l in lmats]
    ts = [eye - ld for ld in lds]
    ps = lds
    span = 2
    while span < SOLVE_BLOCK:
        ps = [_dot(p, p) for p in ps]
        ts = [ts[i] + _dot(ts[i], ps[i]) for i in n]
        span *= 2
    width = lmats[0].shape[1]
    mzs = [_dot(ts[i], jnp.concatenate([lmats[i] - lds[i], rhss[i]], axis=-1)) for i in n]
    mmzs = [_dot(mz[:, 0:width], mz) for mz in mzs]
    zs = [mzs[i][:, width:] - mmzs[i][:, width:] for i in n]
    ps = [mmz[:, 0:width] for mmz in mmzs]
    span = 2
    while span < CHUNK // SOLVE_BLOCK:
        zs = [zs[i] + _dot(ps[i], zs[i]) for i in n]
        span *= 2
        if span < CHUNK // SOLVE_BLOCK:
            ps = [_dot(p, p) for p in ps]
    return zs


def _gdn_kernel(qkv_ref, ba_ref, gate_ref, conv_ref, par_ref, gain_ref, bd_ref, o_ref,
                q_s, k_s, v_s, bl_s, g_s, gt_s, w_s, u0_s, qkd_s, qg_s, kdt_s, egl_s, o_s, st_s):
    nt = qkv_ref.shape[1]
    n_tiles = nt // TOKEN_TILE
    n_chunks = nt // CHUNK
    n_ctx_chunks = CTX_LEN // CHUNK
    w = C_HEADS * HEAD_DIM
    pack = 2 * SUBLANES
    bd = bd_ref[...]
    lane = lax.broadcasted_iota(jnp.int32, (TOKEN_TILE, LANES), 1)
    lane_c = lax.broadcasted_iota(jnp.int32, (CHUNK, LANES), 1)
    row = lax.broadcasted_iota(jnp.int32, (TOKEN_TILE, 1), 0)
    neg_rate = -jnp.exp(par_ref[0:1, :])
    dt_bias = par_ref[1:2, :]
    w_prev, w_mid, w_next = conv_ref[0:1, :], conv_ref[1:2, :], conv_ref[2:3, :]
    masks = _order_masks()
    mask_bf = [jnp.where(m[0], 1.0, 0.0).astype(BF16) for m in masks]

    def prep(i, carry):
        r0 = pl.multiple_of(i * TOKEN_TILE, TOKEN_TILE)
        x = qkv_ref[0, pl.ds(r0, TOKEN_TILE), :].astype(F32)
        before = qkv_ref[0, pl.ds(pl.multiple_of(jnp.maximum(r0 - pack, 0), pack), pack), :].astype(F32)
        after = qkv_ref[0, pl.ds(pl.multiple_of(jnp.minimum(r0 + TOKEN_TILE, nt - pack), pack), pack), :].astype(F32)
        first_of_seq = jnp.logical_or(i == 0, i == CTX_LEN // TOKEN_TILE)
        last_of_seq = jnp.logical_or(i == CTX_LEN // TOKEN_TILE - 1, i == n_tiles - 1)
        edge_prev = jnp.where(first_of_seq, 0.0, before[pack - 1:pack, :])
        edge_next = jnp.where(last_of_seq, 0.0, after[0:1, :])
        x_prev = jnp.where(row == 0, edge_prev, pltpu.roll(x, 1, axis=0))
        x_next = jnp.where(row == TOKEN_TILE - 1, edge_next, pltpu.roll(x, TOKEN_TILE - 1, axis=0))
        y = _silu(x_prev * w_prev + x * w_mid + x_next * w_next)
        q, k, v = y[:, 0:w], y[:, w:2 * w], y[:, 2 * w:3 * w]
        q_s[pl.ds(r0, TOKEN_TILE), :] = q * lax.rsqrt(_dot_sel(q * q, bd) + EPS) * (HEAD_DIM ** -0.5)
        k_s[pl.ds(r0, TOKEN_TILE), :] = k * lax.rsqrt(_dot_sel(k * k, bd) + EPS)
        v_s[pl.ds(r0, TOKEN_TILE), :] = v
        ba = ba_ref[0, pl.ds(r0, TOKEN_TILE), :]
        sp = ba + dt_bias
        softplus = jnp.maximum(sp, 0.0) + jnp.log1p(jnp.exp(-jnp.abs(sp)))
        bl = jnp.where(lane < C_A_LANE, jax.nn.sigmoid(ba), neg_rate * softplus)
        bl_s[pl.ds(r0, TOKEN_TILE), :] = bl
        for c in range(TOKEN_TILE // CHUNK):
            blc = bl[c * CHUNK:(c + 1) * CHUNK, :]
            g = jnp.where(lane_c < C_A_LANE + C_HEADS, _sel_dot(mask_bf[0], blc), _sel_dot(mask_bf[1], blc))
            g_s[pl.ds(r0 + c * CHUNK, CHUNK), :] = g
            gt_s[i * (TOKEN_TILE // CHUNK) + c] = g.T[0:C_T_ROWS, :]
        return carry

    lax.fori_loop(0, n_tiles, prep, 0)

    ri = lax.broadcasted_iota(jnp.int32, (CHUNK, CHUNK), 0)
    ci = lax.broadcasted_iota(jnp.int32, (CHUNK, CHUNK), 1)
    same_block = (ri // SOLVE_BLOCK) == (ci // SOLVE_BLOCK)
    eye = jnp.where(ri == ci, 1.0, 0.0)

    def local(c, carry):
        r0 = pl.multiple_of(c * CHUNK, CHUNK)
        qc = q_s[pl.ds(r0, CHUNK), :]
        kc = k_s[pl.ds(r0, CHUNK), :]
        vc = v_s[pl.ds(r0, CHUNK), :]
        bl = bl_s[pl.ds(r0, CHUNK), :]
        g_cols = g_s[pl.ds(r0, CHUNK), :]
        g_rows = gt_s[c]
        heads = [slice(h * HEAD_DIM, (h + 1) * HEAD_DIM) for h in range(C_HEADS)]
        kks = [_dot_nt(kc[:, sl], kc[:, sl]) for sl in heads]
        qks = [_dot_nt(qc[:, sl], kc[:, sl]) for sl in heads]
        lmats, rhss, rest = [], [], []
        for d in range(2):
            incl, strict = masks[d]
            last = CHUNK - 1 if d == 0 else 0
            for h, sl in enumerate(heads):
                la = C_A_LANE + d * C_HEADS + h
                lb = C_BETA_LANE + d * C_HEADS + h
                g_col = g_cols[:, la:la + 1]
                beta = bl[:, lb:lb + 1]
                decay = jnp.exp(jnp.where(incl, g_col - g_rows[la:la + 1, :], NEG_BIG))
                e_g = jnp.exp(g_col)
                g_last = g_col[last:last + 1, :]
                lmats.append(jnp.where(strict, beta * kks[h] * decay, 0.0))
                rhss.append(jnp.concatenate([kc[:, sl] * (beta * e_g), vc[:, sl] * beta], axis=-1))
                rest.append((qks[h] * decay, qc[:, sl] * e_g, kc[:, sl] * jnp.exp(g_last - g_col),
                             jnp.broadcast_to(jnp.exp(g_last), (SUBLANES, HEAD_DIM))))
        sols = _unit_lower_solve(lmats, rhss, same_block, eye)
        for d in range(2):
            u = range(d * C_HEADS, (d + 1) * C_HEADS)
            w_s[d, c] = jnp.concatenate([sols[i][:, 0:HEAD_DIM] for i in u], axis=-1).astype(BF16)
            u0_s[d, c] = jnp.concatenate([sols[i][:, HEAD_DIM:] for i in u], axis=-1)
            qkd_s[d, c] = jnp.concatenate([rest[i][0] for i in u], axis=-1).astype(BF16)
            qg_s[d, c] = jnp.concatenate([rest[i][1] for i in u], axis=-1).astype(BF16)
            kdt_s[d, c] = jnp.concatenate([rest[i][2] for i in u], axis=-1).T.astype(BF16)
            egl_s[d, c] = jnp.concatenate([rest[i][3] for i in u], axis=-1)
        return carry

    lax.fori_loop(0, n_chunks, local, 0)

    st_s[...] = jnp.zeros(st_s.shape, F32)
    o_s[...] = jnp.zeros(o_s.shape, F32)

    def scan(step, carry):
        units = [(d, h) for d in range(2) for h in range(C_HEADS)]
        cs = [_scan_chunk(step, d, n_ctx_chunks, n_chunks) for d in range(2)]
        sls = [slice(h * HEAD_DIM, (h + 1) * HEAD_DIM) for h in range(C_HEADS)]
        w_c = [w_s[d, cs[d]] for d in range(2)]
        u0_c = [u0_s[d, cs[d]] for d in range(2)]
        qkd_c = [qkd_s[d, cs[d]] for d in range(2)]
        qg_c = [qg_s[d, cs[d]] for d in range(2)]
        kdt_c = [kdt_s[d, cs[d]] for d in range(2)]
        egl_c = [egl_s[d, cs[d]] for d in range(2)]
        ss = [st_s[d, h] for d, h in units]
        sbs = [s.astype(BF16) for s in ss]
        ws_ = [jnp.dot(w_c[d][:, sls[h]], sbs[i], preferred_element_type=F32) for i, (d, h) in enumerate(units)]
        o1 = [jnp.dot(qg_c[d][:, sls[h]], sbs[i], preferred_element_type=F32) for i, (d, h) in enumerate(units)]
        us = [(u0_c[d][:, sls[h]] - ws_[i]).astype(BF16) for i, (d, h) in enumerate(units)]
        o2 = [jnp.dot(qkd_c[d][:, sls[h]], us[i], preferred_element_type=F32) for i, (d, h) in enumerate(units)]
        ds_ = [jnp.dot(kdt_c[d][sls[h], :], us[i], preferred_element_type=F32) for i, (d, h) in enumerate(units)]
        for i, (d, h) in enumerate(units):
            st_s[d, h] = ss[i] * egl_c[d][0:1, sls[h]] + ds_[i]
        for d in range(2):
            r0 = pl.multiple_of(cs[d] * CHUNK, CHUNK)
            o = jnp.concatenate([o1[d * C_HEADS + h] + o2[d * C_HEADS + h] for h in range(C_HEADS)], axis=-1)
            o_s[pl.ds(r0, CHUNK), :] = o_s[pl.ds(r0, CHUNK), :] + o
        return carry

    lax.fori_loop(0, n_chunks, scan, 0)
    _gated_out_tiles((o_s,), gate_ref, gain_ref, bd_ref, o_ref)


def _gdn_call(c_qkv, c_ba, c_g, conv_w, par, gain, ones_bd):
    nb, nt, _ = c_qkv.shape
    w = C_HEADS * HEAD_DIM
    nc = nt // CHUNK
    return pl.pallas_call(
        _gdn_kernel,
        grid=(nb,),
        in_specs=[
            pl.BlockSpec((1, nt, 3 * w), lambda b: (b, 0, 0)),
            pl.BlockSpec((1, nt, LANES), lambda b: (b, 0, 0)),
            pl.BlockSpec((1, nt, w), lambda b: (b, 0, 0)),
            pl.BlockSpec((SUBLANES, 3 * w), lambda b: (0, 0)),
            pl.BlockSpec((SUBLANES, LANES), lambda b: (0, 0)),
            pl.BlockSpec((1, w), lambda b: (0, 0)),
            pl.BlockSpec((w, w), lambda b: (0, 0)),
        ],
        out_specs=pl.BlockSpec((1, nt, w), lambda b: (b, 0, 0)),
        out_shape=jax.ShapeDtypeStruct((nb, nt, w), BF16),
        scratch_shapes=[
            pltpu.VMEM((nt, w), F32), pltpu.VMEM((nt, w), F32), pltpu.VMEM((nt, w), F32),
            pltpu.VMEM((nt, LANES), F32), pltpu.VMEM((nt, LANES), F32),
            pltpu.VMEM((nc, C_T_ROWS, CHUNK), F32),
            pltpu.VMEM((2, nc, CHUNK, w), BF16), pltpu.VMEM((2, nc, CHUNK, w), F32),
            pltpu.VMEM((2, nc, CHUNK, w), BF16), pltpu.VMEM((2, nc, CHUNK, w), BF16),
            pltpu.VMEM((2, nc, w, CHUNK), BF16),
            pltpu.VMEM((2, nc, SUBLANES, w), F32),
            pltpu.VMEM((nt, w), F32),
            pltpu.VMEM((2, C_HEADS, HEAD_DIM, HEAD_DIM), F32),
        ],
        compiler_params=_params(("arbitrary",)),
        name="mixer_c_gated_delta",
    )(c_qkv, c_ba, c_g, conv_w, par, gain, ones_bd)


GLA_EXP_CAP = 80.0


def _gla_kernel(qk_ref, v_ref, lr_ref, gate_ref, gw_ref, gb_ref, gain_ref, bd_ref, o_ref,
                la_s, of_s, ob_s, st_s):
    nt = qk_ref.shape[1]
    n_tiles = nt // TOKEN_TILE
    n_chunks = nt // CHUNK
    n_ctx_chunks = CTX_LEN // CHUNK
    kw = D_HEADS * D_KDIM
    gw = gw_ref[...]
    gb = gb_ref[...]

    def prep(i, carry):
        r0 = pl.multiple_of(i * TOKEN_TILE, TOKEN_TILE)
        z = _dot3(lr_ref[0, pl.ds(r0, TOKEN_TILE), :], gw) + gb
        log_sig = jnp.minimum(z, 0.0) - jnp.log1p(jnp.exp(-jnp.abs(z)))
        la_s[pl.ds(r0, TOKEN_TILE), :] = log_sig * (1.0 / GLA_TAU)
        return carry

    lax.fori_loop(0, n_tiles, prep, 0)

    st_s[...] = jnp.zeros(st_s.shape, F32)
    masks = _order_masks()
    mask_bf = [jnp.where(m[0], 1.0, 0.0).astype(BF16) for m in masks]

    def scan(step, carry):
        units = [(d, h) for d in range(2) for h in range(D_HEADS)]
        kss = [slice(h * D_KDIM, (h + 1) * D_KDIM) for h in range(D_HEADS)]
        vss = [slice(h * HEAD_DIM, (h + 1) * HEAD_DIM) for h in range(D_HEADS)]
        r0s, q_t, k_t, q_g, k_d, e_last, v_b, v_t = [], [], [], [], [], [], [], []
        for d in range(2):
            c = _scan_chunk(step, d, n_ctx_chunks, n_chunks)
            r0 = pl.multiple_of(c * CHUNK, CHUNK)
            last = CHUNK - 1 if d == 0 else 0
            qk = qk_ref[0, pl.ds(r0, CHUNK), :]
            q = qk[:, 0:kw] * (D_KDIM ** -0.5)
            k = qk[:, kw:2 * kw]
            v = v_ref[0, pl.ds(r0, CHUNK), :]
            g = _sel_dot(mask_bf[d], la_s[pl.ds(r0, CHUNK), d * kw:(d + 1) * kw])
            g_mid = g[CHUNK // 2:CHUNK // 2 + 1, :]
            g_last = g[last:last + 1, :]
            r0s.append(r0)
            q_t.append((q * jnp.exp(jnp.minimum(g - g_mid, GLA_EXP_CAP))).astype(BF16))
            k_t.append((k * jnp.exp(jnp.minimum(g_mid - g, GLA_EXP_CAP))).astype(BF16))
            q_g.append((q * jnp.exp(g)).astype(BF16))
            k_d.append((k * jnp.exp(g_last - g)).astype(BF16))
            e_last.append(jnp.exp(g_last))
            v_b.append(v.astype(BF16))
            v_t.append(v.T.astype(BF16))
        sts = [st_s[d, h] for d, h in units]
        a_ = [jnp.where(masks[d][0], _dot_nt(q_t[d][:, kss[h]], k_t[d][:, kss[h]]), 0.0).astype(BF16)
              for d, h in units]
        o1 = [_dot_nt(q_g[d][:, kss[h]], sts[i]) for i, (d, h) in enumerate(units)]
        ds_ = [jnp.dot(v_t[d][vss[h], :], k_d[d][:, kss[h]], preferred_element_type=F32) for d, h in units]
        o2 = [jnp.dot(a_[i], v_b[d][:, vss[h]], preferred_element_type=F32) for i, (d, h) in enumerate(units)]
        for i, (d, h) in enumerate(units):
            st_s[d, h] = sts[i] * e_last[d][:, kss[h]] + ds_[i]
        for d, o_dir in enumerate((of_s, ob_s)):
            o_dir[pl.ds(r0s[d], CHUNK), :] = jnp.concatenate(
                [o1[d * D_HEADS + h] + o2[d * D_HEADS + h] for h in range(D_HEADS)], axis=-1)
        return carry

    lax.fori_loop(0, n_chunks, scan, 0)
    _gated_out_tiles((of_s, ob_s), gate_ref, gain_ref, bd_ref, o_ref)


def _gla_call(d_qk, d_v, d_lr, d_g, gw_blk, gb_row, gain, ones_bd):
    nb, nt, _ = d_qk.shape
    vw = D_HEADS * HEAD_DIM
    kw2 = 2 * D_HEADS * D_KDIM
    return pl.pallas_call(
        _gla_kernel,
        grid=(nb,),
        in_specs=[
            pl.BlockSpec((1, nt, kw2), lambda b: (b, 0, 0)),
            pl.BlockSpec((1, nt, vw), lambda b: (b, 0, 0)),
            pl.BlockSpec((1, nt, LANES), lambda b: (b, 0, 0)),
            pl.BlockSpec((1, nt, vw), lambda b: (b, 0, 0)),
            pl.BlockSpec((LANES, kw2), lambda b: (0, 0)),
            pl.BlockSpec((1, kw2), lambda b: (0, 0)),
            pl.BlockSpec((1, vw), lambda b: (0, 0)),
            pl.BlockSpec((vw, vw), lambda b: (0, 0)),
        ],
        out_specs=pl.BlockSpec((1, nt, vw), lambda b: (b, 0, 0)),
        out_shape=jax.ShapeDtypeStruct((nb, nt, vw), BF16),
        scratch_shapes=[
            pltpu.VMEM((nt, kw2), F32),
            pltpu.VMEM((nt, vw), F32), pltpu.VMEM((nt, vw), F32),
            pltpu.VMEM((2, D_HEADS, HEAD_DIM, D_KDIM), F32),
        ],
        compiler_params=_params(("arbitrary",)),
        name="mixer_d_gla",
    )(d_qk, d_v, d_lr, d_g, gw_blk, gb_row, gain, ones_bd)


ROUTE_E1, ROUTE_E2, ROUTE_W1, ROUTE_W2 = 0, 1, 2, 3
ROUTER_EXPERT_LANE = N_GROUPS


def _route(logits):
    lane = lax.broadcasted_iota(jnp.int32, logits.shape, 1).astype(F32)
    far = float(LANES)
    in_grp = lane < N_GROUPS
    lg = jnp.where(in_grp, logits, NEG_BIG)
    mg = jnp.max(lg, axis=-1, keepdims=True)
    grp = jnp.min(jnp.where(lg == mg, lane, far), axis=-1, keepdims=True)
    p_grp = 1.0 / jnp.sum(jnp.where(in_grp, jnp.exp(lg - mg), 0.0), axis=-1, keepdims=True)
    lo = ROUTER_EXPERT_LANE + EXP_PER_GROUP * grp
    in_exp = jnp.logical_and(lane >= lo, lane < lo + EXP_PER_GROUP)
    le = jnp.where(in_exp, logits, NEG_BIG)
    m1 = jnp.max(le, axis=-1, keepdims=True)
    i1 = jnp.min(jnp.where(le == m1, lane, far), axis=-1, keepdims=True)
    le2 = jnp.where(lane == i1, NEG_BIG, le)
    m2 = jnp.max(le2, axis=-1, keepdims=True)
    i2 = jnp.min(jnp.where(le2 == m2, lane, far), axis=-1, keepdims=True)
    e2 = jnp.exp(m2 - m1)
    w1 = p_grp / (1.0 + e2)
    w2 = p_grp * e2 / (1.0 + e2)
    out = jnp.where(lane == ROUTE_E1, i1 - ROUTER_EXPERT_LANE, 0.0)
    out = jnp.where(lane == ROUTE_E2, i2 - ROUTER_EXPERT_LANE, out)
    out = jnp.where(lane == ROUTE_W1, w1, out)
    return jnp.where(lane == ROUTE_W2, w2, out)


def _merge_kernel(x_ref, mod_ref, oa_ref, ob_ref, oc_ref, od_ref, wg_ref, wbr_ref, wo_ref,
                  lng_ref, lnb_ref, wr_ref, br_ref, x1_ref, h2_ref, route_ref):
    x = x_ref[0]
    mod = mod_ref[0]
    h = (x * (1.0 + mod[1:2]) + mod[0:1]).astype(BF16)
    m = None
    for z, o_ref in enumerate((oa_ref, ob_ref, oc_ref, od_ref)):
        gate = jax.nn.sigmoid(jnp.dot(h, wg_ref[:, z * D_MODEL:(z + 1) * D_MODEL], preferred_element_type=F32))
        up = jnp.dot(o_ref[0], wbr_ref[z], preferred_element_type=F32)
        m = gate * up if m is None else m + gate * up
    y = jnp.dot(m.astype(BF16), wo_ref[...], preferred_element_type=F32)
    x1 = _layer_norm(DN_ALPHA * x + mod[2:3] * y, lng_ref[...], lnb_ref[...])
    x1_ref[0] = x1
    h2 = x1 * (1.0 + mod[4:5]) + mod[3:4]
    for j in range(ROW_VREGS):
        h2_ref[pl.ds(j, TOKEN_TILE, stride=ROW_VREGS), :] = h2[:, j * LANES:(j + 1) * LANES]
    route_ref[0] = _route(_dot3(h2, wr_ref[...]) + br_ref[...])


def _merge_call(xa, mods, oa, ob, oc, od, wg, wbr, wo, ln_g, ln_b, wr, br):
    nb, nt, d = xa.shape
    tiles = nt // TOKEN_TILE
    bw = oa.shape[-1]
    tok = lambda b, t: (b, t, 0)
    const2 = lambda b, t: (0, 0)
    return pl.pallas_call(
        _merge_kernel,
        grid=(nb, tiles),
        in_specs=[
            pl.BlockSpec((1, TOKEN_TILE, d), tok),
            pl.BlockSpec((1, SUBLANES, d), _mod_index(nb)),
            pl.BlockSpec((1, TOKEN_TILE, bw), tok), pl.BlockSpec((1, TOKEN_TILE, bw), tok),
            pl.BlockSpec((1, TOKEN_TILE, bw), tok), pl.BlockSpec((1, TOKEN_TILE, bw), tok),
            pl.BlockSpec((d, N_BRANCH * d), const2),
            pl.BlockSpec((N_BRANCH, bw, d), lambda b, t: (0, 0, 0)),
            pl.BlockSpec((d, d), const2),
            pl.BlockSpec((1, d), const2), pl.BlockSpec((1, d), const2),
            pl.BlockSpec((d, LANES), const2), pl.BlockSpec((1, LANES), const2),
        ],
        out_specs=[pl.BlockSpec((1, TOKEN_TILE, d), tok),
                   pl.BlockSpec((TOKEN_TILE * ROW_VREGS, LANES), lambda b, t: (b * tiles + t, 0)),
                   pl.BlockSpec((1, TOKEN_TILE, LANES), tok)],
        out_shape=[jax.ShapeDtypeStruct((nb, nt, d), F32),
                   jax.ShapeDtypeStruct((nb * nt * ROW_VREGS, LANES), F32),
                   jax.ShapeDtypeStruct((nb, nt, LANES), F32)],
        compiler_params=_params(("arbitrary", "arbitrary")),
        name="merge_out_ln1_router",
    )(xa, mods, oa, ob, oc, od, wg, wbr, wo, ln_g, ln_b, wr, br)


MOE_UNROLL = 4


def _moe_chunk(total):
    return max(c for c in range(TOKEN_TILE, MOE_CHUNK + 1, TOKEN_TILE) if total % c == 0)


def _moe_kernel(cnt_ref, off_ref, idx_ref, wt_ref, x_ref, wg_ref, wu_ref, wd_ref, y_ref, xt_s, ot_s):
    c = pl.program_id(0)
    e = pl.program_id(1)
    chunk = x_ref.shape[1] // ROW_VREGS

    @pl.when(e == 0)
    def _zero():
        y_ref[...] = jnp.zeros(y_ref.shape, F32)

    n_rows = cnt_ref[c * N_EXPERTS + e]
    off = off_ref[c * N_EXPERTS + e]

    def slab(tok):
        return pl.ds(pl.multiple_of(tok * ROW_VREGS, ROW_VREGS), ROW_VREGS)

    def tile_body(i, carry):
        base = off + i * MOE_ROWS
        valid = n_rows - i * MOE_ROWS
        for mi in range(MOE_ROWS):
            tok = jnp.minimum(idx_ref[0, 0, base + mi], chunk - 1)
            xt_s[pl.ds(mi, ROW_VREGS, stride=MOE_STRIDE), :] = x_ref[0, slab(tok), :]
        xt = jnp.concatenate([xt_s[j * MOE_STRIDE:j * MOE_STRIDE + MOE_ROWS, :] for j in range(ROW_VREGS)],
                             axis=-1).astype(BF16)
        a = _silu(jnp.dot(xt, wg_ref[0], preferred_element_type=F32)) * jnp.dot(xt, wu_ref[0], preferred_element_type=F32)
        out = jnp.dot(a.astype(BF16), wd_ref[0], preferred_element_type=F32)
        for j in range(ROW_VREGS):
            ot_s[j * MOE_STRIDE:j * MOE_STRIDE + MOE_ROWS, :] = out[:, j * LANES:(j + 1) * LANES]
        for m0 in range(0, MOE_ROWS, MOE_UNROLL):
            pending = []
            for mi in range(m0, m0 + MOE_UNROLL):
                ok = mi < valid
                rows = slab(jnp.where(ok, idx_ref[0, 0, base + mi], chunk))
                wgt = jnp.where(ok, wt_ref[0, 0, base + mi], 0.0)
                upd = y_ref[0, rows, :] + wgt * ot_s[pl.ds(mi, ROW_VREGS, stride=MOE_STRIDE), :]
                pending.append((rows, upd))
            for rows, upd in pending:
                y_ref[0, rows, :] = upd
        return carry

    lax.fori_loop(0, (n_rows + MOE_ROWS - 1) // MOE_ROWS, tile_body, 0)


def _moe_call(h2_slab, cnt, off, idx_sorted, w_sorted, wg, wu, wd):
    nch, rows_in, _ = h2_slab.shape
    plan = idx_sorted.shape[2]
    rows_out = rows_in + SUBLANES * ROW_VREGS
    grid_spec = pltpu.PrefetchScalarGridSpec(
        num_scalar_prefetch=2,
        grid=(nch, N_EXPERTS),
        in_specs=[
            pl.BlockSpec((1, 1, plan), lambda c, e, *_: (c, 0, 0), memory_space=pltpu.SMEM),
            pl.BlockSpec((1, 1, plan), lambda c, e, *_: (c, 0, 0), memory_space=pltpu.SMEM),
            pl.BlockSpec((1, rows_in, LANES), lambda c, e, *_: (c, 0, 0)),
            pl.BlockSpec((1, D_MODEL, EXP_HIDDEN), lambda c, e, *_: (e, 0, 0)),
            pl.BlockSpec((1, D_MODEL, EXP_HIDDEN), lambda c, e, *_: (e, 0, 0)),
            pl.BlockSpec((1, EXP_HIDDEN, D_MODEL), lambda c, e, *_: (e, 0, 0)),
        ],
        out_specs=pl.BlockSpec((1, rows_out, LANES), lambda c, e, *_: (c, 0, 0)),
        scratch_shapes=[pltpu.VMEM((ROW_VREGS * MOE_STRIDE, LANES), F32),
                        pltpu.VMEM((ROW_VREGS * MOE_STRIDE, LANES), F32)],
    )
    return pl.pallas_call(
        _moe_kernel,
        grid_spec=grid_spec,
        out_shape=jax.ShapeDtypeStruct((nch, rows_out, LANES), F32),
        compiler_params=_params(("arbitrary", "arbitrary")),
        name="moe_experts",
    )(cnt, off, idx_sorted, w_sorted, h2_slab, wg, wu, wd)


def _moe_plan(route, chunk):
    t = route.shape[0]
    nch = t // chunk
    per = chunk * 2
    eid = route[:, ROUTE_E1:ROUTE_E2 + 1].astype(jnp.int32).reshape(nch, per)
    wts = route[:, ROUTE_W1:ROUTE_W2 + 1].reshape(nch, per)
    order = jnp.argsort(eid, axis=1, stable=True).astype(jnp.int32)
    idx_sorted = jnp.pad(order // 2, ((0, 0), (0, MOE_ROWS)), constant_values=chunk)
    w_sorted = jnp.pad(jnp.take_along_axis(wts, order, axis=1), ((0, 0), (0, MOE_ROWS)))
    cnt = jnp.sum((eid[..., None] == jnp.arange(N_EXPERTS, dtype=jnp.int32)).astype(jnp.int32), axis=1)
    off = jnp.cumsum(cnt, axis=1) - cnt
    return (cnt.reshape(-1).astype(jnp.int32), off.reshape(-1).astype(jnp.int32),
            idx_sorted.reshape(nch, 1, per + MOE_ROWS), w_sorted.reshape(nch, 1, per + MOE_ROWS))


def _ln2_kernel(x_ref, y_ref, mod_ref, g_ref, b_ref, o_ref):
    mod = mod_ref[0]
    y = jnp.concatenate([y_ref[0, pl.ds(j, TOKEN_TILE, stride=ROW_VREGS), :] for j in range(ROW_VREGS)], axis=-1)
    o_ref[0] = _layer_norm(DN_ALPHA * x_ref[0] + mod[5:6] * y, g_ref[...], b_ref[...])


def _ln2_call(x1, y_slab, mods, ln_g, ln_b, chunk):
    nb, nt, d = x1.shape
    tiles = nt // TOKEN_TILE
    per_chunk = chunk // TOKEN_TILE
    tok = lambda b, t: (b, t, 0)
    slab = lambda b, t: ((b * tiles + t) // per_chunk, (b * tiles + t) % per_chunk, 0)
    return pl.pallas_call(
        _ln2_kernel,
        grid=(nb, tiles),
        in_specs=[pl.BlockSpec((1, TOKEN_TILE, d), tok),
                  pl.BlockSpec((1, TOKEN_TILE * ROW_VREGS, LANES), slab),
                  pl.BlockSpec((1, SUBLANES, d), _mod_index(nb)),
                  pl.BlockSpec((1, d), lambda b, t: (0, 0)), pl.BlockSpec((1, d), lambda b, t: (0, 0))],
        out_specs=pl.BlockSpec((1, TOKEN_TILE, d), tok),
        out_shape=jax.ShapeDtypeStruct((nb, nt, d), F32),
        compiler_params=_params(("arbitrary", "arbitrary")),
        name="moe_residual_ln2",
    )(x1, y_slab, mods, ln_g, ln_b)


def _head_constants():
    w = A_HEADS * HEAD_DIM
    i = np.arange(w)
    ones_bd = (i[:, None] // HEAD_DIM == i[None, :] // HEAD_DIM).astype(np.float32)
    quarter = HEAD_DIM // 4
    rot = np.zeros((w, w), np.float32)
    first = (i % (2 * quarter)) < quarter
    rot[i[first] + quarter, i[first]] = -1.0
    rot[i[~first] - quarter, i[~first]] = 1.0
    return jnp.asarray(ones_bd, BF16), jnp.asarray(rot, BF16)


def _rope_tables(seq):
    t = jnp.arange(seq, dtype=jnp.int32)
    row = (t // GRID_W).astype(F32)
    col = (t % GRID_W).astype(F32)
    nf = HEAD_DIM // 4
    inv = ROPE_THETA ** (-jnp.arange(nf, dtype=F32) / nf)
    ang_r = row[:, None] * inv
    ang_c = col[:, None] * inv
    cos = jnp.concatenate([jnp.cos(ang_r), jnp.cos(ang_r), jnp.cos(ang_c), jnp.cos(ang_c)], axis=-1)
    sin = jnp.concatenate([jnp.sin(ang_r), jnp.sin(ang_r), jnp.sin(ang_c), jnp.sin(ang_c)], axis=-1)
    return jnp.tile(cos, (1, A_HEADS)), jnp.tile(sin, (1, A_HEADS))


def _in_weight(w_in_l):
    cols = []
    for _, parts, width, _ in IN_GROUPS:
        got = 0
        for p in parts:
            o, n = _IN_OFFS[p]
            cols.append(w_in_l[:, o:o + n])
            got += n
        if got < width:
            cols.append(jnp.zeros((w_in_l.shape[0], width - got), w_in_l.dtype))
    return jnp.concatenate(cols, axis=1).astype(BF16)


def _lane_row(vec, width, offset=0):
    return jnp.zeros((1, width), F32).at[0, offset:offset + vec.shape[0]].set(vec.astype(F32))


def kernel(x, c, ctx, c_ctx, w_ada, b_ada, w_in, a_q_gain, a_k_gain, b_rpb, c_conv, c_a_log, c_dt_bias, c_out_gain, d_gate_w, d_gate_b, d_out_gain, w_branch, w_out, ln1_g, ln1_b, ln2_g, ln2_b, w_router_g, b_router_g, w_router_e, b_router_e, w_up, w_gate, w_down):
    nb, seq, d = x.shape
    depth = w_ada.shape[0]
    nt = CTX_LEN + seq
    assert d == D_MODEL and ctx.shape[1] == CTX_LEN and nb + 1 <= ADA_ROWS
    assert seq % TOKEN_TILE == 0
    chunk = _moe_chunk(nb * nt)

    xa = jnp.concatenate([ctx, x], axis=1)
    cc = jnp.zeros((ADA_ROWS, d), F32).at[:nb].set(c).at[nb].set(c_ctx)
    mods = _ada_call(cc, w_ada, b_ada).reshape(depth, ADA_ROWS, 6, d)[:, :nb + 1]
    mods = jnp.pad(mods, ((0, 0), (0, 0), (0, SUBLANES - 6), (0, 0)))

    ones_bd, rot_m = _head_constants()
    cos_t, sin_t = _rope_tables(seq)
    gates_off = _IN_OFFS['gates'][0]

    for l in range(depth):
        proj = dict(zip([g[0] for g in IN_GROUPS], _in_call(xa, mods[l], _in_weight(w_in[l]))))

        oa = _attn_a_call(proj['a_q'], proj['a_kv'], cos_t, sin_t,
                          jnp.tile(a_q_gain[l], A_HEADS)[None, :], jnp.tile(a_k_gain[l], A_KV_HEADS)[None, :],
                          ones_bd, rot_m)
        ob = _attn_b_call(proj['b_q'], proj['b_k'], proj['b_v'], _nb_bias_table(b_rpb[l]))
        conv_w = jnp.pad(c_conv[l], ((0, SUBLANES - c_conv.shape[1]), (0, 0)))
        par = jnp.concatenate([_lane_row(c_a_log[l].reshape(-1), LANES, C_A_LANE),
                               _lane_row(c_dt_bias[l].reshape(-1), LANES, C_A_LANE),
                               jnp.zeros((SUBLANES - 2, LANES), F32)], axis=0)
        oc = _gdn_call(proj['c_qkv'], proj['c_ba'], proj['c_g'], conv_w, par,
                       jnp.tile(c_out_gain[l], C_HEADS)[None, :], ones_bd)
        kw = D_HEADS * D_KDIM
        gw_blk = jnp.zeros((LANES, 2 * kw), F32)
        gw_blk = gw_blk.at[0:D_GATE_RANK, 0:kw].set(d_gate_w[l, 0])
        gw_blk = gw_blk.at[D_GATE_RANK:2 * D_GATE_RANK, kw:2 * kw].set(d_gate_w[l, 1])
        od = _gla_call(proj['d_qk'], proj['d_v'], proj['d_lr'], proj['d_g'], gw_blk,
                       d_gate_b[l].reshape(1, 2 * kw), jnp.tile(d_out_gain[l], D_HEADS)[None, :], ones_bd)

        wr = jnp.concatenate([w_router_g[l], jnp.transpose(w_router_e[l], (1, 0, 2)).reshape(d, N_EXPERTS)], axis=1)
        wr = jnp.pad(wr, ((0, 0), (0, LANES - wr.shape[1])))
        br = _lane_row(jnp.concatenate([b_router_g[l], b_router_e[l].reshape(-1)]), LANES)
        x1, h2, route = _merge_call(
            xa, mods[l], oa, ob, oc, od, w_in[l][:, gates_off:].astype(BF16), w_branch[l].astype(BF16),
            w_out[l].astype(BF16), ln1_g[l][None, :], ln1_b[l][None, :], wr, br)

        cnt, off, idx_sorted, w_sorted = _moe_plan(route.reshape(nb * nt, LANES), chunk)
        h2_slab = h2.reshape((nb * nt) // chunk, chunk * ROW_VREGS, LANES)
        y_slab = _moe_call(h2_slab, cnt, off, idx_sorted, w_sorted,
                           w_gate[l].astype(BF16), w_up[l].astype(BF16), w_down[l].astype(BF16))
        xa = _ln2_call(x1, y_slab, mods[l], ln2_g[l][None, :], ln2_b[l][None, :], chunk)

    return xa[:, CTX_LEN:, :]
```

```python
import functools
import math

import numpy as np
import jax
import jax.numpy as jnp
from jax import lax
from jax.experimental import pallas as pl
from jax.experimental.pallas import tpu as pltpu

F32 = jnp.float32
BF16 = jnp.bfloat16

D_MODEL = 1024
DEPTH = 4
GRID_W = 64
CTX_LEN = 256
HEAD_DIM = 64
A_HEADS = 4
A_KV_HEADS = 2
ROPE_THETA = 10000.0
B_HEADS = 4
WIN_R = 8
WIN_C = 16
C_HEADS = 4
D_HEADS = 4
D_KDIM = 32
D_GATE_RANK = 16
GLA_TAU = 16.0
CHUNK = 64
N_BRANCH = 4
N_GROUPS = 4
EXP_PER_GROUP = 8
N_EXPERTS = N_GROUPS * EXP_PER_GROUP
EXP_HIDDEN = 512
EPS = 1e-6
DN_ALPHA = (2.0 * DEPTH) ** 0.25
NEG_BIG = -1e30

LANES = 128
SUBLANES = 8
TOKEN_TILE = 256
VMEM_LIMIT = 56 * 1024 * 1024

_IN_OFFS = {}
_off = 0
for _n, _w in (('a_q', 256), ('a_k', 128), ('a_v', 128), ('b_q', 256), ('b_k', 256), ('b_v', 256),
               ('c_qkv', 768), ('c_beta', 8), ('c_a', 8), ('c_g', 256), ('d_q', 128), ('d_k', 128),
               ('d_v', 256), ('d_lr', 32), ('d_g', 256), ('gates', 4096)):
    _IN_OFFS[_n] = (_off, _w)
    _off += _w
IN_GROUPS = (
    ('a_q', ('a_q',), 256, BF16),
    ('a_kv', ('a_k', 'a_v'), 256, BF16),
    ('b_q', ('b_q',), 256, BF16),
    ('b_k', ('b_k',), 256, BF16),
    ('b_v', ('b_v',), 256, BF16),
    ('c_qkv', ('c_qkv',), 768, BF16),
    ('c_ba', ('c_beta', 'c_a'), 128, F32),
    ('c_g', ('c_g',), 256, F32),
    ('d_qk', ('d_q', 'd_k'), 256, F32),
    ('d_v', ('d_v',), 256, F32),
    ('d_lr', ('d_lr',), 128, F32),
    ('d_g', ('d_g',), 256, F32),
)
IN_TOTAL = sum(g[2] for g in IN_GROUPS)

MOE_CHUNK = 2048
MOE_ROWS = 160
MOE_STRIDE = MOE_ROWS + SUBLANES
ROW_VREGS = D_MODEL // LANES


def _dot(a, b):
    return jnp.dot(a.astype(BF16), b.astype(BF16), preferred_element_type=F32)


def _dot_nt(a, b):
    return lax.dot_general(a.astype(BF16), b.astype(BF16), (((1,), (1,)), ((), ())),
                           preferred_element_type=F32)


def _dot_tn(a, b):
    return lax.dot_general(a.astype(BF16), b.astype(BF16), (((0,), (0,)), ((), ())),
                           preferred_element_type=F32)


def _split(x):
    hi = x.astype(BF16)
    lo = (x - hi.astype(F32)).astype(BF16)
    return hi, lo


def _dot3(a, b):
    ah, al = _split(a)
    bh, bl = _split(b)
    return (jnp.dot(ah, bh, preferred_element_type=F32) + jnp.dot(al, bh, preferred_element_type=F32)
            + jnp.dot(ah, bl, preferred_element_type=F32))


def _dot_sel(a, m):
    ah, al = _split(a)
    return jnp.dot(ah, m, preferred_element_type=F32) + jnp.dot(al, m, preferred_element_type=F32)


def _silu(x):
    return x * jax.nn.sigmoid(x)


def _layer_norm(r, g, b):
    mu = jnp.mean(r, axis=-1, keepdims=True)
    d = r - mu
    var = jnp.mean(d * d, axis=-1, keepdims=True)
    return d * lax.rsqrt(var + EPS) * g + b


def _params(sem):
    return pltpu.CompilerParams(dimension_semantics=sem, vmem_limit_bytes=VMEM_LIMIT)


ADA_ROWS = 24
ADA_TILE = 1536


def _ada_kernel(cc_ref, w_ref, b_ref, o_ref):
    s = _silu(cc_ref[...])
    o_ref[0] = _dot3(s, w_ref[0]) + b_ref[0]


def _ada_call(cc, w_ada, b_ada):
    depth = w_ada.shape[0]
    n = w_ada.shape[2]
    return pl.pallas_call(
        _ada_kernel,
        grid=(depth, n // ADA_TILE),
        in_specs=[
            pl.BlockSpec((ADA_ROWS, D_MODEL), lambda l, j: (0, 0)),
            pl.BlockSpec((1, D_MODEL, ADA_TILE), lambda l, j: (l, 0, j)),
            pl.BlockSpec((1, 1, ADA_TILE), lambda l, j: (l, 0, j)),
        ],
        out_specs=pl.BlockSpec((1, ADA_ROWS, ADA_TILE), lambda l, j: (l, 0, j)),
        out_shape=jax.ShapeDtypeStruct((depth, ADA_ROWS, n), F32),
        compiler_params=_params(("arbitrary", "arbitrary")),
        name="ada_mod",
    )(cc, w_ada, b_ada.reshape(depth, 1, n))


def _mod_index(nb):
    return lambda b, t: (jnp.where(t == 0, nb, b), 0, 0)


def _in_kernel(x_ref, mod_ref, w_ref, *out_refs):
    x = x_ref[0]
    mod = mod_ref[0]
    h = (x * (1.0 + mod[1:2]) + mod[0:1]).astype(BF16)
    off = 0
    for (name, _, width, dt), o_ref in zip(IN_GROUPS, out_refs):
        o_ref[0] = jnp.dot(h, w_ref[:, off:off + width], preferred_element_type=F32).astype(dt)
        off += width


def _in_call(xa, mods, w_cat):
    nb, nt, _ = xa.shape
    tiles = nt // TOKEN_TILE
    return pl.pallas_call(
        _in_kernel,
        grid=(nb, tiles),
        in_specs=[
            pl.BlockSpec((1, TOKEN_TILE, D_MODEL), lambda b, t: (b, t, 0)),
            pl.BlockSpec((1, SUBLANES, D_MODEL), _mod_index(nb)),
            pl.BlockSpec((D_MODEL, IN_TOTAL), lambda b, t: (0, 0)),
        ],
        out_specs=[pl.BlockSpec((1, TOKEN_TILE, g[2]), lambda b, t: (b, t, 0)) for g in IN_GROUPS],
        out_shape=[jax.ShapeDtypeStruct((nb, nt, g[2]), g[3]) for g in IN_GROUPS],
        compiler_params=_params(("arbitrary", "arbitrary")),
        name="in_proj",
    )(xa, mods, w_cat)


def _head_rms(x, ones_bd, gain):
    ss = _dot_sel(x * x, ones_bd)
    return x * lax.rsqrt(ss * (1.0 / HEAD_DIM) + EPS) * gain


def _rope(x, rot, cos, sin):
    return x * cos + _dot_sel(x, rot) * sin


def _attn_a_kernel(q_ref, kv_ref, cos_ref, sin_ref, qg_ref, kg_ref, bd_ref, rot_ref, o_ref, kp_ref):
    t = pl.program_id(1)
    kvw = A_KV_HEADS * HEAD_DIM
    n_lat_tiles = (kv_ref.shape[1] - CTX_LEN) // TOKEN_TILE
    scale = HEAD_DIM ** -0.5

    @pl.when(t == 0)
    def _prep_keys():
        bd = bd_ref[0:kvw, 0:kvw]
        rot = rot_ref[0:kvw, 0:kvw]
        kg = kg_ref[...]
        kc = kv_ref[0, 0:CTX_LEN, 0:kvw].astype(F32)
        kp_ref[0:CTX_LEN, :] = _head_rms(kc, bd, kg).astype(BF16)

        def body(i, carry):
            r0 = pl.multiple_of(i * TOKEN_TILE, TOKEN_TILE)
            k = kv_ref[0, pl.ds(CTX_LEN + r0, TOKEN_TILE), 0:kvw].astype(F32)
            kn = _head_rms(k, bd, kg)
            kr = _rope(kn, rot, cos_ref[pl.ds(r0, TOKEN_TILE), 0:kvw], sin_ref[pl.ds(r0, TOKEN_TILE), 0:kvw])
            kp_ref[pl.ds(CTX_LEN + r0, TOKEN_TILE), :] = kr.astype(BF16)
            return carry

        lax.fori_loop(0, n_lat_tiles, body, 0)

    qn = _head_rms(q_ref[0].astype(F32), bd_ref[...], qg_ref[...])

    def attend(qh, nk):
        outs = []
        for g in range(A_KV_HEADS):
            kg_ = kp_ref[0:nk, g * HEAD_DIM:(g + 1) * HEAD_DIM]
            vg_ = kv_ref[0, 0:nk, kvw + g * HEAD_DIM:kvw + (g + 1) * HEAD_DIM]
            for r in range(A_HEADS // A_KV_HEADS):
                h = g * (A_HEADS // A_KV_HEADS) + r
                qq = (qh[:, h * HEAD_DIM:(h + 1) * HEAD_DIM] * scale).astype(BF16)
                s = _dot_nt(qq, kg_)
                m = jnp.max(s, axis=-1, keepdims=True)
                e = jnp.exp(s - m)
                l = jnp.sum(e, axis=-1, keepdims=True)
                outs.append(jnp.dot(e.astype(BF16), vg_, preferred_element_type=F32) / l)
        return jnp.concatenate(outs, axis=-1)

    @pl.when(t == 0)
    def _ctx_queries():
        o_ref[0] = attend(qn, CTX_LEN).astype(o_ref.dtype)

    @pl.when(t > 0)
    def _latent_queries():
        r0 = pl.multiple_of((t - 1) * TOKEN_TILE, TOKEN_TILE)
        qr = _rope(qn, rot_ref[...], cos_ref[pl.ds(r0, TOKEN_TILE), :], sin_ref[pl.ds(r0, TOKEN_TILE), :])
        o_ref[0] = attend(qr, kv_ref.shape[1]).astype(o_ref.dtype)


def _attn_a_call(a_q, a_kv, cos_t, sin_t, q_gain, k_gain, ones_bd, rot_m):
    nb, nt, _ = a_q.shape
    tiles = nt // TOKEN_TILE
    seq = nt - CTX_LEN
    qw = A_HEADS * HEAD_DIM
    return pl.pallas_call(
        _attn_a_kernel,
        grid=(nb, tiles),
        in_specs=[
            pl.BlockSpec((1, TOKEN_TILE, qw), lambda b, t: (b, t, 0)),
            pl.BlockSpec((1, nt, qw), lambda b, t: (b, 0, 0)),
            pl.BlockSpec((seq, qw), lambda b, t: (0, 0)),
            pl.BlockSpec((seq, qw), lambda b, t: (0, 0)),
            pl.BlockSpec((1, qw), lambda b, t: (0, 0)),
            pl.BlockSpec((1, A_KV_HEADS * HEAD_DIM), lambda b, t: (0, 0)),
            pl.BlockSpec((qw, qw), lambda b, t: (0, 0)),
            pl.BlockSpec((qw, qw), lambda b, t: (0, 0)),
        ],
        out_specs=pl.BlockSpec((1, TOKEN_TILE, qw), lambda b, t: (b, t, 0)),
        out_shape=jax.ShapeDtypeStruct((nb, nt, qw), BF16),
        scratch_shapes=[pltpu.VMEM((nt, A_KV_HEADS * HEAD_DIM), BF16)],
        compiler_params=_params(("arbitrary", "arbitrary")),
        name="mixer_a_gqa",
    )(a_q, a_kv, cos_t, sin_t, q_gain, k_gain, ones_bd, rot_m)


NB_QROWS = TOKEN_TILE // GRID_W
NB_KROWS = 12
NB_INVALID = 2 * WIN_R - 1


def _attn_b_kernel(q_ref, k_ref, v_ref, bt_ref, o_ref):
    t = pl.program_id(1)
    scale = HEAD_DIM ** -0.5
    rows = (k_ref.shape[1] - CTX_LEN) // GRID_W
    wr = min(WIN_R, rows)

    def softmax_pv(parts):
        m = None
        for s, _ in parts:
            mi = jnp.max(s, axis=-1, keepdims=True)
            m = mi if m is None else jnp.maximum(m, mi)
        acc, l = None, None
        for s, v in parts:
            e = jnp.exp(s - m)
            li = jnp.sum(e, axis=-1, keepdims=True)
            oi = jnp.dot(e.astype(BF16), v, preferred_element_type=F32)
            acc = oi if acc is None else acc + oi
            l = li if l is None else l + li
        return acc / l

    @pl.when(t == 0)
    def _ctx_queries():
        outs = []
        for h in range(B_HEADS):
            sl = slice(h * HEAD_DIM, (h + 1) * HEAD_DIM)
            qq = (q_ref[0, :, sl].astype(F32) * scale).astype(BF16)
            s = _dot_nt(qq, k_ref[0, 0:CTX_LEN, sl])
            outs.append(softmax_pv([(s, v_ref[0, 0:CTX_LEN, sl])]))
        o_ref[0] = jnp.concatenate(outs, axis=-1).astype(o_ref.dtype)

    @pl.when(t > 0)
    def _latent_queries():
        r0 = (t - 1) * NB_QROWS
        start = jnp.clip(r0 - wr // 2, 0, rows - NB_KROWS)
        k0 = pl.multiple_of(CTX_LEN + start * GRID_W, GRID_W)
        nk = NB_KROWS * GRID_W
        lane = lax.broadcasted_iota(jnp.int32, (GRID_W, 2 * GRID_W), 1)
        left = lane < GRID_W
        slots = []
        for i in range(NB_QROWS):
            r = r0 + i
            rs = jnp.clip(r - wr // 2, 0, rows - wr)
            row_slots = []
            for j in range(NB_KROWS):
                kr = start + j
                ok = jnp.logical_and(kr >= rs, kr < rs + wr)
                row_slots.append(jnp.where(ok, kr - r + WIN_R - 1, NB_INVALID))
            slots.append(row_slots)
        outs = []
        for h in range(B_HEADS):
            sl = slice(h * HEAD_DIM, (h + 1) * HEAD_DIM)
            qq = (q_ref[0, :, sl].astype(F32) * scale).astype(BF16)
            s_loc = _dot_nt(qq, k_ref[0, pl.ds(k0, nk), sl])
            s_ctx = _dot_nt(qq, k_ref[0, 0:CTX_LEN, sl])
            bias_rows = []
            for i in range(NB_QROWS):
                tiles = []
                for jp in range(NB_KROWS // 2):
                    b0 = bt_ref[h, slots[i][2 * jp]]
                    b1 = bt_ref[h, slots[i][2 * jp + 1]]
                    tiles.append(jnp.where(left, b0, b1))
                bias_rows.append(jnp.concatenate(tiles, axis=-1))
            s_loc = s_loc + jnp.concatenate(bias_rows, axis=0)
            outs.append(softmax_pv([(s_loc, v_ref[0, pl.ds(k0, nk), sl]), (s_ctx, v_ref[0, 0:CTX_LEN, sl])]))
        o_ref[0] = jnp.concatenate(outs, axis=-1).astype(o_ref.dtype)


def _attn_b_call(b_q, b_k, b_v, bias_tab):
    nb, nt, w = b_q.shape
    tiles = nt // TOKEN_TILE
    return pl.pallas_call(
        _attn_b_kernel,
        grid=(nb, tiles),
        in_specs=[
            pl.BlockSpec((1, TOKEN_TILE, w), lambda b, t: (b, t, 0)),
            pl.BlockSpec((1, nt, w), lambda b, t: (b, 0, 0)),
            pl.BlockSpec((1, nt, w), lambda b, t: (b, 0, 0)),
            pl.BlockSpec(bias_tab.shape, lambda b, t: (0, 0, 0, 0)),
        ],
        out_specs=pl.BlockSpec((1, TOKEN_TILE, w), lambda b, t: (b, t, 0)),
        out_shape=jax.ShapeDtypeStruct((nb, nt, w), BF16),
        compiler_params=_params(("arbitrary", "arbitrary")),
        name="mixer_b_neighbourhood",
    )(b_q, b_k, b_v, bias_tab)


def _nb_bias_table(rpb):
    cols = jnp.arange(GRID_W, dtype=jnp.int32)
    col_start = jnp.clip(cols - WIN_C // 2, 0, GRID_W - WIN_C)
    col_ok = (cols[None, :] >= col_start[:, None]) & (cols[None, :] < col_start[:, None] + WIN_C)
    dc_idx = jnp.clip(cols[None, :] - cols[:, None] + WIN_C - 1, 0, 2 * WIN_C - 2)
    tab = rpb.astype(F32)[:, :, dc_idx]
    tab = jnp.where(col_ok[None, None], tab, NEG_BIG)
    tab = jnp.concatenate([tab, jnp.full_like(tab[:, :1], NEG_BIG)], axis=1)
    return jnp.concatenate([tab, tab], axis=-1)


def _order_masks():
    i = lax.broadcasted_iota(jnp.int32, (CHUNK, CHUNK), 0)
    j = lax.broadcasted_iota(jnp.int32, (CHUNK, CHUNK), 1)
    return ((j <= i, j < i), (j >= i, j > i))


def _scan_chunk(step, direction, n_ctx_chunks, n_chunks):
    if direction == 0:
        return step
    return jnp.where(step < n_ctx_chunks, n_ctx_chunks - 1 - step, n_chunks + n_ctx_chunks - 1 - step)


def _sel_dot(m, a):
    ah, al = _split(a)
    return jnp.dot(m, ah, preferred_element_type=F32) + jnp.dot(m, al, preferred_element_type=F32)


def _gated_out_tiles(part_refs, gate_ref, gain_ref, bd_ref, o_ref):
    nt = part_refs[0].shape[0]
    bd = bd_ref[...]
    gain = gain_ref[...]

    def body(i, carry):
        r0 = pl.multiple_of(i * TOKEN_TILE, TOKEN_TILE)
        o = sum(p[pl.ds(r0, TOKEN_TILE), :] for p in part_refs)
        y = _head_rms(o, bd, gain) * _silu(gate_ref[0, pl.ds(r0, TOKEN_TILE), :])
        o_ref[0, pl.ds(r0, TOKEN_TILE), :] = y.astype(o_ref.dtype)
        return carry

    lax.fori_loop(0, nt // TOKEN_TILE, body, 0)


C_BETA_LANE = 0
C_A_LANE = 8
C_T_ROWS = 16


SOLVE_BLOCK = 16
GDN_LOCAL_CHUNKS = 4


def _unit_lower_solve(lmats, rhss, same_block, eye):
    n = range(len(lmats))
    lds = [jnp.where(same_block, l, 0.0) for l in lmats]
    ts = [eye - ld for ld in lds]
    ps = lds
    span = 2
    while span < SOLVE_BLOCK:
        ps = [_dot(p, p) for p in ps]
        ts = [ts[i] + _dot(ts[i], ps[i]) for i in n]
        span *= 2
    width = lmats[0].shape[1]
    mzs = [_dot(ts[i], jnp.concatenate([lmats[i] - lds[i], rhss[i]], axis=-1)) for i in n]
    mmzs = [_dot(mz[:, 0:width], mz) for mz in mzs]
    zs = [mzs[i][:, width:] - mmzs[i][:, width:] for i in n]
    ps = [mmz[:, 0:width] for mmz in mmzs]
    span = 2
    while span < CHUNK // SOLVE_BLOCK:
        zs = [zs[i] + _dot(ps[i], zs[i]) for i in n]
        span *= 2
        if span < CHUNK // SOLVE_BLOCK:
            ps = [_dot(p, p) for p in ps]
    return zs


def _gdn_kernel(qkv_ref, ba_ref, gate_ref, conv_ref, par_ref, gain_ref, bd_ref, o_ref,
                q_s, k_s, v_s, bl_s, g_s, gt_s, a12_s, b2_s, egl_s, o_s, st_s):
    nt = qkv_ref.shape[1]
    n_tiles = nt // TOKEN_TILE
    n_chunks = nt // CHUNK
    n_ctx_chunks = CTX_LEN // CHUNK
    w = C_HEADS * HEAD_DIM
    pack = 2 * SUBLANES
    bd = bd_ref[...]
    lane = lax.broadcasted_iota(jnp.int32, (TOKEN_TILE, LANES), 1)
    lane_c = lax.broadcasted_iota(jnp.int32, (CHUNK, LANES), 1)
    row = lax.broadcasted_iota(jnp.int32, (TOKEN_TILE, 1), 0)
    neg_rate = -jnp.exp(par_ref[0:1, :])
    dt_bias = par_ref[1:2, :]
    w_prev, w_mid, w_next = conv_ref[0:1, :], conv_ref[1:2, :], conv_ref[2:3, :]
    masks = _order_masks()
    mask_bf = [jnp.where(m[0], 1.0, 0.0).astype(BF16) for m in masks]

    def prep(i, carry):
        r0 = pl.multiple_of(i * TOKEN_TILE, TOKEN_TILE)
        x = qkv_ref[0, pl.ds(r0, TOKEN_TILE), :].astype(F32)
        before = qkv_ref[0, pl.ds(pl.multiple_of(jnp.maximum(r0 - pack, 0), pack), pack), :].astype(F32)
        after = qkv_ref[0, pl.ds(pl.multiple_of(jnp.minimum(r0 + TOKEN_TILE, nt - pack), pack), pack), :].astype(F32)
        first_of_seq = jnp.logical_or(i == 0, i == CTX_LEN // TOKEN_TILE)
        last_of_seq = jnp.logical_or(i == CTX_LEN // TOKEN_TILE - 1, i == n_tiles - 1)
        edge_prev = jnp.where(first_of_seq, 0.0, before[pack - 1:pack, :])
        edge_next = jnp.where(last_of_seq, 0.0, after[0:1, :])
        x_prev = jnp.where(row == 0, edge_prev, pltpu.roll(x, 1, axis=0))
        x_next = jnp.where(row == TOKEN_TILE - 1, edge_next, pltpu.roll(x, TOKEN_TILE - 1, axis=0))
        y = _silu(x_prev * w_prev + x * w_mid + x_next * w_next)
        q, k, v = y[:, 0:w], y[:, w:2 * w], y[:, 2 * w:3 * w]
        q_s[pl.ds(r0, TOKEN_TILE), :] = q * lax.rsqrt(_dot_sel(q * q, bd) + EPS) * (HEAD_DIM ** -0.5)
        k_s[pl.ds(r0, TOKEN_TILE), :] = k * lax.rsqrt(_dot_sel(k * k, bd) + EPS)
        v_s[pl.ds(r0, TOKEN_TILE), :] = v
        ba = ba_ref[0, pl.ds(r0, TOKEN_TILE), :]
        sp = ba + dt_bias
        softplus = jnp.maximum(sp, 0.0) + jnp.log1p(jnp.exp(-jnp.abs(sp)))
        bl = jnp.where(lane < C_A_LANE, jax.nn.sigmoid(ba), neg_rate * softplus)
        bl_s[pl.ds(r0, TOKEN_TILE), :] = bl
        for c in range(TOKEN_TILE // CHUNK):
            blc = bl[c * CHUNK:(c + 1) * CHUNK, :]
            g = jnp.where(lane_c < C_A_LANE + C_HEADS, _sel_dot(mask_bf[0], blc), _sel_dot(mask_bf[1], blc))
            g_s[pl.ds(r0 + c * CHUNK, CHUNK), :] = g
            gt_s[i * (TOKEN_TILE // CHUNK) + c] = g.T[0:C_T_ROWS, :]
        return carry

    lax.fori_loop(0, n_tiles, prep, 0)

    ri = lax.broadcasted_iota(jnp.int32, (CHUNK, CHUNK), 0)
    ci = lax.broadcasted_iota(jnp.int32, (CHUNK, CHUNK), 1)
    same_block = (ri // SOLVE_BLOCK) == (ci // SOLVE_BLOCK)
    eye = jnp.where(ri == ci, 1.0, 0.0)

    src = lax.broadcasted_iota(jnp.int32, (LANES, w), 0)
    dst_head = lax.broadcasted_iota(jnp.int32, (LANES, w), 1) // HEAD_DIM
    spread_g = [jnp.where(src == C_A_LANE + d * C_HEADS + dst_head, 1.0, 0.0).astype(BF16) for d in range(2)]
    spread_b = [jnp.where(src == C_BETA_LANE + d * C_HEADS + dst_head, 1.0, 0.0).astype(BF16) for d in range(2)]
    heads = [slice(h * HEAD_DIM, (h + 1) * HEAD_DIM) for h in range(C_HEADS)]

    def local(it, carry):
        lmats, rhss, keep = [], [], []
        for t in range(GDN_LOCAL_CHUNKS):
            c = it * GDN_LOCAL_CHUNKS + t
            r0 = pl.multiple_of(c * CHUNK, CHUNK)
            qc = q_s[pl.ds(r0, CHUNK), :]
            kc = k_s[pl.ds(r0, CHUNK), :]
            vc = v_s[pl.ds(r0, CHUNK), :]
            bl = bl_s[pl.ds(r0, CHUNK), :]
            g_cols = g_s[pl.ds(r0, CHUNK), :]
            g_rows = gt_s[c]
            qks = [_dot_nt(qc[:, sl], kc[:, sl]) for sl in heads]
            for d in range(2):
                incl, strict = masks[d]
                last = CHUNK - 1 if d == 0 else 0
                g_all = _dot_sel(g_cols, spread_g[d])
                b_all = _dot_sel(bl, spread_b[d])
                e_g = jnp.exp(g_all)
                g_last = g_all[last:last + 1, :]
                kb = kc * b_all
                rhs_k = kb * e_g
                rhs_v = vc * b_all
                q_g = qc * e_g
                kd_t = (kc * jnp.exp(g_last - g_all)).T.astype(BF16)
                egl_s[d, c] = jnp.broadcast_to(jnp.exp(g_last), (SUBLANES, w))
                for h, sl in enumerate(heads):
                    la = C_A_LANE + d * C_HEADS + h
                    decay = jnp.exp(jnp.where(incl, g_all[:, sl] - g_rows[la:la + 1, :], NEG_BIG))
                    lmats.append(jnp.where(strict, _dot_nt(kb[:, sl], kc[:, sl]) * decay, 0.0))
                    rhss.append(jnp.concatenate([rhs_k[:, sl], rhs_v[:, sl]], axis=-1))
                    keep.append(((qks[h] * decay).astype(BF16), q_g[:, sl], kd_t[sl, :]))
        sols = [s.astype(BF16) for s in _unit_lower_solve(lmats, rhss, same_block, eye)]
        qwu = [jnp.dot(keep[i][0], sols[i], preferred_element_type=F32) for i in range(len(sols))]
        kwu = [jnp.dot(keep[i][2], sols[i], preferred_element_type=F32) for i in range(len(sols))]
        i = 0
        for t in range(GDN_LOCAL_CHUNKS):
            c = it * GDN_LOCAL_CHUNKS + t
            r0 = pl.multiple_of(c * CHUNK, CHUNK)
            o_const = None
            for d in range(2):
                u = range(i, i + C_HEADS)
                a12_s[d, c, 0:CHUNK, :] = jnp.concatenate(
                    [keep[j][1] - qwu[j][:, 0:HEAD_DIM] for j in u], axis=-1).astype(BF16)
                a12_s[d, c, CHUNK:2 * CHUNK, :] = jnp.concatenate(
                    [-kwu[j][:, 0:HEAD_DIM] for j in u], axis=-1).astype(BF16)
                b2_s[d, c] = jnp.concatenate([kwu[j][:, HEAD_DIM:] for j in u], axis=-1)
                part = jnp.concatenate([qwu[j][:, HEAD_DIM:] for j in u], axis=-1)
                o_const = part if o_const is None else o_const + part
                i += C_HEADS
            o_s[pl.ds(r0, CHUNK), :] = o_const
        return carry

    lax.fori_loop(0, n_chunks // GDN_LOCAL_CHUNKS, local, 0)

    st_s[...] = jnp.zeros(st_s.shape, F32)

    def scan(step, carry):
        units = [(d, h) for d in range(2) for h in range(C_HEADS)]
        cs = [_scan_chunk(step, d, n_ctx_chunks, n_chunks) for d in range(2)]
        a12 = [a12_s[d, cs[d]] for d in range(2)]
        b2 = [b2_s[d, cs[d]] for d in range(2)]
        egl = [egl_s[d, cs[d]] for d in range(2)]
        ss = [st_s[d, h] for d, h in units]
        rs = [jnp.dot(a12[d][:, heads[h]], ss[i].astype(BF16), preferred_element_type=F32)
              for i, (d, h) in enumerate(units)]
        for i, (d, h) in enumerate(units):
            st_s[d, h] = ss[i] * egl[d][0:1, heads[h]] + rs[i][CHUNK:, :] + b2[d][:, heads[h]]
        for d in range(2):
            r0 = pl.multiple_of(cs[d] * CHUNK, CHUNK)
            o = jnp.concatenate([rs[d * C_HEADS + h][0:CHUNK, :] for h in range(C_HEADS)], axis=-1)
            o_s[pl.ds(r0, CHUNK), :] = o_s[pl.ds(r0, CHUNK), :] + o
        return carry

    lax.fori_loop(0, n_chunks, scan, 0)
    _gated_out_tiles((o_s,), gate_ref, gain_ref, bd_ref, o_ref)


def _gdn_call(c_qkv, c_ba, c_g, conv_w, par, gain, ones_bd):
    nb, nt, _ = c_qkv.shape
    w = C_HEADS * HEAD_DIM
    nc = nt // CHUNK
    return pl.pallas_call(
        _gdn_kernel,
        grid=(nb,),
        in_specs=[
            pl.BlockSpec((1, nt, 3 * w), lambda b: (b, 0, 0)),
            pl.BlockSpec((1, nt, LANES), lambda b: (b, 0, 0)),
            pl.BlockSpec((1, nt, w), lambda b: (b, 0, 0)),
            pl.BlockSpec((SUBLANES, 3 * w), lambda b: (0, 0)),
            pl.BlockSpec((SUBLANES, LANES), lambda b: (0, 0)),
            pl.BlockSpec((1, w), lambda b: (0, 0)),
            pl.BlockSpec((w, w), lambda b: (0, 0)),
        ],
        out_specs=pl.BlockSpec((1, nt, w), lambda b: (b, 0, 0)),
        out_shape=jax.ShapeDtypeStruct((nb, nt, w), BF16),
        scratch_shapes=[
            pltpu.VMEM((nt, w), F32), pltpu.VMEM((nt, w), F32), pltpu.VMEM((nt, w), F32),
            pltpu.VMEM((nt, LANES), F32), pltpu.VMEM((nt, LANES), F32),
            pltpu.VMEM((nc, C_T_ROWS, CHUNK), F32),
            pltpu.VMEM((2, nc, 2 * CHUNK, w), BF16),
            pltpu.VMEM((2, nc, CHUNK, w), F32),
            pltpu.VMEM((2, nc, SUBLANES, w), F32),
            pltpu.VMEM((nt, w), F32),
            pltpu.VMEM((2, C_HEADS, HEAD_DIM, HEAD_DIM), F32),
        ],
        compiler_params=_params(("arbitrary",)),
        name="mixer_c_gated_delta",
    )(c_qkv, c_ba, c_g, conv_w, par, gain, ones_bd)


GLA_EXP_CAP = 80.0
GLA_LOCAL_CHUNKS = 2
GLA_SCAN_STEPS = 4


def _gla_kernel(qk_ref, v_ref, lr_ref, gate_ref, gw_ref, gb_ref, gain_ref, bd_ref, o_ref,
                la_s, qg_s, el_s, ds_s, o_s):
    nt = qk_ref.shape[1]
    n_tiles = nt // TOKEN_TILE
    n_chunks = nt // CHUNK
    n_ctx_chunks = CTX_LEN // CHUNK
    kw = D_HEADS * D_KDIM
    gw = gw_ref[...]
    gb = gb_ref[...]

    def prep(i, carry):
        r0 = pl.multiple_of(i * TOKEN_TILE, TOKEN_TILE)
        z = _dot3(lr_ref[0, pl.ds(r0, TOKEN_TILE), :], gw) + gb
        log_sig = jnp.minimum(z, 0.0) - jnp.log1p(jnp.exp(-jnp.abs(z)))
        la_s[pl.ds(r0, TOKEN_TILE), :] = log_sig * (1.0 / GLA_TAU)
        return carry

    lax.fori_loop(0, n_tiles, prep, 0)

    masks = _order_masks()
    mask_bf = [jnp.where(m[0], 1.0, 0.0).astype(BF16) for m in masks]
    vw = D_HEADS * HEAD_DIM
    k_head = lax.broadcasted_iota(jnp.int32, (CHUNK, kw), 1) // D_KDIM
    v_head = lax.broadcasted_iota(jnp.int32, (CHUNK, vw), 1) // HEAD_DIM
    state_diag = (lax.broadcasted_iota(jnp.int32, (vw, kw), 0) // HEAD_DIM
                  == lax.broadcasted_iota(jnp.int32, (vw, kw), 1) // D_KDIM)

    def local(it, carry):
        units, pre = [], {}
        for t in range(GLA_LOCAL_CHUNKS):
            c = it * GLA_LOCAL_CHUNKS + t
            r0 = pl.multiple_of(c * CHUNK, CHUNK)
            qk = qk_ref[0, pl.ds(r0, CHUNK), :]
            q = qk[:, 0:kw] * (D_KDIM ** -0.5)
            k = qk[:, kw:2 * kw]
            v = v_ref[0, pl.ds(r0, CHUNK), :]
            v_b = v.astype(BF16)
            v_t = v.T.astype(BF16)
            for d in range(2):
                last = CHUNK - 1 if d == 0 else 0
                g = _sel_dot(mask_bf[d], la_s[pl.ds(r0, CHUNK), d * kw:(d + 1) * kw])
                g_mid = g[CHUNK // 2:CHUNK // 2 + 1, :]
                g_last = g[last:last + 1, :]
                q_t = q * jnp.exp(jnp.minimum(g - g_mid, GLA_EXP_CAP))
                pre[(t, d)] = ([jnp.where(k_head == h, q_t, 0.0).astype(BF16) for h in range(D_HEADS)],
                               (k * jnp.exp(jnp.minimum(g_mid - g, GLA_EXP_CAP))).astype(BF16),
                               (k * jnp.exp(g_last - g)).astype(BF16), v_b, v_t)
                qg_s[d, c] = (q * jnp.exp(g)).astype(BF16)
                el_s[d, c] = jnp.broadcast_to(jnp.exp(g_last), (SUBLANES, kw))
                units += [(t, d, h) for h in range(D_HEADS)]
        a_ = [jnp.where(masks[d][0], _dot_nt(pre[(t, d)][0][h], pre[(t, d)][1]), 0.0).astype(BF16)
              for t, d, h in units]
        ds_ = {td: jnp.dot(p[4], p[2], preferred_element_type=F32) for td, p in pre.items()}
        av = [jnp.dot(a_[i], pre[(t, d)][3], preferred_element_type=F32) for i, (t, d, h) in enumerate(units)]
        i = 0
        for t in range(GLA_LOCAL_CHUNKS):
            c = it * GLA_LOCAL_CHUNKS + t
            r0 = pl.multiple_of(c * CHUNK, CHUNK)
            o_const = jnp.zeros((CHUNK, vw), F32)
            for d in range(2):
                ds_s[d, c] = jnp.where(state_diag, ds_[(t, d)], 0.0)
                for h in range(D_HEADS):
                    o_const = o_const + jnp.where(v_head == h, av[i], 0.0)
                    i += 1
            o_s[pl.ds(r0, CHUNK), :] = o_const
        return carry

    lax.fori_loop(0, n_chunks // GLA_LOCAL_CHUNKS, local, 0)

    def scan(it, states):
        states = list(states)
        jobs = []
        for t in range(GLA_SCAN_STEPS):
            step = it * GLA_SCAN_STEPS + t
            for d in range(2):
                c = _scan_chunk(step, d, n_ctx_chunks, n_chunks)
                jobs.append((c, qg_s[d, c], states[d].astype(BF16)))
                states[d] = states[d] * el_s[d, c][0:1, :] + ds_s[d, c]
        outs = [_dot_nt(qg, sb) for _, qg, sb in jobs]
        for (c, _, _), o in zip(jobs, outs):
            r0 = pl.multiple_of(c * CHUNK, CHUNK)
            o_s[pl.ds(r0, CHUNK), :] = o_s[pl.ds(r0, CHUNK), :] + o
        return tuple(states)

    zero = jnp.zeros((vw, kw), F32)
    lax.fori_loop(0, n_chunks // GLA_SCAN_STEPS, scan, (zero, zero))
    _gated_out_tiles((o_s,), gate_ref, gain_ref, bd_ref, o_ref)


def _gla_call(d_qk, d_v, d_lr, d_g, gw_blk, gb_row, gain, ones_bd):
    nb, nt, _ = d_qk.shape
    vw = D_HEADS * HEAD_DIM
    kw2 = 2 * D_HEADS * D_KDIM
    return pl.pallas_call(
        _gla_kernel,
        grid=(nb,),
        in_specs=[
            pl.BlockSpec((1, nt, kw2), lambda b: (b, 0, 0)),
            pl.BlockSpec((1, nt, vw), lambda b: (b, 0, 0)),
            pl.BlockSpec((1, nt, LANES), lambda b: (b, 0, 0)),
            pl.BlockSpec((1, nt, vw), lambda b: (b, 0, 0)),
            pl.BlockSpec((LANES, kw2), lambda b: (0, 0)),
            pl.BlockSpec((1, kw2), lambda b: (0, 0)),
            pl.BlockSpec((1, vw), lambda b: (0, 0)),
            pl.BlockSpec((vw, vw), lambda b: (0, 0)),
        ],
        out_specs=pl.BlockSpec((1, nt, vw), lambda b: (b, 0, 0)),
        out_shape=jax.ShapeDtypeStruct((nb, nt, vw), BF16),
        scratch_shapes=[
            pltpu.VMEM((nt, kw2), F32),
            pltpu.VMEM((2, nt // CHUNK, CHUNK, kw2 // 2), BF16),
            pltpu.VMEM((2, nt // CHUNK, SUBLANES, kw2 // 2), F32),
            pltpu.VMEM((2, nt // CHUNK, vw, kw2 // 2), F32),
            pltpu.VMEM((nt, vw), F32),
        ],
        compiler_params=_params(("arbitrary",)),
        name="mixer_d_gla",
    )(d_qk, d_v, d_lr, d_g, gw_blk, gb_row, gain, ones_bd)


ROUTE_E1, ROUTE_E2, ROUTE_W1, ROUTE_W2 = 0, 1, 2, 3
ROUTER_EXPERT_LANE = N_GROUPS


def _route(logits):
    lane = lax.broadcasted_iota(jnp.int32, logits.shape, 1).astype(F32)
    far = float(LANES)
    in_grp = lane < N_GROUPS
    lg = jnp.where(in_grp, logits, NEG_BIG)
    mg = jnp.max(lg, axis=-1, keepdims=True)
    grp = jnp.min(jnp.where(lg == mg, lane, far), axis=-1, keepdims=True)
    p_grp = 1.0 / jnp.sum(jnp.where(in_grp, jnp.exp(lg - mg), 0.0), axis=-1, keepdims=True)
    lo = ROUTER_EXPERT_LANE + EXP_PER_GROUP * grp
    in_exp = jnp.logical_and(lane >= lo, lane < lo + EXP_PER_GROUP)
    le = jnp.where(in_exp, logits, NEG_BIG)
    m1 = jnp.max(le, axis=-1, keepdims=True)
    i1 = jnp.min(jnp.where(le == m1, lane, far), axis=-1, keepdims=True)
    le2 = jnp.where(lane == i1, NEG_BIG, le)
    m2 = jnp.max(le2, axis=-1, keepdims=True)
    i2 = jnp.min(jnp.where(le2 == m2, lane, far), axis=-1, keepdims=True)
    e2 = jnp.exp(m2 - m1)
    w1 = p_grp / (1.0 + e2)
    w2 = p_grp * e2 / (1.0 + e2)
    out = jnp.where(lane == ROUTE_E1, i1 - ROUTER_EXPERT_LANE, 0.0)
    out = jnp.where(lane == ROUTE_E2, i2 - ROUTER_EXPERT_LANE, out)
    out = jnp.where(lane == ROUTE_W1, w1, out)
    return jnp.where(lane == ROUTE_W2, w2, out)


def _merge_kernel(x_ref, mod_ref, oa_ref, ob_ref, oc_ref, od_ref, wg_ref, wbr_ref, wo_ref,
                  lng_ref, lnb_ref, wr_ref, br_ref, x1_ref, h2_ref, route_ref):
    x = x_ref[0]
    mod = mod_ref[0]
    h = (x * (1.0 + mod[1:2]) + mod[0:1]).astype(BF16)
    m = None
    for z, o_ref in enumerate((oa_ref, ob_ref, oc_ref, od_ref)):
        gate = jax.nn.sigmoid(jnp.dot(h, wg_ref[:, z * D_MODEL:(z + 1) * D_MODEL], preferred_element_type=F32))
        up = jnp.dot(o_ref[0], wbr_ref[z], preferred_element_type=F32)
        m = gate * up if m is None else m + gate * up
    y = jnp.dot(m.astype(BF16), wo_ref[...], preferred_element_type=F32)
    x1 = _layer_norm(DN_ALPHA * x + mod[2:3] * y, lng_ref[...], lnb_ref[...])
    x1_ref[0] = x1
    h2 = x1 * (1.0 + mod[4:5]) + mod[3:4]
    for j in range(ROW_VREGS):
        h2_ref[pl.ds(j, TOKEN_TILE, stride=ROW_VREGS), :] = h2[:, j * LANES:(j + 1) * LANES]
    route_ref[0] = _route(_dot3(h2, wr_ref[...]) + br_ref[...])


def _merge_call(xa, mods, oa, ob, oc, od, wg, wbr, wo, ln_g, ln_b, wr, br):
    nb, nt, d = xa.shape
    tiles = nt // TOKEN_TILE
    bw = oa.shape[-1]
    tok = lambda b, t: (b, t, 0)
    const2 = lambda b, t: (0, 0)
    return pl.pallas_call(
        _merge_kernel,
        grid=(nb, tiles),
        in_specs=[
            pl.BlockSpec((1, TOKEN_TILE, d), tok),
            pl.BlockSpec((1, SUBLANES, d), _mod_index(nb)),
            pl.BlockSpec((1, TOKEN_TILE, bw), tok), pl.BlockSpec((1, TOKEN_TILE, bw), tok),
            pl.BlockSpec((1, TOKEN_TILE, bw), tok), pl.BlockSpec((1, TOKEN_TILE, bw), tok),
            pl.BlockSpec((d, N_BRANCH * d), const2),
            pl.BlockSpec((N_BRANCH, bw, d), lambda b, t: (0, 0, 0)),
            pl.BlockSpec((d, d), const2),
            pl.BlockSpec((1, d), const2), pl.BlockSpec((1, d), const2),
            pl.BlockSpec((d, LANES), const2), pl.BlockSpec((1, LANES), const2),
        ],
        out_specs=[pl.BlockSpec((1, TOKEN_TILE, d), tok),
                   pl.BlockSpec((TOKEN_TILE * ROW_VREGS, LANES), lambda b, t: (b * tiles + t, 0)),
                   pl.BlockSpec((1, TOKEN_TILE, LANES), tok)],
        out_shape=[jax.ShapeDtypeStruct((nb, nt, d), F32),
                   jax.ShapeDtypeStruct((nb * nt * ROW_VREGS, LANES), F32),
                   jax.ShapeDtypeStruct((nb, nt, LANES), F32)],
        compiler_params=_params(("arbitrary", "arbitrary")),
        name="merge_out_ln1_router",
    )(xa, mods, oa, ob, oc, od, wg, wbr, wo, ln_g, ln_b, wr, br)


MOE_UNROLL = 4


def _moe_chunk(total):
    return max(c for c in range(TOKEN_TILE, MOE_CHUNK + 1, TOKEN_TILE) if total % c == 0)


def _moe_kernel(cnt_ref, off_ref, idx_ref, wt_ref, x_ref, wg_ref, wu_ref, wd_ref, y_ref, xt_s, ot_s):
    c = pl.program_id(0)
    e = pl.program_id(1)
    chunk = x_ref.shape[1] // ROW_VREGS

    @pl.when(e == 0)
    def _zero():
        y_ref[...] = jnp.zeros(y_ref.shape, F32)

    n_rows = cnt_ref[c * N_EXPERTS + e]
    off = off_ref[c * N_EXPERTS + e]

    def slab(tok):
        return pl.ds(pl.multiple_of(tok * ROW_VREGS, ROW_VREGS), ROW_VREGS)

    def tile_body(i, carry):
        base = off + i * MOE_ROWS
        valid = n_rows - i * MOE_ROWS
        for mi in range(MOE_ROWS):
            tok = jnp.minimum(idx_ref[0, 0, base + mi], chunk - 1)
            xt_s[pl.ds(mi, ROW_VREGS, stride=MOE_STRIDE), :] = x_ref[0, slab(tok), :]
        xt = jnp.concatenate([xt_s[j * MOE_STRIDE:j * MOE_STRIDE + MOE_ROWS, :] for j in range(ROW_VREGS)],
                             axis=-1).astype(BF16)
        a = _silu(jnp.dot(xt, wg_ref[0], preferred_element_type=F32)) * jnp.dot(xt, wu_ref[0], preferred_element_type=F32)
        out = jnp.dot(a.astype(BF16), wd_ref[0], preferred_element_type=F32)
        for j in range(ROW_VREGS):
            ot_s[j * MOE_STRIDE:j * MOE_STRIDE + MOE_ROWS, :] = out[:, j * LANES:(j + 1) * LANES]
        for m0 in range(0, MOE_ROWS, MOE_UNROLL):
            pending = []
            for mi in range(m0, m0 + MOE_UNROLL):
                ok = mi < valid
                rows = slab(jnp.where(ok, idx_ref[0, 0, base + mi], chunk))
                wgt = jnp.where(ok, wt_ref[0, 0, base + mi], 0.0)
                upd = y_ref[0, rows, :] + wgt * ot_s[pl.ds(mi, ROW_VREGS, stride=MOE_STRIDE), :]
                pending.append((rows, upd))
            for rows, upd in pending:
                y_ref[0, rows, :] = upd
        return carry

    lax.fori_loop(0, (n_rows + MOE_ROWS - 1) // MOE_ROWS, tile_body, 0)


def _moe_call(h2_slab, cnt, off, idx_sorted, w_sorted, wg, wu, wd):
    nch, rows_in, _ = h2_slab.shape
    plan = idx_sorted.shape[2]
    rows_out = rows_in + SUBLANES * ROW_VREGS
    grid_spec = pltpu.PrefetchScalarGridSpec(
        num_scalar_prefetch=2,
        grid=(nch, N_EXPERTS),
        in_specs=[
            pl.BlockSpec((1, 1, plan), lambda c, e, *_: (c, 0, 0), memory_space=pltpu.SMEM),
            pl.BlockSpec((1, 1, plan), lambda c, e, *_: (c, 0, 0), memory_space=pltpu.SMEM),
            pl.BlockSpec((1, rows_in, LANES), lambda c, e, *_: (c, 0, 0)),
            pl.BlockSpec((1, D_MODEL, EXP_HIDDEN), lambda c, e, *_: (e, 0, 0)),
            pl.BlockSpec((1, D_MODEL, EXP_HIDDEN), lambda c, e, *_: (e, 0, 0)),
            pl.BlockSpec((1, EXP_HIDDEN, D_MODEL), lambda c, e, *_: (e, 0, 0)),
        ],
        out_specs=pl.BlockSpec((1, rows_out, LANES), lambda c, e, *_: (c, 0, 0)),
        scratch_shapes=[pltpu.VMEM((ROW_VREGS * MOE_STRIDE, LANES), F32),
                        pltpu.VMEM((ROW_VREGS * MOE_STRIDE, LANES), F32)],
    )
    return pl.pallas_call(
        _moe_kernel,
        grid_spec=grid_spec,
        out_shape=jax.ShapeDtypeStruct((nch, rows_out, LANES), F32),
        compiler_params=_params(("arbitrary", "arbitrary")),
        name="moe_experts",
    )(cnt, off, idx_sorted, w_sorted, h2_slab, wg, wu, wd)


def _moe_plan(route, chunk):
    t = route.shape[0]
    nch = t // chunk
    per = chunk * 2
    eid = route[:, ROUTE_E1:ROUTE_E2 + 1].astype(jnp.int32).reshape(nch, per)
    wts = route[:, ROUTE_W1:ROUTE_W2 + 1].reshape(nch, per)
    order = jnp.argsort(eid, axis=1, stable=True).astype(jnp.int32)
    idx_sorted = jnp.pad(order // 2, ((0, 0), (0, MOE_ROWS)), constant_values=chunk)
    w_sorted = jnp.pad(jnp.take_along_axis(wts, order, axis=1), ((0, 0), (0, MOE_ROWS)))
    cnt = jnp.sum((eid[..., None] == jnp.arange(N_EXPERTS, dtype=jnp.int32)).astype(jnp.int32), axis=1)
    off = jnp.cumsum(cnt, axis=1) - cnt
    return (cnt.reshape(-1).astype(jnp.int32), off.reshape(-1).astype(jnp.int32),
            idx_sorted.reshape(nch, 1, per + MOE_ROWS), w_sorted.reshape(nch, 1, per + MOE_ROWS))


def _ln2_kernel(x_ref, y_ref, mod_ref, g_ref, b_ref, o_ref):
    mod = mod_ref[0]
    y = jnp.concatenate([y_ref[0, pl.ds(j, TOKEN_TILE, stride=ROW_VREGS), :] for j in range(ROW_VREGS)], axis=-1)
    o_ref[0] = _layer_norm(DN_ALPHA * x_ref[0] + mod[5:6] * y, g_ref[...], b_ref[...])


def _ln2_call(x1, y_slab, mods, ln_g, ln_b, chunk):
    nb, nt, d = x1.shape
    tiles = nt // TOKEN_TILE
    per_chunk = chunk // TOKEN_TILE
    tok = lambda b, t: (b, t, 0)
    slab = lambda b, t: ((b * tiles + t) // per_chunk, (b * tiles + t) % per_chunk, 0)
    return pl.pallas_call(
        _ln2_kernel,
        grid=(nb, tiles),
        in_specs=[pl.BlockSpec((1, TOKEN_TILE, d), tok),
                  pl.BlockSpec((1, TOKEN_TILE * ROW_VREGS, LANES), slab),
                  pl.BlockSpec((1, SUBLANES, d), _mod_index(nb)),
                  pl.BlockSpec((1, d), lambda b, t: (0, 0)), pl.BlockSpec((1, d), lambda b, t: (0, 0))],
        out_specs=pl.BlockSpec((1, TOKEN_TILE, d), tok),
        out_shape=jax.ShapeDtypeStruct((nb, nt, d), F32),
        compiler_params=_params(("arbitrary", "arbitrary")),
        name="moe_residual_ln2",
    )(x1, y_slab, mods, ln_g, ln_b)


def _head_constants():
    w = A_HEADS * HEAD_DIM
    i = np.arange(w)
    ones_bd = (i[:, None] // HEAD_DIM == i[None, :] // HEAD_DIM).astype(np.float32)
    quarter = HEAD_DIM // 4
    rot = np.zeros((w, w), np.float32)
    first = (i % (2 * quarter)) < quarter
    rot[i[first] + quarter, i[first]] = -1.0
    rot[i[~first] - quarter, i[~first]] = 1.0
    return jnp.asarray(ones_bd, BF16), jnp.asarray(rot, BF16)


def _rope_tables(seq):
    t = jnp.arange(seq, dtype=jnp.int32)
    row = (t // GRID_W).astype(F32)
    col = (t % GRID_W).astype(F32)
    nf = HEAD_DIM // 4
    inv = ROPE_THETA ** (-jnp.arange(nf, dtype=F32) / nf)
    ang_r = row[:, None] * inv
    ang_c = col[:, None] * inv
    cos = jnp.concatenate([jnp.cos(ang_r), jnp.cos(ang_r), jnp.cos(ang_c), jnp.cos(ang_c)], axis=-1)
    sin = jnp.concatenate([jnp.sin(ang_r), jnp.sin(ang_r), jnp.sin(ang_c), jnp.sin(ang_c)], axis=-1)
    return jnp.tile(cos, (1, A_HEADS)), jnp.tile(sin, (1, A_HEADS))


def _in_weight(w_in_l):
    cols = []
    for _, parts, width, _ in IN_GROUPS:
        got = 0
        for p in parts:
            o, n = _IN_OFFS[p]
            cols.append(w_in_l[:, o:o + n])
            got += n
        if got < width:
            cols.append(jnp.zeros((w_in_l.shape[0], width - got), w_in_l.dtype))
    return jnp.concatenate(cols, axis=1).astype(BF16)


def _lane_row(vec, width, offset=0):
    return jnp.zeros((1, width), F32).at[0, offset:offset + vec.shape[0]].set(vec.astype(F32))


def kernel(x, c, ctx, c_ctx, w_ada, b_ada, w_in, a_q_gain, a_k_gain, b_rpb, c_conv, c_a_log, c_dt_bias, c_out_gain, d_gate_w, d_gate_b, d_out_gain, w_branch, w_out, ln1_g, ln1_b, ln2_g, ln2_b, w_router_g, b_router_g, w_router_e, b_router_e, w_up, w_gate, w_down):
    nb, seq, d = x.shape
    depth = w_ada.shape[0]
    nt = CTX_LEN + seq
    assert d == D_MODEL and ctx.shape[1] == CTX_LEN and nb + 1 <= ADA_ROWS
    assert seq % TOKEN_TILE == 0
    chunk = _moe_chunk(nb * nt)

    xa = jnp.concatenate([ctx, x], axis=1)
    cc = jnp.zeros((ADA_ROWS, d), F32).at[:nb].set(c).at[nb].set(c_ctx)
    mods = _ada_call(cc, w_ada, b_ada).reshape(depth, ADA_ROWS, 6, d)[:, :nb + 1]
    mods = jnp.pad(mods, ((0, 0), (0, 0), (0, SUBLANES - 6), (0, 0)))

    ones_bd, rot_m = _head_constants()
    cos_t, sin_t = _rope_tables(seq)
    gates_off = _IN_OFFS['gates'][0]

    for l in range(depth):
        proj = dict(zip([g[0] for g in IN_GROUPS], _in_call(xa, mods[l], _in_weight(w_in[l]))))

        oa = _attn_a_call(proj['a_q'], proj['a_kv'], cos_t, sin_t,
                          jnp.tile(a_q_gain[l], A_HEADS)[None, :], jnp.tile(a_k_gain[l], A_KV_HEADS)[None, :],
                          ones_bd, rot_m)
        ob = _attn_b_call(proj['b_q'], proj['b_k'], proj['b_v'], _nb_bias_table(b_rpb[l]))
        conv_w = jnp.pad(c_conv[l], ((0, SUBLANES - c_conv.shape[1]), (0, 0)))
        par = jnp.concatenate([_lane_row(c_a_log[l].reshape(-1), LANES, C_A_LANE),
                               _lane_row(c_dt_bias[l].reshape(-1), LANES, C_A_LANE),
                               jnp.zeros((SUBLANES - 2, LANES), F32)], axis=0)
        oc = _gdn_call(proj['c_qkv'], proj['c_ba'], proj['c_g'], conv_w, par,
                       jnp.tile(c_out_gain[l], C_HEADS)[None, :], ones_bd)
        kw = D_HEADS * D_KDIM
        gw_blk = jnp.zeros((LANES, 2 * kw), F32)
        gw_blk = gw_blk.at[0:D_GATE_RANK, 0:kw].set(d_gate_w[l, 0])
        gw_blk = gw_blk.at[D_GATE_RANK:2 * D_GATE_RANK, kw:2 * kw].set(d_gate_w[l, 1])
        od = _gla_call(proj['d_qk'], proj['d_v'], proj['d_lr'], proj['d_g'], gw_blk,
                       d_gate_b[l].reshape(1, 2 * kw), jnp.tile(d_out_gain[l], D_HEADS)[None, :], ones_bd)

        wr = jnp.concatenate([w_router_g[l], jnp.transpose(w_router_e[l], (1, 0, 2)).reshape(d, N_EXPERTS)], axis=1)
        wr = jnp.pad(wr, ((0, 0), (0, LANES - wr.shape[1])))
        br = _lane_row(jnp.concatenate([b_router_g[l], b_router_e[l].reshape(-1)]), LANES)
        x1, h2, route = _merge_call(
            xa, mods[l], oa, ob, oc, od, w_in[l][:, gates_off:].astype(BF16), w_branch[l].astype(BF16),
            w_out[l].astype(BF16), ln1_g[l][None, :], ln1_b[l][None, :], wr, br)

        cnt, off, idx_sorted, w_sorted = _moe_plan(route.reshape(nb * nt, LANES), chunk)
        h2_slab = h2.reshape((nb * nt) // chunk, chunk * ROW_VREGS, LANES)
        y_slab = _moe_call(h2_slab, cnt, off, idx_sorted, w_sorted,
                           w_gate[l].astype(BF16), w_up[l].astype(BF16), w_down[l].astype(BF16))
        xa = _ln2_call(x1, y_slab, mods[l], ln2_g[l][None, :], ln2_b[l][None, :], chunk)

    return xa[:, CTX_LEN:, :]
```

```python
import functools
import math

import numpy as np
import jax
import jax.numpy as jnp
from jax import lax
from jax.experimental import pallas as pl
from jax.experimental.pallas import tpu as pltpu

F32 = jnp.float32
BF16 = jnp.bfloat16

D_MODEL = 1024
DEPTH = 4
GRID_W = 64
CTX_LEN = 256
HEAD_DIM = 64
A_HEADS = 4
A_KV_HEADS = 2
ROPE_THETA = 10000.0
B_HEADS = 4
WIN_R = 8
WIN_C = 16
C_HEADS = 4
D_HEADS = 4
D_KDIM = 32
D_GATE_RANK = 16
GLA_TAU = 16.0
CHUNK = 64
N_BRANCH = 4
N_GROUPS = 4
EXP_PER_GROUP = 8
N_EXPERTS = N_GROUPS * EXP_PER_GROUP
EXP_HIDDEN = 512
EPS = 1e-6
DN_ALPHA = (2.0 * DEPTH) ** 0.25
NEG_BIG = -1e30

LANES = 128
SUBLANES = 8
TOKEN_TILE = 256
VMEM_LIMIT = 56 * 1024 * 1024

_IN_OFFS = {}
_off = 0
for _n, _w in (('a_q', 256), ('a_k', 128), ('a_v', 128), ('b_q', 256), ('b_k', 256), ('b_v', 256),
               ('c_qkv', 768), ('c_beta', 8), ('c_a', 8), ('c_g', 256), ('d_q', 128), ('d_k', 128),
               ('d_v', 256), ('d_lr', 32), ('d_g', 256), ('gates', 4096)):
    _IN_OFFS[_n] = (_off, _w)
    _off += _w
IN_GROUPS = (
    ('a_q', ('a_q',), 256, BF16),
    ('a_kv', ('a_k', 'a_v'), 256, BF16),
    ('b_q', ('b_q',), 256, BF16),
    ('b_k', ('b_k',), 256, BF16),
    ('b_v', ('b_v',), 256, BF16),
    ('c_qkv', ('c_qkv',), 768, BF16),
    ('c_ba', ('c_beta', 'c_a'), 128, F32),
    ('c_g', ('c_g',), 256, F32),
    ('d_qk', ('d_q', 'd_k'), 256, F32),
    ('d_v', ('d_v',), 256, F32),
    ('d_lr', ('d_lr',), 128, F32),
    ('d_g', ('d_g',), 256, F32),
)
IN_TOTAL = sum(g[2] for g in IN_GROUPS)

MOE_CHUNK = 2048
MOE_ROWS = 160
MOE_STRIDE = MOE_ROWS + SUBLANES
ROW_VREGS = D_MODEL // LANES


def _dot(a, b):
    return jnp.dot(a.astype(BF16), b.astype(BF16), preferred_element_type=F32)


def _dot_nt(a, b):
    return lax.dot_general(a.astype(BF16), b.astype(BF16), (((1,), (1,)), ((), ())),
                           preferred_element_type=F32)


def _dot_tn(a, b):
    return lax.dot_general(a.astype(BF16), b.astype(BF16), (((0,), (0,)), ((), ())),
                           preferred_element_type=F32)


def _split(x):
    hi = x.astype(BF16)
    lo = (x - hi.astype(F32)).astype(BF16)
    return hi, lo


def _dot3(a, b):
    ah, al = _split(a)
    bh, bl = _split(b)
    return (jnp.dot(ah, bh, preferred_element_type=F32) + jnp.dot(al, bh, preferred_element_type=F32)
            + jnp.dot(ah, bl, preferred_element_type=F32))


def _dot_sel(a, m):
    ah, al = _split(a)
    return jnp.dot(ah, m, preferred_element_type=F32) + jnp.dot(al, m, preferred_element_type=F32)


def _silu(x):
    return x * jax.nn.sigmoid(x)


def _layer_norm(r, g, b):
    mu = jnp.mean(r, axis=-1, keepdims=True)
    d = r - mu
    var = jnp.mean(d * d, axis=-1, keepdims=True)
    return d * lax.rsqrt(var + EPS) * g + b


def _params(sem):
    return pltpu.CompilerParams(dimension_semantics=sem, vmem_limit_bytes=VMEM_LIMIT)


ADA_ROWS = 24
ADA_TILE = 1536


def _ada_kernel(cc_ref, w_ref, b_ref, o_ref):
    s = _silu(cc_ref[...])
    o_ref[0] = _dot3(s, w_ref[0]) + b_ref[0]


def _ada_call(cc, w_ada, b_ada):
    depth = w_ada.shape[0]
    n = w_ada.shape[2]
    return pl.pallas_call(
        _ada_kernel,
        grid=(depth, n // ADA_TILE),
        in_specs=[
            pl.BlockSpec((ADA_ROWS, D_MODEL), lambda l, j: (0, 0)),
            pl.BlockSpec((1, D_MODEL, ADA_TILE), lambda l, j: (l, 0, j)),
            pl.BlockSpec((1, 1, ADA_TILE), lambda l, j: (l, 0, j)),
        ],
        out_specs=pl.BlockSpec((1, ADA_ROWS, ADA_TILE), lambda l, j: (l, 0, j)),
        out_shape=jax.ShapeDtypeStruct((depth, ADA_ROWS, n), F32),
        compiler_params=_params(("arbitrary", "arbitrary")),
        name="ada_mod",
    )(cc, w_ada, b_ada.reshape(depth, 1, n))


def _mod_index(nb):
    return lambda b, t: (jnp.where(t == 0, nb, b), 0, 0)


def _in_kernel(x_ref, mod_ref, w_ref, *out_refs):
    x = x_ref[0]
    mod = mod_ref[0]
    h = (x * (1.0 + mod[1:2]) + mod[0:1]).astype(BF16)
    off = 0
    for (name, _, width, dt), o_ref in zip(IN_GROUPS, out_refs):
        o_ref[0] = jnp.dot(h, w_ref[:, off:off + width], preferred_element_type=F32).astype(dt)
        off += width


def _in_call(xa, mods, w_cat):
    nb, nt, _ = xa.shape
    tiles = nt // TOKEN_TILE
    return pl.pallas_call(
        _in_kernel,
        grid=(nb, tiles),
        in_specs=[
            pl.BlockSpec((1, TOKEN_TILE, D_MODEL), lambda b, t: (b, t, 0)),
            pl.BlockSpec((1, SUBLANES, D_MODEL), _mod_index(nb)),
            pl.BlockSpec((D_MODEL, IN_TOTAL), lambda b, t: (0, 0)),
        ],
        out_specs=[pl.BlockSpec((1, TOKEN_TILE, g[2]), lambda b, t: (b, t, 0)) for g in IN_GROUPS],
        out_shape=[jax.ShapeDtypeStruct((nb, nt, g[2]), g[3]) for g in IN_GROUPS],
        compiler_params=_params(("arbitrary", "arbitrary")),
        name="in_proj",
    )(xa, mods, w_cat)


def _head_rms(x, ones_bd, gain):
    ss = _dot_sel(x * x, ones_bd)
    return x * lax.rsqrt(ss * (1.0 / HEAD_DIM) + EPS) * gain


def _rope(x, rot, cos, sin):
    return x * cos + _dot_sel(x, rot) * sin


def _attn_a_kernel(q_ref, kv_ref, cos_ref, sin_ref, qg_ref, kg_ref, bd_ref, rot_ref, o_ref, kp_ref, vt_ref):
    t = pl.program_id(1)
    kvw = A_KV_HEADS * HEAD_DIM
    rep = A_HEADS // A_KV_HEADS
    assert rep * HEAD_DIM == kvw == LANES
    n_tiles = kv_ref.shape[1] // TOKEN_TILE
    scale = HEAD_DIM ** -0.5
    lane = lax.broadcasted_iota(jnp.int32, (TOKEN_TILE, kvw), 1)
    first_half = lane < HEAD_DIM

    @pl.when(t == 0)
    def _prep_keys():
        bd = bd_ref[0:kvw, 0:kvw]
        rot = rot_ref[0:kvw, 0:kvw]
        kg = kg_ref[...]

        def put(i, k_rows, k):
            swapped = pltpu.roll(k, HEAD_DIM, axis=1)
            kp_ref[0, k_rows, :] = jnp.where(first_half, k, swapped).astype(BF16)
            kp_ref[1, k_rows, :] = jnp.where(first_half, swapped, k).astype(BF16)
            vt_ref[i] = kv_ref[0, k_rows, kvw:2 * kvw].astype(F32).T.astype(BF16)

        put(0, pl.ds(0, TOKEN_TILE), _head_rms(kv_ref[0, 0:TOKEN_TILE, 0:kvw].astype(F32), bd, kg))

        def body(i, carry):
            r0 = pl.multiple_of(i * TOKEN_TILE, TOKEN_TILE)
            rows = pl.ds(CTX_LEN + r0, TOKEN_TILE)
            kn = _head_rms(kv_ref[0, rows, 0:kvw].astype(F32), bd, kg)
            put(i + CTX_LEN // TOKEN_TILE, rows,
                _rope(kn, rot, cos_ref[pl.ds(r0, TOKEN_TILE), 0:kvw], sin_ref[pl.ds(r0, TOKEN_TILE), 0:kvw]))
            return carry

        lax.fori_loop(0, n_tiles - CTX_LEN // TOKEN_TILE, body, 0)

    qn = _head_rms(q_ref[0].astype(F32), bd_ref[...], qg_ref[...])

    def attend(qh, key_tiles):
        nk = key_tiles * TOKEN_TILE
        heads = [(g, r) for g in range(A_KV_HEADS) for r in range(rep)]
        qms = [jnp.where(first_half == (r == 0), qh[:, g * kvw:(g + 1) * kvw] * scale, 0.0).astype(BF16)
               for g, r in heads]
        ss = [_dot_nt(kp_ref[g, 0:nk, :], qms[i]) for i, (g, r) in enumerate(heads)]
        es = [jnp.exp(s - jnp.max(s, axis=0, keepdims=True)) for s in ss]
        ls = [jnp.sum(e, axis=0, keepdims=True) for e in es]
        ebs = [e.astype(BF16) for e in es]
        outs = []
        for i, (g, r) in enumerate(heads):
            o = None
            for k in range(key_tiles):
                part = jnp.dot(vt_ref[k, g * HEAD_DIM:(g + 1) * HEAD_DIM, :],
                               ebs[i][k * TOKEN_TILE:(k + 1) * TOKEN_TILE, :], preferred_element_type=F32)
                o = part if o is None else o + part
            outs.append(o / ls[i])
        return jnp.concatenate(outs, axis=0).T

    @pl.when(t == 0)
    def _ctx_queries():
        o_ref[0] = attend(qn, CTX_LEN // TOKEN_TILE).astype(o_ref.dtype)

    @pl.when(t > 0)
    def _latent_queries():
        r0 = pl.multiple_of((t - 1) * TOKEN_TILE, TOKEN_TILE)
        qr = _rope(qn, rot_ref[...], cos_ref[pl.ds(r0, TOKEN_TILE), :], sin_ref[pl.ds(r0, TOKEN_TILE), :])
        o_ref[0] = attend(qr, n_tiles).astype(o_ref.dtype)


def _attn_a_call(a_q, a_kv, cos_t, sin_t, q_gain, k_gain, ones_bd, rot_m):
    nb, nt, _ = a_q.shape
    tiles = nt // TOKEN_TILE
    seq = nt - CTX_LEN
    qw = A_HEADS * HEAD_DIM
    return pl.pallas_call(
        _attn_a_kernel,
        grid=(nb, tiles),
        in_specs=[
            pl.BlockSpec((1, TOKEN_TILE, qw), lambda b, t: (b, t, 0)),
            pl.BlockSpec((1, nt, qw), lambda b, t: (b, 0, 0)),
            pl.BlockSpec((seq, qw), lambda b, t: (0, 0)),
            pl.BlockSpec((seq, qw), lambda b, t: (0, 0)),
            pl.BlockSpec((1, qw), lambda b, t: (0, 0)),
            pl.BlockSpec((1, A_KV_HEADS * HEAD_DIM), lambda b, t: (0, 0)),
            pl.BlockSpec((qw, qw), lambda b, t: (0, 0)),
            pl.BlockSpec((qw, qw), lambda b, t: (0, 0)),
        ],
        out_specs=pl.BlockSpec((1, TOKEN_TILE, qw), lambda b, t: (b, t, 0)),
        out_shape=jax.ShapeDtypeStruct((nb, nt, qw), BF16),
        scratch_shapes=[pltpu.VMEM((A_KV_HEADS, nt, A_KV_HEADS * HEAD_DIM), BF16),
                        pltpu.VMEM((tiles, A_KV_HEADS * HEAD_DIM, TOKEN_TILE), BF16)],
        compiler_params=_params(("arbitrary", "arbitrary")),
        name="mixer_a_gqa",
    )(a_q, a_kv, cos_t, sin_t, q_gain, k_gain, ones_bd, rot_m)


NB_QROWS = TOKEN_TILE // GRID_W
NB_KROWS = 12
NB_INVALID = 2 * WIN_R - 1


def _attn_b_kernel(q_ref, k_ref, v_ref, bt_ref, o_ref):
    t = pl.program_id(1)
    scale = HEAD_DIM ** -0.5
    rows = (k_ref.shape[1] - CTX_LEN) // GRID_W
    wr = min(WIN_R, rows)

    def softmax_pv(parts):
        m = None
        for s, _ in parts:
            mi = jnp.max(s, axis=-1, keepdims=True)
            m = mi if m is None else jnp.maximum(m, mi)
        acc, l = None, None
        for s, v in parts:
            e = jnp.exp(s - m)
            li = jnp.sum(e, axis=-1, keepdims=True)
            oi = jnp.dot(e.astype(BF16), v, preferred_element_type=F32)
            acc = oi if acc is None else acc + oi
            l = li if l is None else l + li
        return acc / l

    @pl.when(t == 0)
    def _ctx_queries():
        outs = []
        for h in range(B_HEADS):
            sl = slice(h * HEAD_DIM, (h + 1) * HEAD_DIM)
            qq = (q_ref[0, :, sl].astype(F32) * scale).astype(BF16)
            s = _dot_nt(qq, k_ref[0, 0:CTX_LEN, sl])
            outs.append(softmax_pv([(s, v_ref[0, 0:CTX_LEN, sl])]))
        o_ref[0] = jnp.concatenate(outs, axis=-1).astype(o_ref.dtype)

    @pl.when(t > 0)
    def _latent_queries():
        r0 = (t - 1) * NB_QROWS
        start = jnp.clip(r0 - wr // 2, 0, rows - NB_KROWS)
        k0 = pl.multiple_of(CTX_LEN + start * GRID_W, GRID_W)
        nk = NB_KROWS * GRID_W
        lane = lax.broadcasted_iota(jnp.int32, (GRID_W, 2 * GRID_W), 1)
        left = lane < GRID_W
        slots = []
        for i in range(NB_QROWS):
            r = r0 + i
            rs = jnp.clip(r - wr // 2, 0, rows - wr)
            row_slots = []
            for j in range(NB_KROWS):
                kr = start + j
                ok = jnp.logical_and(kr >= rs, kr < rs + wr)
                row_slots.append(jnp.where(ok, kr - r + WIN_R - 1, NB_INVALID))
            slots.append(row_slots)
        sls = [slice(h * HEAD_DIM, (h + 1) * HEAD_DIM) for h in range(B_HEADS)]
        qqs = [(q_ref[0, :, sl].astype(F32) * scale).astype(BF16) for sl in sls]
        s_locs = [_dot_nt(qqs[h], k_ref[0, pl.ds(k0, nk), sls[h]]) for h in range(B_HEADS)]
        s_ctxs = [_dot_nt(qqs[h], k_ref[0, 0:CTX_LEN, sls[h]]) for h in range(B_HEADS)]
        probs = []
        for h in range(B_HEADS):
            bias_rows = []
            for i in range(NB_QROWS):
                tiles = []
                for jp in range(NB_KROWS // 2):
                    b0 = bt_ref[h, slots[i][2 * jp]]
                    b1 = bt_ref[h, slots[i][2 * jp + 1]]
                    tiles.append(jnp.where(left, b0, b1))
                bias_rows.append(jnp.concatenate(tiles, axis=-1))
            s_loc = s_locs[h] + jnp.concatenate(bias_rows, axis=0)
            m = jnp.maximum(jnp.max(s_loc, axis=-1, keepdims=True), jnp.max(s_ctxs[h], axis=-1, keepdims=True))
            e_loc = jnp.exp(s_loc - m)
            e_ctx = jnp.exp(s_ctxs[h] - m)
            probs.append((e_loc.astype(BF16), e_ctx.astype(BF16),
                          jnp.sum(e_loc, axis=-1, keepdims=True) + jnp.sum(e_ctx, axis=-1, keepdims=True)))
        outs = [(jnp.dot(probs[h][0], v_ref[0, pl.ds(k0, nk), sls[h]], preferred_element_type=F32)
                 + jnp.dot(probs[h][1], v_ref[0, 0:CTX_LEN, sls[h]], preferred_element_type=F32)) / probs[h][2]
                for h in range(B_HEADS)]
        o_ref[0] = jnp.concatenate(outs, axis=-1).astype(o_ref.dtype)


def _attn_b_call(b_q, b_k, b_v, bias_tab):
    nb, nt, w = b_q.shape
    tiles = nt // TOKEN_TILE
    return pl.pallas_call(
        _attn_b_kernel,
        grid=(nb, tiles),
        in_specs=[
            pl.BlockSpec((1, TOKEN_TILE, w), lambda b, t: (b, t, 0)),
            pl.BlockSpec((1, nt, w), lambda b, t: (b, 0, 0)),
            pl.BlockSpec((1, nt, w), lambda b, t: (b, 0, 0)),
            pl.BlockSpec(bias_tab.shape, lambda b, t: (0, 0, 0, 0)),
        ],
        out_specs=pl.BlockSpec((1, TOKEN_TILE, w), lambda b, t: (b, t, 0)),
        out_shape=jax.ShapeDtypeStruct((nb, nt, w), BF16),
        compiler_params=_params(("arbitrary", "arbitrary")),
        name="mixer_b_neighbourhood",
    )(b_q, b_k, b_v, bias_tab)


def _nb_bias_table(rpb):
    cols = jnp.arange(GRID_W, dtype=jnp.int32)
    col_start = jnp.clip(cols - WIN_C // 2, 0, GRID_W - WIN_C)
    col_ok = (cols[None, :] >= col_start[:, None]) & (cols[None, :] < col_start[:, None] + WIN_C)
    dc_idx = jnp.clip(cols[None, :] - cols[:, None] + WIN_C - 1, 0, 2 * WIN_C - 2)
    tab = rpb.astype(F32)[:, :, dc_idx]
    tab = jnp.where(col_ok[None, None], tab, NEG_BIG)
    tab = jnp.concatenate([tab, jnp.full_like(tab[:, :1], NEG_BIG)], axis=1)
    return jnp.concatenate([tab, tab], axis=-1)


def _order_masks():
    i = lax.broadcasted_iota(jnp.int32, (CHUNK, CHUNK), 0)
    j = lax.broadcasted_iota(jnp.int32, (CHUNK, CHUNK), 1)
    return ((j <= i, j < i), (j >= i, j > i))


def _scan_chunk(step, direction, n_ctx_chunks, n_chunks):
    if direction == 0:
        return step
    return jnp.where(step < n_ctx_chunks, n_ctx_chunks - 1 - step, n_chunks + n_ctx_chunks - 1 - step)


def _sel_dot(m, a):
    ah, al = _split(a)
    return jnp.dot(m, ah, preferred_element_type=F32) + jnp.dot(m, al, preferred_element_type=F32)


def _gated_out_tiles(part_refs, gate_ref, gain_ref, bd_ref, o_ref):
    nt = part_refs[0].shape[0]
    bd = bd_ref[...]
    gain = gain_ref[...]

    def body(i, carry):
        r0 = pl.multiple_of(i * TOKEN_TILE, TOKEN_TILE)
        o = sum(p[pl.ds(r0, TOKEN_TILE), :] for p in part_refs)
        y = _head_rms(o, bd, gain) * _silu(gate_ref[0, pl.ds(r0, TOKEN_TILE), :])
        o_ref[0, pl.ds(r0, TOKEN_TILE), :] = y.astype(o_ref.dtype)
        return carry

    lax.fori_loop(0, nt // TOKEN_TILE, body, 0)


C_BETA_LANE = 0
C_A_LANE = 8
C_T_ROWS = 16


SOLVE_BLOCK = 16
GDN_LOCAL_CHUNKS = 4


def _unit_lower_solve(lmats, rhss, same_block, eye):
    n = range(len(lmats))
    lds = [jnp.where(same_block, l, 0.0) for l in lmats]
    ts = [eye - ld for ld in lds]
    ps = lds
    span = 2
    while span < SOLVE_BLOCK:
        ps = [_dot(p, p) for p in ps]
        ts = [ts[i] + _dot(ts[i], ps[i]) for i in n]
        span *= 2
    width = lmats[0].shape[1]
    mzs = [_dot(ts[i], jnp.concatenate([lmats[i] - lds[i], rhss[i]], axis=-1)) for i in n]
    mmzs = [_dot(mz[:, 0:width], mz) for mz in mzs]
    zs = [mzs[i][:, width:] - mmzs[i][:, width:] for i in n]
    ps = [mmz[:, 0:width] for mmz in mmzs]
    span = 2
    while span < CHUNK // SOLVE_BLOCK:
        zs = [zs[i] + _dot(ps[i], zs[i]) for i in n]
        span *= 2
        if span < CHUNK // SOLVE_BLOCK:
            ps = [_dot(p, p) for p in ps]
    return zs


def _gdn_kernel(qkv_ref, ba_ref, gate_ref, conv_ref, par_ref, gain_ref, bd_ref, o_ref,
                q_s, k_s, v_s, bl_s, g_s, gt_s, a12_s, b2_s, egl_s, o_s, st_s):
    nt = qkv_ref.shape[1]
    n_tiles = nt // TOKEN_TILE
    n_chunks = nt // CHUNK
    n_ctx_chunks = CTX_LEN // CHUNK
    w = C_HEADS * HEAD_DIM
    pack = 2 * SUBLANES
    bd = bd_ref[...]
    lane = lax.broadcasted_iota(jnp.int32, (TOKEN_TILE, LANES), 1)
    lane_c = lax.broadcasted_iota(jnp.int32, (CHUNK, LANES), 1)
    row = lax.broadcasted_iota(jnp.int32, (TOKEN_TILE, 1), 0)
    neg_rate = -jnp.exp(par_ref[0:1, :])
    dt_bias = par_ref[1:2, :]
    w_prev, w_mid, w_next = conv_ref[0:1, :], conv_ref[1:2, :], conv_ref[2:3, :]
    masks = _order_masks()
    mask_bf = [jnp.where(m[0], 1.0, 0.0).astype(BF16) for m in masks]

    def prep(i, carry):
        r0 = pl.multiple_of(i * TOKEN_TILE, TOKEN_TILE)
        x = qkv_ref[0, pl.ds(r0, TOKEN_TILE), :].astype(F32)
        before = qkv_ref[0, pl.ds(pl.multiple_of(jnp.maximum(r0 - pack, 0), pack), pack), :].astype(F32)
        after = qkv_ref[0, pl.ds(pl.multiple_of(jnp.minimum(r0 + TOKEN_TILE, nt - pack), pack), pack), :].astype(F32)
        first_of_seq = jnp.logical_or(i == 0, i == CTX_LEN // TOKEN_TILE)
        last_of_seq = jnp.logical_or(i == CTX_LEN // TOKEN_TILE - 1, i == n_tiles - 1)
        edge_prev = jnp.where(first_of_seq, 0.0, before[pack - 1:pack, :])
        edge_next = jnp.where(last_of_seq, 0.0, after[0:1, :])
        x_prev = jnp.where(row == 0, edge_prev, pltpu.roll(x, 1, axis=0))
        x_next = jnp.where(row == TOKEN_TILE - 1, edge_next, pltpu.roll(x, TOKEN_TILE - 1, axis=0))
        y = _silu(x_prev * w_prev + x * w_mid + x_next * w_next)
        q, k, v = y[:, 0:w], y[:, w:2 * w], y[:, 2 * w:3 * w]
        q_s[pl.ds(r0, TOKEN_TILE), :] = q * lax.rsqrt(_dot_sel(q * q, bd) + EPS) * (HEAD_DIM ** -0.5)
        k_s[pl.ds(r0, TOKEN_TILE), :] = k * lax.rsqrt(_dot_sel(k * k, bd) + EPS)
        v_s[pl.ds(r0, TOKEN_TILE), :] = v
        ba = ba_ref[0, pl.ds(r0, TOKEN_TILE), :]
        sp = ba + dt_bias
        softplus = jnp.maximum(sp, 0.0) + jnp.log1p(jnp.exp(-jnp.abs(sp)))
        bl = jnp.where(lane < C_A_LANE, jax.nn.sigmoid(ba), neg_rate * softplus)
        bl_s[pl.ds(r0, TOKEN_TILE), :] = bl
        for c in range(TOKEN_TILE // CHUNK):
            blc = bl[c * CHUNK:(c + 1) * CHUNK, :]
            g = jnp.where(lane_c < C_A_LANE + C_HEADS, _sel_dot(mask_bf[0], blc), _sel_dot(mask_bf[1], blc))
            g_s[pl.ds(r0 + c * CHUNK, CHUNK), :] = g
            gt_s[i * (TOKEN_TILE // CHUNK) + c] = g.T[0:C_T_ROWS, :]
        return carry

    lax.fori_loop(0, n_tiles, prep, 0)

    ri = lax.broadcasted_iota(jnp.int32, (CHUNK, CHUNK), 0)
    ci = lax.broadcasted_iota(jnp.int32, (CHUNK, CHUNK), 1)
    same_block = (ri // SOLVE_BLOCK) == (ci // SOLVE_BLOCK)
    eye = jnp.where(ri == ci, 1.0, 0.0)

    src = lax.broadcasted_iota(jnp.int32, (LANES, w), 0)
    dst_head = lax.broadcasted_iota(jnp.int32, (LANES, w), 1) // HEAD_DIM
    spread_g = [jnp.where(src == C_A_LANE + d * C_HEADS + dst_head, 1.0, 0.0).astype(BF16) for d in range(2)]
    spread_b = [jnp.where(src == C_BETA_LANE + d * C_HEADS + dst_head, 1.0, 0.0).astype(BF16) for d in range(2)]
    heads = [slice(h * HEAD_DIM, (h + 1) * HEAD_DIM) for h in range(C_HEADS)]

    def local(it, carry):
        lmats, rhss, keep = [], [], []
        for t in range(GDN_LOCAL_CHUNKS):
            c = it * GDN_LOCAL_CHUNKS + t
            r0 = pl.multiple_of(c * CHUNK, CHUNK)
            qc = q_s[pl.ds(r0, CHUNK), :]
            kc = k_s[pl.ds(r0, CHUNK), :]
            vc = v_s[pl.ds(r0, CHUNK), :]
            bl = bl_s[pl.ds(r0, CHUNK), :]
            g_cols = g_s[pl.ds(r0, CHUNK), :]
            g_rows = gt_s[c]
            qks = [_dot_nt(qc[:, sl], kc[:, sl]) for sl in heads]
            for d in range(2):
                incl, strict = masks[d]
                last = CHUNK - 1 if d == 0 else 0
                g_all = _dot_sel(g_cols, spread_g[d])
                b_all = _dot_sel(bl, spread_b[d])
                e_g = jnp.exp(g_all)
                g_last = g_all[last:last + 1, :]
                kb = kc * b_all
                rhs_k = kb * e_g
                rhs_v = vc * b_all
                q_g = qc * e_g
                kd_t = (kc * jnp.exp(g_last - g_all)).T.astype(BF16)
                egl_s[d, c] = jnp.broadcast_to(jnp.exp(g_last), (SUBLANES, w))
                for h, sl in enumerate(heads):
                    la = C_A_LANE + d * C_HEADS + h
                    decay = jnp.exp(jnp.where(incl, g_all[:, sl] - g_rows[la:la + 1, :], NEG_BIG))
                    lmats.append(jnp.where(strict, _dot_nt(kb[:, sl], kc[:, sl]) * decay, 0.0))
                    rhss.append(jnp.concatenate([rhs_k[:, sl], rhs_v[:, sl]], axis=-1))
                    keep.append(((qks[h] * decay).astype(BF16), q_g[:, sl], kd_t[sl, :]))
        sols = [s.astype(BF16) for s in _unit_lower_solve(lmats, rhss, same_block, eye)]
        qwu = [jnp.dot(keep[i][0], sols[i], preferred_element_type=F32) for i in range(len(sols))]
        kwu = [jnp.dot(keep[i][2], sols[i], preferred_element_type=F32) for i in range(len(sols))]
        i = 0
        for t in range(GDN_LOCAL_CHUNKS):
            c = it * GDN_LOCAL_CHUNKS + t
            r0 = pl.multiple_of(c * CHUNK, CHUNK)
            o_const = None
            for d in range(2):
                u = range(i, i + C_HEADS)
                a12_s[d, c, 0:CHUNK, :] = jnp.concatenate(
                    [keep[j][1] - qwu[j][:, 0:HEAD_DIM] for j in u], axis=-1).astype(BF16)
                a12_s[d, c, CHUNK:2 * CHUNK, :] = jnp.concatenate(
                    [-kwu[j][:, 0:HEAD_DIM] for j in u], axis=-1).astype(BF16)
                b2_s[d, c] = jnp.concatenate([kwu[j][:, HEAD_DIM:] for j in u], axis=-1)
                part = jnp.concatenate([qwu[j][:, HEAD_DIM:] for j in u], axis=-1)
                o_const = part if o_const is None else o_const + part
                i += C_HEADS
            o_s[pl.ds(r0, CHUNK), :] = o_const
        return carry

    lax.fori_loop(0, n_chunks // GDN_LOCAL_CHUNKS, local, 0)

    st_s[...] = jnp.zeros(st_s.shape, F32)

    def scan(step, carry):
        units = [(d, h) for d in range(2) for h in range(C_HEADS)]
        cs = [_scan_chunk(step, d, n_ctx_chunks, n_chunks) for d in range(2)]
        a12 = [a12_s[d, cs[d]] for d in range(2)]
        b2 = [b2_s[d, cs[d]] for d in range(2)]
        egl = [egl_s[d, cs[d]] for d in range(2)]
        ss = [st_s[d, h] for d, h in units]
        rs = [jnp.dot(a12[d][:, heads[h]], ss[i].astype(BF16), preferred_element_type=F32)
              for i, (d, h) in enumerate(units)]
        for i, (d, h) in enumerate(units):
            st_s[d, h] = ss[i] * egl[d][0:1, heads[h]] + rs[i][CHUNK:, :] + b2[d][:, heads[h]]
        for d in range(2):
            r0 = pl.multiple_of(cs[d] * CHUNK, CHUNK)
            o = jnp.concatenate([rs[d * C_HEADS + h][0:CHUNK, :] for h in range(C_HEADS)], axis=-1)
            o_s[pl.ds(r0, CHUNK), :] = o_s[pl.ds(r0, CHUNK), :] + o
        return carry

    lax.fori_loop(0, n_chunks, scan, 0)
    _gated_out_tiles((o_s,), gate_ref, gain_ref, bd_ref, o_ref)


def _gdn_call(c_qkv, c_ba, c_g, conv_w, par, gain, ones_bd):
    nb, nt, _ = c_qkv.shape
    w = C_HEADS * HEAD_DIM
    nc = nt // CHUNK
    return pl.pallas_call(
        _gdn_kernel,
        grid=(nb,),
        in_specs=[
            pl.BlockSpec((1, nt, 3 * w), lambda b: (b, 0, 0)),
            pl.BlockSpec((1, nt, LANES), lambda b: (b, 0, 0)),
            pl.BlockSpec((1, nt, w), lambda b: (b, 0, 0)),
            pl.BlockSpec((SUBLANES, 3 * w), lambda b: (0, 0)),
            pl.BlockSpec((SUBLANES, LANES), lambda b: (0, 0)),
            pl.BlockSpec((1, w), lambda b: (0, 0)),
            pl.BlockSpec((w, w), lambda b: (0, 0)),
        ],
        out_specs=pl.BlockSpec((1, nt, w), lambda b: (b, 0, 0)),
        out_shape=jax.ShapeDtypeStruct((nb, nt, w), BF16),
        scratch_shapes=[
            pltpu.VMEM((nt, w), F32), pltpu.VMEM((nt, w), F32), pltpu.VMEM((nt, w), F32),
            pltpu.VMEM((nt, LANES), F32), pltpu.VMEM((nt, LANES), F32),
            pltpu.VMEM((nc, C_T_ROWS, CHUNK), F32),
            pltpu.VMEM((2, nc, 2 * CHUNK, w), BF16),
            pltpu.VMEM((2, nc, CHUNK, w), F32),
            pltpu.VMEM((2, nc, SUBLANES, w), F32),
            pltpu.VMEM((nt, w), F32),
            pltpu.VMEM((2, C_HEADS, HEAD_DIM, HEAD_DIM), F32),
        ],
        compiler_params=_params(("arbitrary",)),
        name="mixer_c_gated_delta",
    )(c_qkv, c_ba, c_g, conv_w, par, gain, ones_bd)


GLA_EXP_CAP = 80.0
GLA_LOCAL_CHUNKS = 2
GLA_SCAN_STEPS = 4


def _gla_kernel(qk_ref, v_ref, lr_ref, gate_ref, gw_ref, gb_ref, gain_ref, bd_ref, o_ref,
                la_s, qg_s, el_s, ds_s, o_s):
    nt = qk_ref.shape[1]
    n_tiles = nt // TOKEN_TILE
    n_chunks = nt // CHUNK
    n_ctx_chunks = CTX_LEN // CHUNK
    kw = D_HEADS * D_KDIM
    gw = gw_ref[...]
    gb = gb_ref[...]

    def prep(i, carry):
        r0 = pl.multiple_of(i * TOKEN_TILE, TOKEN_TILE)
        z = _dot3(lr_ref[0, pl.ds(r0, TOKEN_TILE), :], gw) + gb
        log_sig = jnp.minimum(z, 0.0) - jnp.log1p(jnp.exp(-jnp.abs(z)))
        la_s[pl.ds(r0, TOKEN_TILE), :] = log_sig * (1.0 / GLA_TAU)
        return carry

    lax.fori_loop(0, n_tiles, prep, 0)

    masks = _order_masks()
    mask_bf = [jnp.where(m[0], 1.0, 0.0).astype(BF16) for m in masks]
    vw = D_HEADS * HEAD_DIM
    k_head = lax.broadcasted_iota(jnp.int32, (CHUNK, kw), 1) // D_KDIM
    v_head = lax.broadcasted_iota(jnp.int32, (CHUNK, vw), 1) // HEAD_DIM
    state_diag = (lax.broadcasted_iota(jnp.int32, (vw, kw), 0) // HEAD_DIM
                  == lax.broadcasted_iota(jnp.int32, (vw, kw), 1) // D_KDIM)

    def local(it, carry):
        units, pre = [], {}
        for t in range(GLA_LOCAL_CHUNKS):
            c = it * GLA_LOCAL_CHUNKS + t
            r0 = pl.multiple_of(c * CHUNK, CHUNK)
            qk = qk_ref[0, pl.ds(r0, CHUNK), :]
            q = qk[:, 0:kw] * (D_KDIM ** -0.5)
            k = qk[:, kw:2 * kw]
            v = v_ref[0, pl.ds(r0, CHUNK), :]
            v_b = v.astype(BF16)
            v_t = v.T.astype(BF16)
            for d in range(2):
                last = CHUNK - 1 if d == 0 else 0
                g = _sel_dot(mask_bf[d], la_s[pl.ds(r0, CHUNK), d * kw:(d + 1) * kw])
                g_mid = g[CHUNK // 2:CHUNK // 2 + 1, :]
                g_last = g[last:last + 1, :]
                q_t = q * jnp.exp(jnp.minimum(g - g_mid, GLA_EXP_CAP))
                pre[(t, d)] = ([jnp.where(k_head == h, q_t, 0.0).astype(BF16) for h in range(D_HEADS)],
                               (k * jnp.exp(jnp.minimum(g_mid - g, GLA_EXP_CAP))).astype(BF16),
                               (k * jnp.exp(g_last - g)).astype(BF16), v_b, v_t)
                qg_s[d, c] = (q * jnp.exp(g)).astype(BF16)
                el_s[d, c] = jnp.broadcast_to(jnp.exp(g_last), (SUBLANES, kw))
                units += [(t, d, h) for h in range(D_HEADS)]
        a_ = [jnp.where(masks[d][0], _dot_nt(pre[(t, d)][0][h], pre[(t, d)][1]), 0.0).astype(BF16)
              for t, d, h in units]
        ds_ = {td: jnp.dot(p[4], p[2], preferred_element_type=F32) for td, p in pre.items()}
        av = [jnp.dot(a_[i], pre[(t, d)][3], preferred_element_type=F32) for i, (t, d, h) in enumerate(units)]
        i = 0
        for t in range(GLA_LOCAL_CHUNKS):
            c = it * GLA_LOCAL_CHUNKS + t
            r0 = pl.multiple_of(c * CHUNK, CHUNK)
            o_const = jnp.zeros((CHUNK, vw), F32)
            for d in range(2):
                ds_s[d, c] = jnp.where(state_diag, ds_[(t, d)], 0.0)
                for h in range(D_HEADS):
                    o_const = o_const + jnp.where(v_head == h, av[i], 0.0)
                    i += 1
            o_s[pl.ds(r0, CHUNK), :] = o_const
        return carry

    lax.fori_loop(0, n_chunks // GLA_LOCAL_CHUNKS, local, 0)

    def scan(it, states):
        states = list(states)
        jobs = []
        for t in range(GLA_SCAN_STEPS):
            step = it * GLA_SCAN_STEPS + t
            for d in range(2):
                c = _scan_chunk(step, d, n_ctx_chunks, n_chunks)
                jobs.append((c, qg_s[d, c], states[d].astype(BF16)))
                states[d] = states[d] * el_s[d, c][0:1, :] + ds_s[d, c]
        outs = [_dot_nt(qg, sb) for _, qg, sb in jobs]
        for (c, _, _), o in zip(jobs, outs):
            r0 = pl.multiple_of(c * CHUNK, CHUNK)
            o_s[pl.ds(r0, CHUNK), :] = o_s[pl.ds(r0, CHUNK), :] + o
        return tuple(states)

    zero = jnp.zeros((vw, kw), F32)
    lax.fori_loop(0, n_chunks // GLA_SCAN_STEPS, scan, (zero, zero))
    _gated_out_tiles((o_s,), gate_ref, gain_ref, bd_ref, o_ref)


def _gla_call(d_qk, d_v, d_lr, d_g, gw_blk, gb_row, gain, ones_bd):
    nb, nt, _ = d_qk.shape
    vw = D_HEADS * HEAD_DIM
    kw2 = 2 * D_HEADS * D_KDIM
    return pl.pallas_call(
        _gla_kernel,
        grid=(nb,),
        in_specs=[
            pl.BlockSpec((1, nt, kw2), lambda b: (b, 0, 0)),
            pl.BlockSpec((1, nt, vw), lambda b: (b, 0, 0)),
            pl.BlockSpec((1, nt, LANES), lambda b: (b, 0, 0)),
            pl.BlockSpec((1, nt, vw), lambda b: (b, 0, 0)),
            pl.BlockSpec((LANES, kw2), lambda b: (0, 0)),
            pl.BlockSpec((1, kw2), lambda b: (0, 0)),
            pl.BlockSpec((1, vw), lambda b: (0, 0)),
            pl.BlockSpec((vw, vw), lambda b: (0, 0)),
        ],
        out_specs=pl.BlockSpec((1, nt, vw), lambda b: (b, 0, 0)),
        out_shape=jax.ShapeDtypeStruct((nb, nt, vw), BF16),
        scratch_shapes=[
            pltpu.VMEM((nt, kw2), F32),
            pltpu.VMEM((2, nt // CHUNK, CHUNK, kw2 // 2), BF16),
            pltpu.VMEM((2, nt // CHUNK, SUBLANES, kw2 // 2), F32),
            pltpu.VMEM((2, nt // CHUNK, vw, kw2 // 2), F32),
            pltpu.VMEM((nt, vw), F32),
        ],
        compiler_params=_params(("arbitrary",)),
        name="mixer_d_gla",
    )(d_qk, d_v, d_lr, d_g, gw_blk, gb_row, gain, ones_bd)


ROUTE_E1, ROUTE_E2, ROUTE_W1, ROUTE_W2 = 0, 1, 2, 3
ROUTER_EXPERT_LANE = N_GROUPS


def _route(logits):
    lane = lax.broadcasted_iota(jnp.int32, logits.shape, 1).astype(F32)
    far = float(LANES)
    in_grp = lane < N_GROUPS
    lg = jnp.where(in_grp, logits, NEG_BIG)
    mg = jnp.max(lg, axis=-1, keepdims=True)
    grp = jnp.min(jnp.where(lg == mg, lane, far), axis=-1, keepdims=True)
    p_grp = 1.0 / jnp.sum(jnp.where(in_grp, jnp.exp(lg - mg), 0.0), axis=-1, keepdims=True)
    lo = ROUTER_EXPERT_LANE + EXP_PER_GROUP * grp
    in_exp = jnp.logical_and(lane >= lo, lane < lo + EXP_PER_GROUP)
    le = jnp.where(in_exp, logits, NEG_BIG)
    m1 = jnp.max(le, axis=-1, keepdims=True)
    i1 = jnp.min(jnp.where(le == m1, lane, far), axis=-1, keepdims=True)
    le2 = jnp.where(lane == i1, NEG_BIG, le)
    m2 = jnp.max(le2, axis=-1, keepdims=True)
    i2 = jnp.min(jnp.where(le2 == m2, lane, far), axis=-1, keepdims=True)
    e2 = jnp.exp(m2 - m1)
    w1 = p_grp / (1.0 + e2)
    w2 = p_grp * e2 / (1.0 + e2)
    out = jnp.where(lane == ROUTE_E1, i1 - ROUTER_EXPERT_LANE, 0.0)
    out = jnp.where(lane == ROUTE_E2, i2 - ROUTER_EXPERT_LANE, out)
    out = jnp.where(lane == ROUTE_W1, w1, out)
    return jnp.where(lane == ROUTE_W2, w2, out)


def _merge_kernel(x_ref, mod_ref, oa_ref, ob_ref, oc_ref, od_ref, wg_ref, wbr_ref, wo_ref,
                  lng_ref, lnb_ref, wr_ref, br_ref, x1_ref, h2_ref, route_ref):
    x = x_ref[0]
    mod = mod_ref[0]
    h = (x * (1.0 + mod[1:2]) + mod[0:1]).astype(BF16)
    m = None
    for z, o_ref in enumerate((oa_ref, ob_ref, oc_ref, od_ref)):
        gate = jax.nn.sigmoid(jnp.dot(h, wg_ref[:, z * D_MODEL:(z + 1) * D_MODEL], preferred_element_type=F32))
        up = jnp.dot(o_ref[0], wbr_ref[z], preferred_element_type=F32)
        m = gate * up if m is None else m + gate * up
    y = jnp.dot(m.astype(BF16), wo_ref[...], preferred_element_type=F32)
    x1 = _layer_norm(DN_ALPHA * x + mod[2:3] * y, lng_ref[...], lnb_ref[...])
    x1_ref[0] = x1
    h2 = x1 * (1.0 + mod[4:5]) + mod[3:4]
    for j in range(ROW_VREGS):
        h2_ref[pl.ds(j, TOKEN_TILE, stride=ROW_VREGS), :] = h2[:, j * LANES:(j + 1) * LANES]
    route_ref[0] = _route(_dot3(h2, wr_ref[...]) + br_ref[...])


def _merge_call(xa, mods, oa, ob, oc, od, wg, wbr, wo, ln_g, ln_b, wr, br):
    nb, nt, d = xa.shape
    tiles = nt // TOKEN_TILE
    bw = oa.shape[-1]
    tok = lambda b, t: (b, t, 0)
    const2 = lambda b, t: (0, 0)
    return pl.pallas_call(
        _merge_kernel,
        grid=(nb, tiles),
        in_specs=[
            pl.BlockSpec((1, TOKEN_TILE, d), tok),
            pl.BlockSpec((1, SUBLANES, d), _mod_index(nb)),
            pl.BlockSpec((1, TOKEN_TILE, bw), tok), pl.BlockSpec((1, TOKEN_TILE, bw), tok),
            pl.BlockSpec((1, TOKEN_TILE, bw), tok), pl.BlockSpec((1, TOKEN_TILE, bw), tok),
            pl.BlockSpec((d, N_BRANCH * d), const2),
            pl.BlockSpec((N_BRANCH, bw, d), lambda b, t: (0, 0, 0)),
            pl.BlockSpec((d, d), const2),
            pl.BlockSpec((1, d), const2), pl.BlockSpec((1, d), const2),
            pl.BlockSpec((d, LANES), const2), pl.BlockSpec((1, LANES), const2),
        ],
        out_specs=[pl.BlockSpec((1, TOKEN_TILE, d), tok),
                   pl.BlockSpec((TOKEN_TILE * ROW_VREGS, LANES), lambda b, t: (b * tiles + t, 0)),
                   pl.BlockSpec((1, TOKEN_TILE, LANES), tok)],
        out_shape=[jax.ShapeDtypeStruct((nb, nt, d), F32),
                   jax.ShapeDtypeStruct((nb * nt * ROW_VREGS, LANES), F32),
                   jax.ShapeDtypeStruct((nb, nt, LANES), F32)],
        compiler_params=_params(("arbitrary", "arbitrary")),
        name="merge_out_ln1_router",
    )(xa, mods, oa, ob, oc, od, wg, wbr, wo, ln_g, ln_b, wr, br)


MOE_UNROLL = 4


def _moe_chunk(total):
    return max(c for c in range(TOKEN_TILE, MOE_CHUNK + 1, TOKEN_TILE) if total % c == 0)


MOE_GROUP = 2
MOE_PARTS = 2


def _moe_kernel(cnt_ref, off_ref, off_again_ref, idx_ref, wt_ref, x_ref, *refs):
    w_refs = [refs[3 * g:3 * g + 3] for g in range(MOE_GROUP)]
    y_ref = refs[3 * MOE_GROUP]
    xt_s = refs[3 * MOE_GROUP + 1:3 * MOE_GROUP + 1 + MOE_GROUP]
    ot_s = refs[3 * MOE_GROUP + 1 + MOE_GROUP:]
    c = pl.program_id(0)
    j = pl.program_id(1)
    chunk = x_ref.shape[1] // ROW_VREGS
    plan_rows = 2 * chunk

    @pl.when(j == 0)
    def _zero():
        y_ref[...] = jnp.zeros(y_ref.shape, F32)

    n_rows = [cnt_ref[c * N_EXPERTS + j * MOE_GROUP + g] for g in range(MOE_GROUP)]
    off = [off_ref[c * N_EXPERTS + j * MOE_GROUP + g] for g in range(MOE_GROUP)]
    off_again = [off_again_ref[c * N_EXPERTS + j * MOE_GROUP + g] for g in range(MOE_GROUP)]
    n_tiles = functools.reduce(jnp.maximum, [(n + MOE_ROWS - 1) // MOE_ROWS for n in n_rows])

    def slab(tok):
        return pl.ds(pl.multiple_of(tok * ROW_VREGS, ROW_VREGS), ROW_VREGS)

    def gather(base, xt, lo, hi):
        for mi in range(lo, hi):
            xt[pl.ds(mi, ROW_VREGS, stride=MOE_STRIDE), :] = x_ref[0, slab(idx_ref[0, 0, base + mi]), :]

    def rows_of(xt):
        return jnp.concatenate([xt[k * MOE_STRIDE:k * MOE_STRIDE + MOE_ROWS, :] for k in range(ROW_VREGS)],
                               axis=-1).astype(BF16)

    def expert_part(x, p, wg_ref, wu_ref, wd_ref):
        cols = slice(p * EXP_HIDDEN // MOE_PARTS, (p + 1) * EXP_HIDDEN // MOE_PARTS)
        a = (_silu(jnp.dot(x, wg_ref[0, :, cols], preferred_element_type=F32))
             * jnp.dot(x, wu_ref[0, :, cols], preferred_element_type=F32))
        return jnp.dot(a.astype(BF16), wd_ref[0, cols, :], preferred_element_type=F32)

    def put(ot, out):
        for k in range(ROW_VREGS):
            ot[k * MOE_STRIDE:k * MOE_STRIDE + MOE_ROWS, :] = out[:, k * LANES:(k + 1) * LANES]

    def scatter(base, valid, ot, lo, hi):
        for m0 in range(lo, hi, MOE_UNROLL):
            pending = []
            for mi in range(m0, m0 + MOE_UNROLL):
                ok = mi < valid
                rows = slab(jnp.where(ok, idx_ref[0, 0, base + mi], chunk))
                wgt = jnp.where(ok, wt_ref[0, 0, base + mi], 0.0)
                upd = y_ref[0, rows, :] + wgt * ot[pl.ds(mi, ROW_VREGS, stride=MOE_STRIDE), :]
                pending.append((rows, upd))
            for rows, upd in pending:
                y_ref[0, rows, :] = upd

    cuts = [MOE_UNROLL * round(s * MOE_ROWS / ((MOE_PARTS + 1) * MOE_UNROLL)) for s in range(MOE_PARTS + 2)]

    def tile_body(i, carry):
        base = [jnp.minimum(off[g] + i * MOE_ROWS, plan_rows) for g in range(MOE_GROUP)]
        base_s = [jnp.minimum(off_again[g] + i * MOE_ROWS, plan_rows) for g in range(MOE_GROUP)]
        valid = [n_rows[g] - i * MOE_ROWS for g in range(MOE_GROUP)]
        gather(base[0], xt_s[0], 0, MOE_ROWS)
        x0 = rows_of(xt_s[0])
        out0 = None
        for p in range(MOE_PARTS):
            gather(base[1], xt_s[1], cuts[p], cuts[p + 1])
            part = expert_part(x0, p, *w_refs[0])
            out0 = part if out0 is None else out0 + part
        gather(base[1], xt_s[1], cuts[MOE_PARTS], MOE_ROWS)
        put(ot_s[0], out0)
        x1 = rows_of(xt_s[1])
        out1 = None
        for p in range(MOE_PARTS):
            scatter(base_s[0], valid[0], ot_s[0], cuts[p], cuts[p + 1])
            part = expert_part(x1, p, *w_refs[1])
            out1 = part if out1 is None else out1 + part
        scatter(base_s[0], valid[0], ot_s[0], cuts[MOE_PARTS], MOE_ROWS)
        put(ot_s[1], out1)
        scatter(base_s[1], valid[1], ot_s[1], 0, MOE_ROWS)
        return carry

    lax.fori_loop(0, n_tiles, tile_body, 0)


def _moe_call(h2_slab, cnt, off, idx_sorted, w_sorted, wg, wu, wd):
    nch, rows_in, _ = h2_slab.shape
    plan = idx_sorted.shape[2]
    rows_out = rows_in + SUBLANES * ROW_VREGS
    w_specs, w_args = [], []
    for g in range(MOE_GROUP):
        pick = lambda c, j, *_, g=g: (j * MOE_GROUP + g, 0, 0)
        w_specs += [pl.BlockSpec((1, D_MODEL, EXP_HIDDEN), pick), pl.BlockSpec((1, D_MODEL, EXP_HIDDEN), pick),
                    pl.BlockSpec((1, EXP_HIDDEN, D_MODEL), pick)]
        w_args += [wg, wu, wd]
    tile_buf = pltpu.VMEM((ROW_VREGS * MOE_STRIDE, LANES), F32)
    grid_spec = pltpu.PrefetchScalarGridSpec(
        num_scalar_prefetch=3,
        grid=(nch, N_EXPERTS // MOE_GROUP),
        in_specs=[
            pl.BlockSpec((1, 1, plan), lambda c, j, *_: (c, 0, 0), memory_space=pltpu.SMEM),
            pl.BlockSpec((1, 1, plan), lambda c, j, *_: (c, 0, 0), memory_space=pltpu.SMEM),
            pl.BlockSpec((1, rows_in, LANES), lambda c, j, *_: (c, 0, 0)),
        ] + w_specs,
        out_specs=pl.BlockSpec((1, rows_out, LANES), lambda c, j, *_: (c, 0, 0)),
        scratch_shapes=[tile_buf] * (2 * MOE_GROUP),
    )
    return pl.pallas_call(
        _moe_kernel,
        grid_spec=grid_spec,
        out_shape=jax.ShapeDtypeStruct((nch, rows_out, LANES), F32),
        compiler_params=_params(("arbitrary", "arbitrary")),
        name="moe_experts",
    )(cnt, off, off, idx_sorted, w_sorted, h2_slab, *w_args)


def _moe_plan(route, chunk):
    t = route.shape[0]
    nch = t // chunk
    per = chunk * 2
    eid = route[:, ROUTE_E1:ROUTE_E2 + 1].astype(jnp.int32).reshape(nch, per)
    wts = route[:, ROUTE_W1:ROUTE_W2 + 1].reshape(nch, per)
    order = jnp.argsort(eid, axis=1, stable=True).astype(jnp.int32)
    idx_sorted = jnp.pad(order // 2, ((0, 0), (0, MOE_ROWS)))
    w_sorted = jnp.pad(jnp.take_along_axis(wts, order, axis=1), ((0, 0), (0, MOE_ROWS)))
    cnt = jnp.sum((eid[..., None] == jnp.arange(N_EXPERTS, dtype=jnp.int32)).astype(jnp.int32), axis=1)
    off = jnp.cumsum(cnt, axis=1) - cnt
    return (cnt.reshape(-1).astype(jnp.int32), off.reshape(-1).astype(jnp.int32),
            idx_sorted.reshape(nch, 1, per + MOE_ROWS), w_sorted.reshape(nch, 1, per + MOE_ROWS))


def _ln2_kernel(x_ref, y_ref, mod_ref, g_ref, b_ref, o_ref):
    mod = mod_ref[0]
    y = jnp.concatenate([y_ref[0, pl.ds(j, TOKEN_TILE, stride=ROW_VREGS), :] for j in range(ROW_VREGS)], axis=-1)
    o_ref[0] = _layer_norm(DN_ALPHA * x_ref[0] + mod[5:6] * y, g_ref[...], b_ref[...])


def _ln2_call(x1, y_slab, mods, ln_g, ln_b, chunk, latent_only):
    nb, nt, d = x1.shape
    tiles = nt // TOKEN_TILE
    per_chunk = chunk // TOKEN_TILE
    ctx_tiles = CTX_LEN // TOKEN_TILE
    tok = lambda b, t: (b, t, 0)
    slab = lambda b, t: ((b * tiles + t) // per_chunk, (b * tiles + t) % per_chunk, 0)
    if latent_only:
        out_rows, out_map = nt - CTX_LEN, lambda b, t: (b, jnp.maximum(t - ctx_tiles, 0), 0)
    else:
        out_rows, out_map = nt, tok
    return pl.pallas_call(
        _ln2_kernel,
        grid=(nb, tiles),
        in_specs=[pl.BlockSpec((1, TOKEN_TILE, d), tok),
                  pl.BlockSpec((1, TOKEN_TILE * ROW_VREGS, LANES), slab),
                  pl.BlockSpec((1, SUBLANES, d), _mod_index(nb)),
                  pl.BlockSpec((1, d), lambda b, t: (0, 0)), pl.BlockSpec((1, d), lambda b, t: (0, 0))],
        out_specs=pl.BlockSpec((1, TOKEN_TILE, d), out_map),
        out_shape=jax.ShapeDtypeStruct((nb, out_rows, d), F32),
        compiler_params=_params(("arbitrary", "arbitrary")),
        name="moe_residual_ln2",
    )(x1, y_slab, mods, ln_g, ln_b)


def _head_constants():
    w = A_HEADS * HEAD_DIM
    i = np.arange(w)
    ones_bd = (i[:, None] // HEAD_DIM == i[None, :] // HEAD_DIM).astype(np.float32)
    quarter = HEAD_DIM // 4
    rot = np.zeros((w, w), np.float32)
    first = (i % (2 * quarter)) < quarter
    rot[i[first] + quarter, i[first]] = -1.0
    rot[i[~first] - quarter, i[~first]] = 1.0
    return jnp.asarray(ones_bd, BF16), jnp.asarray(rot, BF16)


def _rope_tables(seq):
    t = jnp.arange(seq, dtype=jnp.int32)
    row = (t // GRID_W).astype(F32)
    col = (t % GRID_W).astype(F32)
    nf = HEAD_DIM // 4
    inv = ROPE_THETA ** (-jnp.arange(nf, dtype=F32) / nf)
    ang_r = row[:, None] * inv
    ang_c = col[:, None] * inv
    cos = jnp.concatenate([jnp.cos(ang_r), jnp.cos(ang_r), jnp.cos(ang_c), jnp.cos(ang_c)], axis=-1)
    sin = jnp.concatenate([jnp.sin(ang_r), jnp.sin(ang_r), jnp.sin(ang_c), jnp.sin(ang_c)], axis=-1)
    return jnp.tile(cos, (1, A_HEADS)), jnp.tile(sin, (1, A_HEADS))


def _in_weight(w_in_l):
    cols = []
    for _, parts, width, _ in IN_GROUPS:
        got = 0
        for p in parts:
            o, n = _IN_OFFS[p]
            cols.append(w_in_l[:, o:o + n])
            got += n
        if got < width:
            cols.append(jnp.zeros((w_in_l.shape[0], width - got), w_in_l.dtype))
    return jnp.concatenate(cols, axis=1).astype(BF16)


def _lane_row(vec, width, offset=0):
    return jnp.zeros((1, width), F32).at[0, offset:offset + vec.shape[0]].set(vec.astype(F32))


def kernel(x, c, ctx, c_ctx, w_ada, b_ada, w_in, a_q_gain, a_k_gain, b_rpb, c_conv, c_a_log, c_dt_bias, c_out_gain, d_gate_w, d_gate_b, d_out_gain, w_branch, w_out, ln1_g, ln1_b, ln2_g, ln2_b, w_router_g, b_router_g, w_router_e, b_router_e, w_up, w_gate, w_down):
    nb, seq, d = x.shape
    depth = w_ada.shape[0]
    nt = CTX_LEN + seq
    assert d == D_MODEL and ctx.shape[1] == CTX_LEN and nb + 1 <= ADA_ROWS
    assert seq % TOKEN_TILE == 0
    chunk = _moe_chunk(nb * nt)

    xa = jnp.concatenate([ctx, x], axis=1)
    cc = jnp.zeros((ADA_ROWS, d), F32).at[:nb].set(c).at[nb].set(c_ctx)
    mods = _ada_call(cc, w_ada, b_ada).reshape(depth, ADA_ROWS, 6, d)[:, :nb + 1]
    mods = jnp.pad(mods, ((0, 0), (0, 0), (0, SUBLANES - 6), (0, 0)))

    ones_bd, rot_m = _head_constants()
    cos_t, sin_t = _rope_tables(seq)
    gates_off = _IN_OFFS['gates'][0]

    for l in range(depth):
        proj = dict(zip([g[0] for g in IN_GROUPS], _in_call(xa, mods[l], _in_weight(w_in[l]))))

        oa = _attn_a_call(proj['a_q'], proj['a_kv'], cos_t, sin_t,
                          jnp.tile(a_q_gain[l], A_HEADS)[None, :], jnp.tile(a_k_gain[l], A_KV_HEADS)[None, :],
                          ones_bd, rot_m)
        ob = _attn_b_call(proj['b_q'], proj['b_k'], proj['b_v'], _nb_bias_table(b_rpb[l]))
        conv_w = jnp.pad(c_conv[l], ((0, SUBLANES - c_conv.shape[1]), (0, 0)))
        par = jnp.concatenate([_lane_row(c_a_log[l].reshape(-1), LANES, C_A_LANE),
                               _lane_row(c_dt_bias[l].reshape(-1), LANES, C_A_LANE),
                               jnp.zeros((SUBLANES - 2, LANES), F32)], axis=0)
        oc = _gdn_call(proj['c_qkv'], proj['c_ba'], proj['c_g'], conv_w, par,
                       jnp.tile(c_out_gain[l], C_HEADS)[None, :], ones_bd)
        kw = D_HEADS * D_KDIM
        gw_blk = jnp.zeros((LANES, 2 * kw), F32)
        gw_blk = gw_blk.at[0:D_GATE_RANK, 0:kw].set(d_gate_w[l, 0])
        gw_blk = gw_blk.at[D_GATE_RANK:2 * D_GATE_RANK, kw:2 * kw].set(d_gate_w[l, 1])
        od = _gla_call(proj['d_qk'], proj['d_v'], proj['d_lr'], proj['d_g'], gw_blk,
                       d_gate_b[l].reshape(1, 2 * kw), jnp.tile(d_out_gain[l], D_HEADS)[None, :], ones_bd)

        wr = jnp.concatenate([w_router_g[l], jnp.transpose(w_router_e[l], (1, 0, 2)).reshape(d, N_EXPERTS)], axis=1)
        wr = jnp.pad(wr, ((0, 0), (0, LANES - wr.shape[1])))
        br = _lane_row(jnp.concatenate([b_router_g[l], b_router_e[l].reshape(-1)]), LANES)
        x1, h2, route = _merge_call(
            xa, mods[l], oa, ob, oc, od, w_in[l][:, gates_off:].astype(BF16), w_branch[l].astype(BF16),
            w_out[l].astype(BF16), ln1_g[l][None, :], ln1_b[l][None, :], wr, br)

        cnt, off, idx_sorted, w_sorted = _moe_plan(route.reshape(nb * nt, LANES), chunk)
        h2_slab = h2.reshape((nb * nt) // chunk, chunk * ROW_VREGS, LANES)
        y_slab = _moe_call(h2_slab, cnt, off, idx_sorted, w_sorted,
                           w_gate[l].astype(BF16), w_up[l].astype(BF16), w_down[l].astype(BF16))
        xa = _ln2_call(x1, y_slab, mods[l], ln2_g[l][None, :], ln2_b[l][None, :], chunk,
                       latent_only=l == depth - 1)

    return xa
```

```python
import functools
import math

import numpy as np
import jax
import jax.numpy as jnp
from jax import lax
from jax.experimental import pallas as pl
from jax.experimental.pallas import tpu as pltpu

F32 = jnp.float32
BF16 = jnp.bfloat16

D_MODEL = 1024
DEPTH = 4
GRID_W = 64
CTX_LEN = 256
HEAD_DIM = 64
A_HEADS = 4
A_KV_HEADS = 2
ROPE_THETA = 10000.0
B_HEADS = 4
WIN_R = 8
WIN_C = 16
C_HEADS = 4
D_HEADS = 4
D_KDIM = 32
D_GATE_RANK = 16
GLA_TAU = 16.0
CHUNK = 64
N_BRANCH = 4
N_GROUPS = 4
EXP_PER_GROUP = 8
N_EXPERTS = N_GROUPS * EXP_PER_GROUP
EXP_HIDDEN = 512
EPS = 1e-6
DN_ALPHA = (2.0 * DEPTH) ** 0.25
NEG_BIG = -1e30

LANES = 128
SUBLANES = 8
TOKEN_TILE = 256
VMEM_LIMIT = 56 * 1024 * 1024

_IN_OFFS = {}
_off = 0
for _n, _w in (('a_q', 256), ('a_k', 128), ('a_v', 128), ('b_q', 256), ('b_k', 256), ('b_v', 256),
               ('c_qkv', 768), ('c_beta', 8), ('c_a', 8), ('c_g', 256), ('d_q', 128), ('d_k', 128),
               ('d_v', 256), ('d_lr', 32), ('d_g', 256), ('gates', 4096)):
    _IN_OFFS[_n] = (_off, _w)
    _off += _w
IN_GROUPS = (
    ('a_q', ('a_q',), 256, BF16),
    ('a_kv', ('a_k', 'a_v'), 256, BF16),
    ('b_q', ('b_q',), 256, BF16),
    ('b_k', ('b_k',), 256, BF16),
    ('b_v', ('b_v',), 256, BF16),
    ('c_qkv', ('c_qkv',), 768, BF16),
    ('c_ba', ('c_beta', 'c_a'), 128, F32),
    ('c_g', ('c_g',), 256, F32),
    ('d_qk', ('d_q', 'd_k'), 256, F32),
    ('d_v', ('d_v',), 256, F32),
    ('d_lr', ('d_lr',), 128, F32),
    ('d_g', ('d_g',), 256, F32),
)
IN_TOTAL = sum(g[2] for g in IN_GROUPS)

MOE_CHUNK = 2048
MOE_ROWS = 160
MOE_STRIDE = MOE_ROWS + SUBLANES
ROW_VREGS = D_MODEL // LANES


def _dot(a, b):
    return jnp.dot(a.astype(BF16), b.astype(BF16), preferred_element_type=F32)


def _dot_nt(a, b):
    return lax.dot_general(a.astype(BF16), b.astype(BF16), (((1,), (1,)), ((), ())),
                           preferred_element_type=F32)


def _dot_tn(a, b):
    return lax.dot_general(a.astype(BF16), b.astype(BF16), (((0,), (0,)), ((), ())),
                           preferred_element_type=F32)


def _split(x):
    hi = x.astype(BF16)
    lo = (x - hi.astype(F32)).astype(BF16)
    return hi, lo


def _dot3(a, b):
    ah, al = _split(a)
    bh, bl = _split(b)
    return (jnp.dot(ah, bh, preferred_element_type=F32) + jnp.dot(al, bh, preferred_element_type=F32)
            + jnp.dot(ah, bl, preferred_element_type=F32))


def _dot_sel(a, m):
    ah, al = _split(a)
    return jnp.dot(ah, m, preferred_element_type=F32) + jnp.dot(al, m, preferred_element_type=F32)


def _silu(x):
    return x * jax.nn.sigmoid(x)


def _layer_norm(r, g, b):
    mu = jnp.mean(r, axis=-1, keepdims=True)
    d = r - mu
    var = jnp.mean(d * d, axis=-1, keepdims=True)
    return d * lax.rsqrt(var + EPS) * g + b


def _params(sem):
    return pltpu.CompilerParams(dimension_semantics=sem, vmem_limit_bytes=VMEM_LIMIT)


ADA_ROWS = 24
ADA_TILE = 1536


def _ada_kernel(cc_ref, w_ref, b_ref, o_ref):
    s = _silu(cc_ref[...])
    o_ref[0] = _dot3(s, w_ref[0]) + b_ref[0]


def _ada_call(cc, w_ada, b_ada):
    depth = w_ada.shape[0]
    n = w_ada.shape[2]
    return pl.pallas_call(
        _ada_kernel,
        grid=(depth, n // ADA_TILE),
        in_specs=[
            pl.BlockSpec((ADA_ROWS, D_MODEL), lambda l, j: (0, 0)),
            pl.BlockSpec((1, D_MODEL, ADA_TILE), lambda l, j: (l, 0, j)),
            pl.BlockSpec((1, 1, ADA_TILE), lambda l, j: (l, 0, j)),
        ],
        out_specs=pl.BlockSpec((1, ADA_ROWS, ADA_TILE), lambda l, j: (l, 0, j)),
        out_shape=jax.ShapeDtypeStruct((depth, ADA_ROWS, n), F32),
        compiler_params=_params(("arbitrary", "arbitrary")),
        name="ada_mod",
    )(cc, w_ada, b_ada.reshape(depth, 1, n))


def _mod_index(nb):
    return lambda b, t: (jnp.where(t == 0, nb, b), 0, 0)


def _in_kernel(x_ref, mod_ref, w_ref, *out_refs):
    x = x_ref[0]
    mod = mod_ref[0]
    h = (x * (1.0 + mod[1:2]) + mod[0:1]).astype(BF16)
    off = 0
    for (name, _, width, dt), o_ref in zip(IN_GROUPS, out_refs):
        o_ref[0] = jnp.dot(h, w_ref[:, off:off + width], preferred_element_type=F32).astype(dt)
        off += width


def _in_call(xa, mods, w_cat):
    nb, nt, _ = xa.shape
    tiles = nt // TOKEN_TILE
    return pl.pallas_call(
        _in_kernel,
        grid=(nb, tiles),
        in_specs=[
            pl.BlockSpec((1, TOKEN_TILE, D_MODEL), lambda b, t: (b, t, 0)),
            pl.BlockSpec((1, SUBLANES, D_MODEL), _mod_index(nb)),
            pl.BlockSpec((D_MODEL, IN_TOTAL), lambda b, t: (0, 0)),
        ],
        out_specs=[pl.BlockSpec((1, TOKEN_TILE, g[2]), lambda b, t: (b, t, 0)) for g in IN_GROUPS],
        out_shape=[jax.ShapeDtypeStruct((nb, nt, g[2]), g[3]) for g in IN_GROUPS],
        compiler_params=_params(("arbitrary", "arbitrary")),
        name="in_proj",
    )(xa, mods, w_cat)


def _head_rms(x, ones_bd, gain):
    ss = _dot_sel(x * x, ones_bd)
    return x * lax.rsqrt(ss * (1.0 / HEAD_DIM) + EPS) * gain


def _rope(x, rot, cos, sin):
    return x * cos + _dot_sel(x, rot) * sin


def _attn_a_kernel(q_ref, kv_ref, cos_ref, sin_ref, qg_ref, kg_ref, bd_ref, rot_ref, o_ref, kp_ref, vt_ref):
    t = pl.program_id(1)
    kvw = A_KV_HEADS * HEAD_DIM
    rep = A_HEADS // A_KV_HEADS
    assert rep * HEAD_DIM == kvw == LANES
    n_tiles = kv_ref.shape[1] // TOKEN_TILE
    scale = HEAD_DIM ** -0.5
    lane = lax.broadcasted_iota(jnp.int32, (TOKEN_TILE, kvw), 1)
    first_half = lane < HEAD_DIM

    @pl.when(t == 0)
    def _prep_keys():
        bd = bd_ref[0:kvw, 0:kvw]
        rot = rot_ref[0:kvw, 0:kvw]
        kg = kg_ref[...]

        def put(i, k_rows, k):
            swapped = pltpu.roll(k, HEAD_DIM, axis=1)
            kp_ref[0, k_rows, :] = jnp.where(first_half, k, swapped).astype(BF16)
            kp_ref[1, k_rows, :] = jnp.where(first_half, swapped, k).astype(BF16)
            vt_ref[i] = kv_ref[0, k_rows, kvw:2 * kvw].astype(F32).T.astype(BF16)

        put(0, pl.ds(0, TOKEN_TILE), _head_rms(kv_ref[0, 0:TOKEN_TILE, 0:kvw].astype(F32), bd, kg))

        def body(i, carry):
            r0 = pl.multiple_of(i * TOKEN_TILE, TOKEN_TILE)
            rows = pl.ds(CTX_LEN + r0, TOKEN_TILE)
            kn = _head_rms(kv_ref[0, rows, 0:kvw].astype(F32), bd, kg)
            put(i + CTX_LEN // TOKEN_TILE, rows,
                _rope(kn, rot, cos_ref[pl.ds(r0, TOKEN_TILE), 0:kvw], sin_ref[pl.ds(r0, TOKEN_TILE), 0:kvw]))
            return carry

        lax.fori_loop(0, n_tiles - CTX_LEN // TOKEN_TILE, body, 0)

    qn = _head_rms(q_ref[0].astype(F32), bd_ref[...], qg_ref[...])

    def attend(qh, key_tiles):
        nk = key_tiles * TOKEN_TILE
        heads = [(g, r) for g in range(A_KV_HEADS) for r in range(rep)]
        qms = [jnp.where(first_half == (r == 0), qh[:, g * kvw:(g + 1) * kvw] * scale, 0.0).astype(BF16)
               for g, r in heads]
        ss = [_dot_nt(kp_ref[g, 0:nk, :], qms[i]) for i, (g, r) in enumerate(heads)]
        es = [jnp.exp(s - jnp.max(s, axis=0, keepdims=True)) for s in ss]
        ls = [jnp.sum(e, axis=0, keepdims=True) for e in es]
        ebs = [e.astype(BF16) for e in es]
        outs = []
        for i, (g, r) in enumerate(heads):
            o = None
            for k in range(key_tiles):
                part = jnp.dot(vt_ref[k, g * HEAD_DIM:(g + 1) * HEAD_DIM, :],
                               ebs[i][k * TOKEN_TILE:(k + 1) * TOKEN_TILE, :], preferred_element_type=F32)
                o = part if o is None else o + part
            outs.append(o / ls[i])
        return jnp.concatenate(outs, axis=0).T

    @pl.when(t == 0)
    def _ctx_queries():
        o_ref[0] = attend(qn, CTX_LEN // TOKEN_TILE).astype(o_ref.dtype)

    @pl.when(t > 0)
    def _latent_queries():
        r0 = pl.multiple_of((t - 1) * TOKEN_TILE, TOKEN_TILE)
        qr = _rope(qn, rot_ref[...], cos_ref[pl.ds(r0, TOKEN_TILE), :], sin_ref[pl.ds(r0, TOKEN_TILE), :])
        o_ref[0] = attend(qr, n_tiles).astype(o_ref.dtype)


def _attn_a_call(a_q, a_kv, cos_t, sin_t, q_gain, k_gain, ones_bd, rot_m):
    nb, nt, _ = a_q.shape
    tiles = nt // TOKEN_TILE
    seq = nt - CTX_LEN
    qw = A_HEADS * HEAD_DIM
    return pl.pallas_call(
        _attn_a_kernel,
        grid=(nb, tiles),
        in_specs=[
            pl.BlockSpec((1, TOKEN_TILE, qw), lambda b, t: (b, t, 0)),
            pl.BlockSpec((1, nt, qw), lambda b, t: (b, 0, 0)),
            pl.BlockSpec((seq, qw), lambda b, t: (0, 0)),
            pl.BlockSpec((seq, qw), lambda b, t: (0, 0)),
            pl.BlockSpec((1, qw), lambda b, t: (0, 0)),
            pl.BlockSpec((1, A_KV_HEADS * HEAD_DIM), lambda b, t: (0, 0)),
            pl.BlockSpec((qw, qw), lambda b, t: (0, 0)),
            pl.BlockSpec((qw, qw), lambda b, t: (0, 0)),
        ],
        out_specs=pl.BlockSpec((1, TOKEN_TILE, qw), lambda b, t: (b, t, 0)),
        out_shape=jax.ShapeDtypeStruct((nb, nt, qw), BF16),
        scratch_shapes=[pltpu.VMEM((A_KV_HEADS, nt, A_KV_HEADS * HEAD_DIM), BF16),
                        pltpu.VMEM((tiles, A_KV_HEADS * HEAD_DIM, TOKEN_TILE), BF16)],
        compiler_params=_params(("arbitrary", "arbitrary")),
        name="mixer_a_gqa",
    )(a_q, a_kv, cos_t, sin_t, q_gain, k_gain, ones_bd, rot_m)


NB_QROWS = TOKEN_TILE // GRID_W
NB_KROWS = 12
NB_INVALID = 2 * WIN_R - 1


def _attn_b_kernel(q_ref, k_ref, v_ref, bt_ref, o_ref):
    t = pl.program_id(1)
    scale = HEAD_DIM ** -0.5
    rows = (k_ref.shape[1] - CTX_LEN) // GRID_W
    wr = min(WIN_R, rows)

    def softmax_pv(parts):
        m = None
        for s, _ in parts:
            mi = jnp.max(s, axis=-1, keepdims=True)
            m = mi if m is None else jnp.maximum(m, mi)
        acc, l = None, None
        for s, v in parts:
            e = jnp.exp(s - m)
            li = jnp.sum(e, axis=-1, keepdims=True)
            oi = jnp.dot(e.astype(BF16), v, preferred_element_type=F32)
            acc = oi if acc is None else acc + oi
            l = li if l is None else l + li
        return acc / l

    @pl.when(t == 0)
    def _ctx_queries():
        outs = []
        for h in range(B_HEADS):
            sl = slice(h * HEAD_DIM, (h + 1) * HEAD_DIM)
            qq = (q_ref[0, :, sl].astype(F32) * scale).astype(BF16)
            s = _dot_nt(qq, k_ref[0, 0:CTX_LEN, sl])
            outs.append(softmax_pv([(s, v_ref[0, 0:CTX_LEN, sl])]))
        o_ref[0] = jnp.concatenate(outs, axis=-1).astype(o_ref.dtype)

    @pl.when(t > 0)
    def _latent_queries():
        r0 = (t - 1) * NB_QROWS
        start = jnp.clip(r0 - wr // 2, 0, rows - NB_KROWS)
        k0 = pl.multiple_of(CTX_LEN + start * GRID_W, GRID_W)
        nk = NB_KROWS * GRID_W
        lane = lax.broadcasted_iota(jnp.int32, (GRID_W, 2 * GRID_W), 1)
        left = lane < GRID_W
        slots = []
        for i in range(NB_QROWS):
            r = r0 + i
            rs = jnp.clip(r - wr // 2, 0, rows - wr)
            row_slots = []
            for j in range(NB_KROWS):
                kr = start + j
                ok = jnp.logical_and(kr >= rs, kr < rs + wr)
                row_slots.append(jnp.where(ok, kr - r + WIN_R - 1, NB_INVALID))
            slots.append(row_slots)
        sls = [slice(h * HEAD_DIM, (h + 1) * HEAD_DIM) for h in range(B_HEADS)]
        qqs = [(q_ref[0, :, sl].astype(F32) * scale).astype(BF16) for sl in sls]
        s_locs = [_dot_nt(qqs[h], k_ref[0, pl.ds(k0, nk), sls[h]]) for h in range(B_HEADS)]
        s_ctxs = [_dot_nt(qqs[h], k_ref[0, 0:CTX_LEN, sls[h]]) for h in range(B_HEADS)]
        probs = []
        for h in range(B_HEADS):
            bias_rows = []
            for i in range(NB_QROWS):
                tiles = []
                for jp in range(NB_KROWS // 2):
                    b0 = bt_ref[h, slots[i][2 * jp]]
                    b1 = bt_ref[h, slots[i][2 * jp + 1]]
                    tiles.append(jnp.where(left, b0, b1))
                bias_rows.append(jnp.concatenate(tiles, axis=-1))
            s_loc = s_locs[h] + jnp.concatenate(bias_rows, axis=0)
            m = jnp.maximum(jnp.max(s_loc, axis=-1, keepdims=True), jnp.max(s_ctxs[h], axis=-1, keepdims=True))
            e_loc = jnp.exp(s_loc - m)
            e_ctx = jnp.exp(s_ctxs[h] - m)
            probs.append((e_loc.astype(BF16), e_ctx.astype(BF16),
                          jnp.sum(e_loc, axis=-1, keepdims=True) + jnp.sum(e_ctx, axis=-1, keepdims=True)))
        outs = [(jnp.dot(probs[h][0], v_ref[0, pl.ds(k0, nk), sls[h]], preferred_element_type=F32)
                 + jnp.dot(probs[h][1], v_ref[0, 0:CTX_LEN, sls[h]], preferred_element_type=F32)) / probs[h][2]
                for h in range(B_HEADS)]
        o_ref[0] = jnp.concatenate(outs, axis=-1).astype(o_ref.dtype)


def _attn_b_call(b_q, b_k, b_v, bias_tab):
    nb, nt, w = b_q.shape
    tiles = nt // TOKEN_TILE
    return pl.pallas_call(
        _attn_b_kernel,
        grid=(nb, tiles),
        in_specs=[
            pl.BlockSpec((1, TOKEN_TILE, w), lambda b, t: (b, t, 0)),
            pl.BlockSpec((1, nt, w), lambda b, t: (b, 0, 0)),
            pl.BlockSpec((1, nt, w), lambda b, t: (b, 0, 0)),
            pl.BlockSpec(bias_tab.shape, lambda b, t: (0, 0, 0, 0)),
        ],
        out_specs=pl.BlockSpec((1, TOKEN_TILE, w), lambda b, t: (b, t, 0)),
        out_shape=jax.ShapeDtypeStruct((nb, nt, w), BF16),
        compiler_params=_params(("arbitrary", "arbitrary")),
        name="mixer_b_neighbourhood",
    )(b_q, b_k, b_v, bias_tab)


def _nb_bias_table(rpb):
    cols = jnp.arange(GRID_W, dtype=jnp.int32)
    col_start = jnp.clip(cols - WIN_C // 2, 0, GRID_W - WIN_C)
    col_ok = (cols[None, :] >= col_start[:, None]) & (cols[None, :] < col_start[:, None] + WIN_C)
    dc_idx = jnp.clip(cols[None, :] - cols[:, None] + WIN_C - 1, 0, 2 * WIN_C - 2)
    tab = rpb.astype(F32)[:, :, dc_idx]
    tab = jnp.where(col_ok[None, None], tab, NEG_BIG)
    tab = jnp.concatenate([tab, jnp.full_like(tab[:, :1], NEG_BIG)], axis=1)
    return jnp.concatenate([tab, tab], axis=-1)


def _order_masks():
    i = lax.broadcasted_iota(jnp.int32, (CHUNK, CHUNK), 0)
    j = lax.broadcasted_iota(jnp.int32, (CHUNK, CHUNK), 1)
    return ((j <= i, j < i), (j >= i, j > i))


def _scan_chunk(step, direction, n_ctx_chunks, n_chunks):
    if direction == 0:
        return step
    return jnp.where(step < n_ctx_chunks, n_ctx_chunks - 1 - step, n_chunks + n_ctx_chunks - 1 - step)


def _sel_dot(m, a):
    ah, al = _split(a)
    return jnp.dot(m, ah, preferred_element_type=F32) + jnp.dot(m, al, preferred_element_type=F32)


def _gated_out_tiles(part_refs, gate_ref, gain_ref, bd_ref, o_ref):
    nt = part_refs[0].shape[0]
    bd = bd_ref[...]
    gain = gain_ref[...]

    def body(i, carry):
        r0 = pl.multiple_of(i * TOKEN_TILE, TOKEN_TILE)
        o = sum(p[pl.ds(r0, TOKEN_TILE), :] for p in part_refs)
        y = _head_rms(o, bd, gain) * _silu(gate_ref[0, pl.ds(r0, TOKEN_TILE), :])
        o_ref[0, pl.ds(r0, TOKEN_TILE), :] = y.astype(o_ref.dtype)
        return carry

    lax.fori_loop(0, nt // TOKEN_TILE, body, 0)


C_BETA_LANE = 0
C_A_LANE = 8
C_T_ROWS = 16


SOLVE_BLOCK = 16
GDN_LOCAL_CHUNKS = 4


def _unit_lower_solve(lmats, rhss, same_block, eye):
    n = range(len(lmats))
    lds = [jnp.where(same_block, l, 0.0) for l in lmats]
    ts = [eye - ld for ld in lds]
    ps = lds
    span = 2
    while span < SOLVE_BLOCK:
        ps = [_dot(p, p) for p in ps]
        ts = [ts[i] + _dot(ts[i], ps[i]) for i in n]
        span *= 2
    width = lmats[0].shape[1]
    mzs = [_dot(ts[i], jnp.concatenate([lmats[i] - lds[i], rhss[i]], axis=-1)) for i in n]
    mmzs = [_dot(mz[:, 0:width], mz) for mz in mzs]
    zs = [mzs[i][:, width:] - mmzs[i][:, width:] for i in n]
    ps = [mmz[:, 0:width] for mmz in mmzs]
    span = 2
    while span < CHUNK // SOLVE_BLOCK:
        zs = [zs[i] + _dot(ps[i], zs[i]) for i in n]
        span *= 2
        if span < CHUNK // SOLVE_BLOCK:
            ps = [_dot(p, p) for p in ps]
    return zs


def _gdn_kernel(qkv_ref, ba_ref, gate_ref, conv_ref, par_ref, gain_ref, bd_ref, o_ref,
                q_s, k_s, v_s, bl_s, g_s, gt_s, a12_s, b2_s, egl_s, o_s, st_s):
    nt = qkv_ref.shape[1]
    n_tiles = nt // TOKEN_TILE
    n_chunks = nt // CHUNK
    n_ctx_chunks = CTX_LEN // CHUNK
    w = C_HEADS * HEAD_DIM
    pack = 2 * SUBLANES
    bd = bd_ref[...]
    lane = lax.broadcasted_iota(jnp.int32, (TOKEN_TILE, LANES), 1)
    lane_c = lax.broadcasted_iota(jnp.int32, (CHUNK, LANES), 1)
    row = lax.broadcasted_iota(jnp.int32, (TOKEN_TILE, 1), 0)
    neg_rate = -jnp.exp(par_ref[0:1, :])
    dt_bias = par_ref[1:2, :]
    w_prev, w_mid, w_next = conv_ref[0:1, :], conv_ref[1:2, :], conv_ref[2:3, :]
    masks = _order_masks()
    mask_bf = [jnp.where(m[0], 1.0, 0.0).astype(BF16) for m in masks]

    def prep(i, carry):
        r0 = pl.multiple_of(i * TOKEN_TILE, TOKEN_TILE)
        x = qkv_ref[0, pl.ds(r0, TOKEN_TILE), :].astype(F32)
        before = qkv_ref[0, pl.ds(pl.multiple_of(jnp.maximum(r0 - pack, 0), pack), pack), :].astype(F32)
        after = qkv_ref[0, pl.ds(pl.multiple_of(jnp.minimum(r0 + TOKEN_TILE, nt - pack), pack), pack), :].astype(F32)
        first_of_seq = jnp.logical_or(i == 0, i == CTX_LEN // TOKEN_TILE)
        last_of_seq = jnp.logical_or(i == CTX_LEN // TOKEN_TILE - 1, i == n_tiles - 1)
        edge_prev = jnp.where(first_of_seq, 0.0, before[pack - 1:pack, :])
        edge_next = jnp.where(last_of_seq, 0.0, after[0:1, :])
        x_prev = jnp.where(row == 0, edge_prev, pltpu.roll(x, 1, axis=0))
        x_next = jnp.where(row == TOKEN_TILE - 1, edge_next, pltpu.roll(x, TOKEN_TILE - 1, axis=0))
        y = _silu(x_prev * w_prev + x * w_mid + x_next * w_next)
        q, k, v = y[:, 0:w], y[:, w:2 * w], y[:, 2 * w:3 * w]
        q_s[pl.ds(r0, TOKEN_TILE), :] = q * lax.rsqrt(_dot_sel(q * q, bd) + EPS) * (HEAD_DIM ** -0.5)
        k_s[pl.ds(r0, TOKEN_TILE), :] = k * lax.rsqrt(_dot_sel(k * k, bd) + EPS)
        v_s[pl.ds(r0, TOKEN_TILE), :] = v
        ba = ba_ref[0, pl.ds(r0, TOKEN_TILE), :]
        sp = ba + dt_bias
        softplus = jnp.maximum(sp, 0.0) + jnp.log1p(jnp.exp(-jnp.abs(sp)))
        bl = jnp.where(lane < C_A_LANE, jax.nn.sigmoid(ba), neg_rate * softplus)
        bl_s[pl.ds(r0, TOKEN_TILE), :] = bl
        for c in range(TOKEN_TILE // CHUNK):
            blc = bl[c * CHUNK:(c + 1) * CHUNK, :]
            g = jnp.where(lane_c < C_A_LANE + C_HEADS, _sel_dot(mask_bf[0], blc), _sel_dot(mask_bf[1], blc))
            g_s[pl.ds(r0 + c * CHUNK, CHUNK), :] = g
            gt_s[i * (TOKEN_TILE // CHUNK) + c] = g.T[0:C_T_ROWS, :]
        return carry

    lax.fori_loop(0, n_tiles, prep, 0)

    ri = lax.broadcasted_iota(jnp.int32, (CHUNK, CHUNK), 0)
    ci = lax.broadcasted_iota(jnp.int32, (CHUNK, CHUNK), 1)
    same_block = (ri // SOLVE_BLOCK) == (ci // SOLVE_BLOCK)
    eye = jnp.where(ri == ci, 1.0, 0.0)

    src = lax.broadcasted_iota(jnp.int32, (LANES, w), 0)
    dst_head = lax.broadcasted_iota(jnp.int32, (LANES, w), 1) // HEAD_DIM
    spread_g = [jnp.where(src == C_A_LANE + d * C_HEADS + dst_head, 1.0, 0.0).astype(BF16) for d in range(2)]
    spread_b = [jnp.where(src == C_BETA_LANE + d * C_HEADS + dst_head, 1.0, 0.0).astype(BF16) for d in range(2)]
    heads = [slice(h * HEAD_DIM, (h + 1) * HEAD_DIM) for h in range(C_HEADS)]

    def local(it, carry):
        chunks = [it * GDN_LOCAL_CHUNKS + t for t in range(GDN_LOCAL_CHUNKS)]
        rows = [pl.ds(pl.multiple_of(c * CHUNK, CHUNK), CHUNK) for c in chunks]
        td = [(t, d) for t in range(GDN_LOCAL_CHUNKS) for d in range(2)]
        qcs, kcs, vcs = [q_s[r, :] for r in rows], [k_s[r, :] for r in rows], [v_s[r, :] for r in rows]
        g_alls = [_dot_sel(g_s[rows[t], :], spread_g[d]) for t, d in td]
        b_alls = [_dot_sel(bl_s[rows[t], :], spread_b[d]) for t, d in td]
        qks = [[_dot_nt(qcs[t][:, sl], kcs[t][:, sl]) for sl in heads] for t in range(GDN_LOCAL_CHUNKS)]
        pre = []
        for i, (t, d) in enumerate(td):
            last = CHUNK - 1 if d == 0 else 0
            e_g = jnp.exp(g_alls[i])
            g_last = g_alls[i][last:last + 1, :]
            kb = kcs[t] * b_alls[i]
            kd_t = (kcs[t] * jnp.exp(g_last - g_alls[i])).T.astype(BF16)
            egl_s[d, chunks[t]] = jnp.broadcast_to(jnp.exp(g_last), (SUBLANES, w))
            pre.append((kb, kb * e_g, vcs[t] * b_alls[i], qcs[t] * e_g, kd_t))
        kbk = [[_dot_nt(pre[i][0][:, sl], kcs[t][:, sl]) for sl in heads] for i, (t, d) in enumerate(td)]
        lmats, rhss, keep = [], [], []
        for i, (t, d) in enumerate(td):
            incl, strict = masks[d]
            g_rows = gt_s[chunks[t]]
            for h, sl in enumerate(heads):
                la = C_A_LANE + d * C_HEADS + h
                decay = jnp.exp(jnp.where(incl, g_alls[i][:, sl] - g_rows[la:la + 1, :], NEG_BIG))
                lmats.append(jnp.where(strict, kbk[i][h] * decay, 0.0))
                rhss.append(jnp.concatenate([pre[i][1][:, sl], pre[i][2][:, sl]], axis=-1))
                keep.append(((qks[t][h] * decay).astype(BF16), pre[i][3][:, sl], pre[i][4][sl, :]))
        sols = [s.astype(BF16) for s in _unit_lower_solve(lmats, rhss, same_block, eye)]
        qwu = [jnp.dot(keep[i][0], sols[i], preferred_element_type=F32) for i in range(len(sols))]
        kwu = [jnp.dot(keep[i][2], sols[i], preferred_element_type=F32) for i in range(len(sols))]
        i = 0
        for t in range(GDN_LOCAL_CHUNKS):
            c = it * GDN_LOCAL_CHUNKS + t
            r0 = pl.multiple_of(c * CHUNK, CHUNK)
            o_const = None
            for d in range(2):
                u = range(i, i + C_HEADS)
                a12_s[d, c, 0:CHUNK, :] = jnp.concatenate(
                    [keep[j][1] - qwu[j][:, 0:HEAD_DIM] for j in u], axis=-1).astype(BF16)
                a12_s[d, c, CHUNK:2 * CHUNK, :] = jnp.concatenate(
                    [-kwu[j][:, 0:HEAD_DIM] for j in u], axis=-1).astype(BF16)
                b2_s[d, c] = jnp.concatenate([kwu[j][:, HEAD_DIM:] for j in u], axis=-1)
                part = jnp.concatenate([qwu[j][:, HEAD_DIM:] for j in u], axis=-1)
                o_const = part if o_const is None else o_const + part
                i += C_HEADS
            o_s[pl.ds(r0, CHUNK), :] = o_const
        return carry

    lax.fori_loop(0, n_chunks // GDN_LOCAL_CHUNKS, local, 0)

    st_s[...] = jnp.zeros(st_s.shape, F32)

    def scan(step, carry):
        units = [(d, h) for d in range(2) for h in range(C_HEADS)]
        cs = [_scan_chunk(step, d, n_ctx_chunks, n_chunks) for d in range(2)]
        a12 = [a12_s[d, cs[d]] for d in range(2)]
        b2 = [b2_s[d, cs[d]] for d in range(2)]
        egl = [egl_s[d, cs[d]] for d in range(2)]
        ss = [st_s[d, h] for d, h in units]
        rs = [jnp.dot(a12[d][:, heads[h]], ss[i].astype(BF16), preferred_element_type=F32)
              for i, (d, h) in enumerate(units)]
        for i, (d, h) in enumerate(units):
            st_s[d, h] = ss[i] * egl[d][0:1, heads[h]] + rs[i][CHUNK:, :] + b2[d][:, heads[h]]
        for d in range(2):
            r0 = pl.multiple_of(cs[d] * CHUNK, CHUNK)
            o = jnp.concatenate([rs[d * C_HEADS + h][0:CHUNK, :] for h in range(C_HEADS)], axis=-1)
            o_s[pl.ds(r0, CHUNK), :] = o_s[pl.ds(r0, CHUNK), :] + o
        return carry

    lax.fori_loop(0, n_chunks, scan, 0)
    _gated_out_tiles((o_s,), gate_ref, gain_ref, bd_ref, o_ref)


def _gdn_call(c_qkv, c_ba, c_g, conv_w, par, gain, ones_bd):
    nb, nt, _ = c_qkv.shape
    w = C_HEADS * HEAD_DIM
    nc = nt // CHUNK
    return pl.pallas_call(
        _gdn_kernel,
        grid=(nb,),
        in_specs=[
            pl.BlockSpec((1, nt, 3 * w), lambda b: (b, 0, 0)),
            pl.BlockSpec((1, nt, LANES), lambda b: (b, 0, 0)),
            pl.BlockSpec((1, nt, w), lambda b: (b, 0, 0)),
            pl.BlockSpec((SUBLANES, 3 * w), lambda b: (0, 0)),
            pl.BlockSpec((SUBLANES, LANES), lambda b: (0, 0)),
            pl.BlockSpec((1, w), lambda b: (0, 0)),
            pl.BlockSpec((w, w), lambda b: (0, 0)),
        ],
        out_specs=pl.BlockSpec((1, nt, w), lambda b: (b, 0, 0)),
        out_shape=jax.ShapeDtypeStruct((nb, nt, w), BF16),
        scratch_shapes=[
            pltpu.VMEM((nt, w), F32), pltpu.VMEM((nt, w), F32), pltpu.VMEM((nt, w), F32),
            pltpu.VMEM((nt, LANES), F32), pltpu.VMEM((nt, LANES), F32),
            pltpu.VMEM((nc, C_T_ROWS, CHUNK), F32),
            pltpu.VMEM((2, nc, 2 * CHUNK, w), BF16),
            pltpu.VMEM((2, nc, CHUNK, w), F32),
            pltpu.VMEM((2, nc, SUBLANES, w), F32),
            pltpu.VMEM((nt, w), F32),
            pltpu.VMEM((2, C_HEADS, HEAD_DIM, HEAD_DIM), F32),
        ],
        compiler_params=_params(("arbitrary",)),
        name="mixer_c_gated_delta",
    )(c_qkv, c_ba, c_g, conv_w, par, gain, ones_bd)


GLA_EXP_CAP = 80.0
GLA_LOCAL_CHUNKS = 2
GLA_SCAN_STEPS = 4


def _gla_kernel(qk_ref, v_ref, lr_ref, gate_ref, gw_ref, gb_ref, gain_ref, bd_ref, o_ref,
                la_s, qg_s, el_s, ds_s, o_s):
    nt = qk_ref.shape[1]
    n_tiles = nt // TOKEN_TILE
    n_chunks = nt // CHUNK
    n_ctx_chunks = CTX_LEN // CHUNK
    kw = D_HEADS * D_KDIM
    gw = gw_ref[...]
    gb = gb_ref[...]

    def prep(i, carry):
        r0 = pl.multiple_of(i * TOKEN_TILE, TOKEN_TILE)
        z = _dot3(lr_ref[0, pl.ds(r0, TOKEN_TILE), :], gw) + gb
        log_sig = jnp.minimum(z, 0.0) - jnp.log1p(jnp.exp(-jnp.abs(z)))
        la_s[pl.ds(r0, TOKEN_TILE), :] = log_sig * (1.0 / GLA_TAU)
        return carry

    lax.fori_loop(0, n_tiles, prep, 0)

    masks = _order_masks()
    mask_bf = [jnp.where(m[0], 1.0, 0.0).astype(BF16) for m in masks]
    vw = D_HEADS * HEAD_DIM
    k_head = lax.broadcasted_iota(jnp.int32, (CHUNK, kw), 1) // D_KDIM
    v_head = lax.broadcasted_iota(jnp.int32, (CHUNK, vw), 1) // HEAD_DIM
    state_diag = (lax.broadcasted_iota(jnp.int32, (vw, kw), 0) // HEAD_DIM
                  == lax.broadcasted_iota(jnp.int32, (vw, kw), 1) // D_KDIM)

    def local(it, carry):
        units, pre = [], {}
        for t in range(GLA_LOCAL_CHUNKS):
            c = it * GLA_LOCAL_CHUNKS + t
            r0 = pl.multiple_of(c * CHUNK, CHUNK)
            qk = qk_ref[0, pl.ds(r0, CHUNK), :]
            q = qk[:, 0:kw] * (D_KDIM ** -0.5)
            k = qk[:, kw:2 * kw]
            v = v_ref[0, pl.ds(r0, CHUNK), :]
            v_b = v.astype(BF16)
            v_t = v.T.astype(BF16)
            for d in range(2):
                last = CHUNK - 1 if d == 0 else 0
                g = _sel_dot(mask_bf[d], la_s[pl.ds(r0, CHUNK), d * kw:(d + 1) * kw])
                g_mid = g[CHUNK // 2:CHUNK // 2 + 1, :]
                g_last = g[last:last + 1, :]
                q_t = q * jnp.exp(jnp.minimum(g - g_mid, GLA_EXP_CAP))
                pre[(t, d)] = ([jnp.where(k_head == h, q_t, 0.0).astype(BF16) for h in range(D_HEADS)],
                               (k * jnp.exp(jnp.minimum(g_mid - g, GLA_EXP_CAP))).astype(BF16),
                               (k * jnp.exp(g_last - g)).astype(BF16), v_b, v_t)
                qg_s[d, c] = (q * jnp.exp(g)).astype(BF16)
                el_s[d, c] = jnp.broadcast_to(jnp.exp(g_last), (SUBLANES, kw))
                units += [(t, d, h) for h in range(D_HEADS)]
        a_ = [jnp.where(masks[d][0], _dot_nt(pre[(t, d)][0][h], pre[(t, d)][1]), 0.0).astype(BF16)
              for t, d, h in units]
        ds_ = {td: jnp.dot(p[4], p[2], preferred_element_type=F32) for td, p in pre.items()}
        av = [jnp.dot(a_[i], pre[(t, d)][3], preferred_element_type=F32) for i, (t, d, h) in enumerate(units)]
        i = 0
        for t in range(GLA_LOCAL_CHUNKS):
            c = it * GLA_LOCAL_CHUNKS + t
            r0 = pl.multiple_of(c * CHUNK, CHUNK)
            o_const = jnp.zeros((CHUNK, vw), F32)
            for d in range(2):
                ds_s[d, c] = jnp.where(state_diag, ds_[(t, d)], 0.0)
                for h in range(D_HEADS):
                    o_const = o_const + jnp.where(v_head == h, av[i], 0.0)
                    i += 1
            o_s[pl.ds(r0, CHUNK), :] = o_const
        return carry

    lax.fori_loop(0, n_chunks // GLA_LOCAL_CHUNKS, local, 0)

    def scan(it, states):
        states = list(states)
        jobs = []
        for t in range(GLA_SCAN_STEPS):
            step = it * GLA_SCAN_STEPS + t
            for d in range(2):
                c = _scan_chunk(step, d, n_ctx_chunks, n_chunks)
                jobs.append((c, qg_s[d, c], states[d].astype(BF16)))
                states[d] = states[d] * el_s[d, c][0:1, :] + ds_s[d, c]
        outs = [_dot_nt(qg, sb) for _, qg, sb in jobs]
        for (c, _, _), o in zip(jobs, outs):
            r0 = pl.multiple_of(c * CHUNK, CHUNK)
            o_s[pl.ds(r0, CHUNK), :] = o_s[pl.ds(r0, CHUNK), :] + o
        return tuple(states)

    zero = jnp.zeros((vw, kw), F32)
    lax.fori_loop(0, n_chunks // GLA_SCAN_STEPS, scan, (zero, zero))
    _gated_out_tiles((o_s,), gate_ref, gain_ref, bd_ref, o_ref)


def _gla_call(d_qk, d_v, d_lr, d_g, gw_blk, gb_row, gain, ones_bd):
    nb, nt, _ = d_qk.shape
    vw = D_HEADS * HEAD_DIM
    kw2 = 2 * D_HEADS * D_KDIM
    return pl.pallas_call(
        _gla_kernel,
        grid=(nb,),
        in_specs=[
            pl.BlockSpec((1, nt, kw2), lambda b: (b, 0, 0)),
            pl.BlockSpec((1, nt, vw), lambda b: (b, 0, 0)),
            pl.BlockSpec((1, nt, LANES), lambda b: (b, 0, 0)),
            pl.BlockSpec((1, nt, vw), lambda b: (b, 0, 0)),
            pl.BlockSpec((LANES, kw2), lambda b: (0, 0)),
            pl.BlockSpec((1, kw2), lambda b: (0, 0)),
            pl.BlockSpec((1, vw), lambda b: (0, 0)),
            pl.BlockSpec((vw, vw), lambda b: (0, 0)),
        ],
        out_specs=pl.BlockSpec((1, nt, vw), lambda b: (b, 0, 0)),
        out_shape=jax.ShapeDtypeStruct((nb, nt, vw), BF16),
        scratch_shapes=[
            pltpu.VMEM((nt, kw2), F32),
            pltpu.VMEM((2, nt // CHUNK, CHUNK, kw2 // 2), BF16),
            pltpu.VMEM((2, nt // CHUNK, SUBLANES, kw2 // 2), F32),
            pltpu.VMEM((2, nt // CHUNK, vw, kw2 // 2), F32),
            pltpu.VMEM((nt, vw), F32),
        ],
        compiler_params=_params(("arbitrary",)),
        name="mixer_d_gla",
    )(d_qk, d_v, d_lr, d_g, gw_blk, gb_row, gain, ones_bd)


ROUTE_E1, ROUTE_E2, ROUTE_W1, ROUTE_W2 = 0, 1, 2, 3
ROUTER_EXPERT_LANE = N_GROUPS


def _route(logits):
    lane = lax.broadcasted_iota(jnp.int32, logits.shape, 1).astype(F32)
    far = float(LANES)
    in_grp = lane < N_GROUPS
    lg = jnp.where(in_grp, logits, NEG_BIG)
    mg = jnp.max(lg, axis=-1, keepdims=True)
    grp = jnp.min(jnp.where(lg == mg, lane, far), axis=-1, keepdims=True)
    p_grp = 1.0 / jnp.sum(jnp.where(in_grp, jnp.exp(lg - mg), 0.0), axis=-1, keepdims=True)
    lo = ROUTER_EXPERT_LANE + EXP_PER_GROUP * grp
    in_exp = jnp.logical_and(lane >= lo, lane < lo + EXP_PER_GROUP)
    le = jnp.where(in_exp, logits, NEG_BIG)
    m1 = jnp.max(le, axis=-1, keepdims=True)
    i1 = jnp.min(jnp.where(le == m1, lane, far), axis=-1, keepdims=True)
    le2 = jnp.where(lane == i1, NEG_BIG, le)
    m2 = jnp.max(le2, axis=-1, keepdims=True)
    i2 = jnp.min(jnp.where(le2 == m2, lane, far), axis=-1, keepdims=True)
    e2 = jnp.exp(m2 - m1)
    w1 = p_grp / (1.0 + e2)
    w2 = p_grp * e2 / (1.0 + e2)
    out = jnp.where(lane == ROUTE_E1, i1 - ROUTER_EXPERT_LANE, 0.0)
    out = jnp.where(lane == ROUTE_E2, i2 - ROUTER_EXPERT_LANE, out)
    out = jnp.where(lane == ROUTE_W1, w1, out)
    return jnp.where(lane == ROUTE_W2, w2, out)


def _merge_kernel(x_ref, mod_ref, oa_ref, ob_ref, oc_ref, od_ref, wg_ref, wbr_ref, wo_ref,
                  lng_ref, lnb_ref, wr_ref, br_ref, x1_ref, h2_ref, route_ref):
    x = x_ref[0]
    mod = mod_ref[0]
    h = (x * (1.0 + mod[1:2]) + mod[0:1]).astype(BF16)
    m = None
    for z, o_ref in enumerate((oa_ref, ob_ref, oc_ref, od_ref)):
        gate = jax.nn.sigmoid(jnp.dot(h, wg_ref[:, z * D_MODEL:(z + 1) * D_MODEL], preferred_element_type=F32))
        up = jnp.dot(o_ref[0], wbr_ref[z], preferred_element_type=F32)
        m = gate * up if m is None else m + gate * up
    y = jnp.dot(m.astype(BF16), wo_ref[...], preferred_element_type=F32)
    x1 = _layer_norm(DN_ALPHA * x + mod[2:3] * y, lng_ref[...], lnb_ref[...])
    x1_ref[0] = x1
    h2 = x1 * (1.0 + mod[4:5]) + mod[3:4]
    for j in range(ROW_VREGS):
        h2_ref[pl.ds(j, TOKEN_TILE, stride=ROW_VREGS), :] = h2[:, j * LANES:(j + 1) * LANES]
    route_ref[0] = _route(_dot3(h2, wr_ref[...]) + br_ref[...]).T[0:SUBLANES, :]


def _merge_call(xa, mods, oa, ob, oc, od, wg, wbr, wo, ln_g, ln_b, wr, br):
    nb, nt, d = xa.shape
    tiles = nt // TOKEN_TILE
    bw = oa.shape[-1]
    tok = lambda b, t: (b, t, 0)
    const2 = lambda b, t: (0, 0)
    return pl.pallas_call(
        _merge_kernel,
        grid=(nb, tiles),
        in_specs=[
            pl.BlockSpec((1, TOKEN_TILE, d), tok),
            pl.BlockSpec((1, SUBLANES, d), _mod_index(nb)),
            pl.BlockSpec((1, TOKEN_TILE, bw), tok), pl.BlockSpec((1, TOKEN_TILE, bw), tok),
            pl.BlockSpec((1, TOKEN_TILE, bw), tok), pl.BlockSpec((1, TOKEN_TILE, bw), tok),
            pl.BlockSpec((d, N_BRANCH * d), const2),
            pl.BlockSpec((N_BRANCH, bw, d), lambda b, t: (0, 0, 0)),
            pl.BlockSpec((d, d), const2),
            pl.BlockSpec((1, d), const2), pl.BlockSpec((1, d), const2),
            pl.BlockSpec((d, LANES), const2), pl.BlockSpec((1, LANES), const2),
        ],
        out_specs=[pl.BlockSpec((1, TOKEN_TILE, d), tok),
                   pl.BlockSpec((TOKEN_TILE * ROW_VREGS, LANES), lambda b, t: (b * tiles + t, 0)),
                   pl.BlockSpec((1, SUBLANES, TOKEN_TILE), lambda b, t: (b * tiles + t, 0, 0))],
        out_shape=[jax.ShapeDtypeStruct((nb, nt, d), F32),
                   jax.ShapeDtypeStruct((nb * nt * ROW_VREGS, LANES), F32),
                   jax.ShapeDtypeStruct((nb * tiles, SUBLANES, TOKEN_TILE), F32)],
        compiler_params=_params(("arbitrary", "arbitrary")),
        name="merge_out_ln1_router",
    )(xa, mods, oa, ob, oc, od, wg, wbr, wo, ln_g, ln_b, wr, br)


MOE_UNROLL = 4


def _moe_chunk(total):
    return max(c for c in range(TOKEN_TILE, MOE_CHUNK + 1, TOKEN_TILE) if total % c == 0)


MOE_GROUP = 2


def _moe_kernel(cnt_ref, off_ref, off_again_ref, idx_ref, wt_ref, x_ref, *refs):
    w_refs = [refs[3 * g:3 * g + 3] for g in range(MOE_GROUP)]
    y_ref = refs[3 * MOE_GROUP]
    xt_s = refs[3 * MOE_GROUP + 1:3 * MOE_GROUP + 1 + MOE_GROUP]
    ot_s = refs[3 * MOE_GROUP + 1 + MOE_GROUP:]
    c = pl.program_id(0)
    j = pl.program_id(1)
    chunk = x_ref.shape[1] // ROW_VREGS
    plan_rows = 2 * chunk

    @pl.when(j == 0)
    def _zero():
        y_ref[...] = jnp.zeros(y_ref.shape, F32)

    n_rows = [cnt_ref[c * N_EXPERTS + j * MOE_GROUP + g] for g in range(MOE_GROUP)]
    off = [off_ref[c * N_EXPERTS + j * MOE_GROUP + g] for g in range(MOE_GROUP)]
    off_again = [off_again_ref[c * N_EXPERTS + j * MOE_GROUP + g] for g in range(MOE_GROUP)]
    n_tiles = functools.reduce(jnp.maximum, [(n + MOE_ROWS - 1) // MOE_ROWS for n in n_rows])

    def slab(tok):
        return pl.ds(pl.multiple_of(tok * ROW_VREGS, ROW_VREGS), ROW_VREGS)

    def gather(base, xt):
        for mi in range(MOE_ROWS):
            xt[pl.ds(mi, ROW_VREGS, stride=MOE_STRIDE), :] = x_ref[0, slab(idx_ref[0, 0, base + mi]), :]

    def expert(xt, ot, wg_ref, wu_ref, wd_ref):
        x = jnp.concatenate([xt[k * MOE_STRIDE:k * MOE_STRIDE + MOE_ROWS, :] for k in range(ROW_VREGS)],
                            axis=-1).astype(BF16)
        a = _silu(jnp.dot(x, wg_ref[0], preferred_element_type=F32)) * jnp.dot(x, wu_ref[0], preferred_element_type=F32)
        out = jnp.dot(a.astype(BF16), wd_ref[0], preferred_element_type=F32)
        for k in range(ROW_VREGS):
            ot[k * MOE_STRIDE:k * MOE_STRIDE + MOE_ROWS, :] = out[:, k * LANES:(k + 1) * LANES]

    def scatter(base, valid, ot):
        for m0 in range(0, MOE_ROWS, MOE_UNROLL):
            pending = []
            for mi in range(m0, m0 + MOE_UNROLL):
                ok = mi < valid
                rows = slab(jnp.where(ok, idx_ref[0, 0, base + mi], chunk))
                wgt = jnp.where(ok, wt_ref[0, 0, base + mi], 0.0)
                upd = y_ref[0, rows, :] + wgt * ot[pl.ds(mi, ROW_VREGS, stride=MOE_STRIDE), :]
                pending.append((rows, upd))
            for rows, upd in pending:
                y_ref[0, rows, :] = upd

    def tile_body(i, carry):
        base = [jnp.minimum(off[g] + i * MOE_ROWS, plan_rows) for g in range(MOE_GROUP)]
        base_s = [jnp.minimum(off_again[g] + i * MOE_ROWS, plan_rows) for g in range(MOE_GROUP)]
        valid = [n_rows[g] - i * MOE_ROWS for g in range(MOE_GROUP)]
        for g in range(MOE_GROUP):
            gather(base[g], xt_s[g])
        for g in range(MOE_GROUP):
            expert(xt_s[g], ot_s[g], *w_refs[g])
        for g in range(MOE_GROUP):
            scatter(base_s[g], valid[g], ot_s[g])
        return carry

    lax.fori_loop(0, n_tiles, tile_body, 0)


def _moe_call(h2_slab, cnt, off, idx_sorted, w_sorted, wg, wu, wd):
    nch, rows_in, _ = h2_slab.shape
    plan = idx_sorted.shape[2]
    rows_out = rows_in + SUBLANES * ROW_VREGS
    w_specs, w_args = [], []
    for g in range(MOE_GROUP):
        pick = lambda c, j, *_, g=g: (j * MOE_GROUP + g, 0, 0)
        w_specs += [pl.BlockSpec((1, D_MODEL, EXP_HIDDEN), pick), pl.BlockSpec((1, D_MODEL, EXP_HIDDEN), pick),
                    pl.BlockSpec((1, EXP_HIDDEN, D_MODEL), pick)]
        w_args += [wg, wu, wd]
    tile_buf = pltpu.VMEM((ROW_VREGS * MOE_STRIDE, LANES), F32)
    grid_spec = pltpu.PrefetchScalarGridSpec(
        num_scalar_prefetch=3,
        grid=(nch, N_EXPERTS // MOE_GROUP),
        in_specs=[
            pl.BlockSpec((1, 1, plan), lambda c, j, *_: (c, 0, 0), memory_space=pltpu.SMEM),
            pl.BlockSpec((1, 1, plan), lambda c, j, *_: (c, 0, 0), memory_space=pltpu.SMEM),
            pl.BlockSpec((1, rows_in, LANES), lambda c, j, *_: (c, 0, 0)),
        ] + w_specs,
        out_specs=pl.BlockSpec((1, rows_out, LANES), lambda c, j, *_: (c, 0, 0)),
        scratch_shapes=[tile_buf] * (2 * MOE_GROUP),
    )
    return pl.pallas_call(
        _moe_kernel,
        grid_spec=grid_spec,
        out_shape=jax.ShapeDtypeStruct((nch, rows_out, LANES), F32),
        compiler_params=_params(("arbitrary", "arbitrary")),
        name="moe_experts",
    )(cnt, off, off, idx_sorted, w_sorted, h2_slab, *w_args)


def _moe_plan(route, chunk):
    nch = route.shape[0] * TOKEN_TILE // chunk
    per = chunk * 2
    eid = route[:, ROUTE_E1:ROUTE_E2 + 1, :].astype(jnp.int32).reshape(nch, per)
    wts = route[:, ROUTE_W1:ROUTE_W2 + 1, :].reshape(nch, per)
    order = jnp.argsort(eid, axis=1, stable=True).astype(jnp.int32)
    token = (order // (2 * TOKEN_TILE)) * TOKEN_TILE + order % TOKEN_TILE
    idx_sorted = jnp.pad(token, ((0, 0), (0, MOE_ROWS)))
    w_sorted = jnp.pad(jnp.take_along_axis(wts, order, axis=1), ((0, 0), (0, MOE_ROWS)))
    cnt = jnp.sum((eid[..., None] == jnp.arange(N_EXPERTS, dtype=jnp.int32)).astype(jnp.int32), axis=1)
    off = jnp.cumsum(cnt, axis=1) - cnt
    return (cnt.reshape(-1).astype(jnp.int32), off.reshape(-1).astype(jnp.int32),
            idx_sorted.reshape(nch, 1, per + MOE_ROWS), w_sorted.reshape(nch, 1, per + MOE_ROWS))


def _ln2_kernel(x_ref, y_ref, mod_ref, g_ref, b_ref, o_ref):
    mod = mod_ref[0]
    y = jnp.concatenate([y_ref[0, pl.ds(j, TOKEN_TILE, stride=ROW_VREGS), :] for j in range(ROW_VREGS)], axis=-1)
    o_ref[0] = _layer_norm(DN_ALPHA * x_ref[0] + mod[5:6] * y, g_ref[...], b_ref[...])


def _ln2_call(x1, y_slab, mods, ln_g, ln_b, chunk, latent_only):
    nb, nt, d = x1.shape
    tiles = nt // TOKEN_TILE
    per_chunk = chunk // TOKEN_TILE
    ctx_tiles = CTX_LEN // TOKEN_TILE
    tok = lambda b, t: (b, t, 0)
    slab = lambda b, t: ((b * tiles + t) // per_chunk, (b * tiles + t) % per_chunk, 0)
    if latent_only:
        out_rows, out_map = nt - CTX_LEN, lambda b, t: (b, jnp.maximum(t - ctx_tiles, 0), 0)
    else:
        out_rows, out_map = nt, tok
    return pl.pallas_call(
        _ln2_kernel,
        grid=(nb, tiles),
        in_specs=[pl.BlockSpec((1, TOKEN_TILE, d), tok),
                  pl.BlockSpec((1, TOKEN_TILE * ROW_VREGS, LANES), slab),
                  pl.BlockSpec((1, SUBLANES, d), _mod_index(nb)),
                  pl.BlockSpec((1, d), lambda b, t: (0, 0)), pl.BlockSpec((1, d), lambda b, t: (0, 0))],
        out_specs=pl.BlockSpec((1, TOKEN_TILE, d), out_map),
        out_shape=jax.ShapeDtypeStruct((nb, out_rows, d), F32),
        compiler_params=_params(("arbitrary", "arbitrary")),
        name="moe_residual_ln2",
    )(x1, y_slab, mods, ln_g, ln_b)


def _head_constants():
    w = A_HEADS * HEAD_DIM
    i = np.arange(w)
    ones_bd = (i[:, None] // HEAD_DIM == i[None, :] // HEAD_DIM).astype(np.float32)
    quarter = HEAD_DIM // 4
    rot = np.zeros((w, w), np.float32)
    first = (i % (2 * quarter)) < quarter
    rot[i[first] + quarter, i[first]] = -1.0
    rot[i[~first] - quarter, i[~first]] = 1.0
    return jnp.asarray(ones_bd, BF16), jnp.asarray(rot, BF16)


def _rope_tables(seq):
    t = jnp.arange(seq, dtype=jnp.int32)
    row = (t // GRID_W).astype(F32)
    col = (t % GRID_W).astype(F32)
    nf = HEAD_DIM // 4
    inv = ROPE_THETA ** (-jnp.arange(nf, dtype=F32) / nf)
    ang_r = row[:, None] * inv
    ang_c = col[:, None] * inv
    cos = jnp.concatenate([jnp.cos(ang_r), jnp.cos(ang_r), jnp.cos(ang_c), jnp.cos(ang_c)], axis=-1)
    sin = jnp.concatenate([jnp.sin(ang_r), jnp.sin(ang_r), jnp.sin(ang_c), jnp.sin(ang_c)], axis=-1)
    return jnp.tile(cos, (1, A_HEADS)), jnp.tile(sin, (1, A_HEADS))


def _in_weight(w_in):
    cols = []
    for _, parts, width, _ in IN_GROUPS:
        got = 0
        for p in parts:
            o, n = _IN_OFFS[p]
            cols.append(w_in[..., o:o + n])
            got += n
        if got < width:
            cols.append(jnp.zeros(w_in.shape[:-1] + (width - got,), w_in.dtype))
    return jnp.concatenate(cols, axis=-1).astype(BF16)


def _lane_row(vec, width, offset=0):
    return jnp.zeros((1, width), F32).at[0, offset:offset + vec.shape[0]].set(vec.astype(F32))


def kernel(x, c, ctx, c_ctx, w_ada, b_ada, w_in, a_q_gain, a_k_gain, b_rpb, c_conv, c_a_log, c_dt_bias, c_out_gain, d_gate_w, d_gate_b, d_out_gain, w_branch, w_out, ln1_g, ln1_b, ln2_g, ln2_b, w_router_g, b_router_g, w_router_e, b_router_e, w_up, w_gate, w_down):
    nb, seq, d = x.shape
    depth = w_ada.shape[0]
    nt = CTX_LEN + seq
    assert d == D_MODEL and ctx.shape[1] == CTX_LEN and nb + 1 <= ADA_ROWS
    assert seq % TOKEN_TILE == 0
    chunk = _moe_chunk(nb * nt)

    xa = jnp.concatenate([ctx, x], axis=1)
    cc = jnp.zeros((ADA_ROWS, d), F32).at[:nb].set(c).at[nb].set(c_ctx)
    mods = _ada_call(cc, w_ada, b_ada).reshape(depth, ADA_ROWS, 6, d)[:, :nb + 1]
    mods = jnp.pad(mods, ((0, 0), (0, 0), (0, SUBLANES - 6), (0, 0)))

    ones_bd, rot_m = _head_constants()
    cos_t, sin_t = _rope_tables(seq)
    gates_off = _IN_OFFS['gates'][0]

    w_mix = _in_weight(w_in)
    w_gates = w_in[..., gates_off:].astype(BF16)
    w_br, w_o = w_branch.astype(BF16), w_out.astype(BF16)
    w_eg, w_eu, w_ed = w_gate.astype(BF16), w_up.astype(BF16), w_down.astype(BF16)

    for l in range(depth):
        proj = dict(zip([g[0] for g in IN_GROUPS], _in_call(xa, mods[l], w_mix[l])))

        oa = _attn_a_call(proj['a_q'], proj['a_kv'], cos_t, sin_t,
                          jnp.tile(a_q_gain[l], A_HEADS)[None, :], jnp.tile(a_k_gain[l], A_KV_HEADS)[None, :],
                          ones_bd, rot_m)
        ob = _attn_b_call(proj['b_q'], proj['b_k'], proj['b_v'], _nb_bias_table(b_rpb[l]))
        conv_w = jnp.pad(c_conv[l], ((0, SUBLANES - c_conv.shape[1]), (0, 0)))
        par = jnp.concatenate([_lane_row(c_a_log[l].reshape(-1), LANES, C_A_LANE),
                               _lane_row(c_dt_bias[l].reshape(-1), LANES, C_A_LANE),
                               jnp.zeros((SUBLANES - 2, LANES), F32)], axis=0)
        oc = _gdn_call(proj['c_qkv'], proj['c_ba'], proj['c_g'], conv_w, par,
                       jnp.tile(c_out_gain[l], C_HEADS)[None, :], ones_bd)
        kw = D_HEADS * D_KDIM
        gw_blk = jnp.zeros((LANES, 2 * kw), F32)
        gw_blk = gw_blk.at[0:D_GATE_RANK, 0:kw].set(d_gate_w[l, 0])
        gw_blk = gw_blk.at[D_GATE_RANK:2 * D_GATE_RANK, kw:2 * kw].set(d_gate_w[l, 1])
        od = _gla_call(proj['d_qk'], proj['d_v'], proj['d_lr'], proj['d_g'], gw_blk,
                       d_gate_b[l].reshape(1, 2 * kw), jnp.tile(d_out_gain[l], D_HEADS)[None, :], ones_bd)

        wr = jnp.concatenate([w_router_g[l], jnp.transpose(w_router_e[l], (1, 0, 2)).reshape(d, N_EXPERTS)], axis=1)
        wr = jnp.pad(wr, ((0, 0), (0, LANES - wr.shape[1])))
        br = _lane_row(jnp.concatenate([b_router_g[l], b_router_e[l].reshape(-1)]), LANES)
        x1, h2, route = _merge_call(
            xa, mods[l], oa, ob, oc, od, w_gates[l], w_br[l], w_o[l], ln1_g[l][None, :], ln1_b[l][None, :], wr, br)

        cnt, off, idx_sorted, w_sorted = _moe_plan(route, chunk)
        h2_slab = h2.reshape((nb * nt) // chunk, chunk * ROW_VREGS, LANES)
        y_slab = _moe_call(h2_slab, cnt, off, idx_sorted, w_sorted, w_eg[l], w_eu[l], w_ed[l])
        xa = _ln2_call(x1, y_slab, mods[l], ln2_g[l][None, :], ln2_b[l][None, :], chunk,
                       latent_only=l == depth - 1)

    return xa
```

```python
import functools
import math

import numpy as np
import jax
import jax.numpy as jnp
from jax import lax
from jax.experimental import pallas as pl
from jax.experimental.pallas import tpu as pltpu

F32 = jnp.float32
BF16 = jnp.bfloat16

D_MODEL = 1024
DEPTH = 4
GRID_W = 64
CTX_LEN = 256
HEAD_DIM = 64
A_HEADS = 4
A_KV_HEADS = 2
ROPE_THETA = 10000.0
B_HEADS = 4
WIN_R = 8
WIN_C = 16
C_HEADS = 4
D_HEADS = 4
D_KDIM = 32
D_GATE_RANK = 16
GLA_TAU = 16.0
CHUNK = 64
N_BRANCH = 4
N_GROUPS = 4
EXP_PER_GROUP = 8
N_EXPERTS = N_GROUPS * EXP_PER_GROUP
EXP_HIDDEN = 512
EPS = 1e-6
DN_ALPHA = (2.0 * DEPTH) ** 0.25
NEG_BIG = -1e30

LANES = 128
SUBLANES = 8
TOKEN_TILE = 256
VMEM_LIMIT = 56 * 1024 * 1024

_IN_OFFS = {}
_off = 0
for _n, _w in (('a_q', 256), ('a_k', 128), ('a_v', 128), ('b_q', 256), ('b_k', 256), ('b_v', 256),
               ('c_qkv', 768), ('c_beta', 8), ('c_a', 8), ('c_g', 256), ('d_q', 128), ('d_k', 128),
               ('d_v', 256), ('d_lr', 32), ('d_g', 256), ('gates', 4096)):
    _IN_OFFS[_n] = (_off, _w)
    _off += _w
IN_GROUPS = (
    ('a_q', ('a_q',), 256, BF16),
    ('a_kv', ('a_k', 'a_v'), 256, BF16),
    ('b_q', ('b_q',), 256, BF16),
    ('b_k', ('b_k',), 256, BF16),
    ('b_v', ('b_v',), 256, BF16),
    ('c_qkv', ('c_qkv',), 768, BF16),
    ('c_ba', ('c_beta', 'c_a'), 128, F32),
    ('c_g', ('c_g',), 256, F32),
    ('d_qk', ('d_q', 'd_k'), 256, F32),
    ('d_v', ('d_v',), 256, F32),
    ('d_lr', ('d_lr',), 128, F32),
    ('d_g', ('d_g',), 256, F32),
)
IN_TOTAL = sum(g[2] for g in IN_GROUPS)

MOE_CHUNK = 2048
MOE_ROWS = 160
MOE_STRIDE = MOE_ROWS + SUBLANES
ROW_VREGS = D_MODEL // LANES


def _dot(a, b):
    return jnp.dot(a.astype(BF16), b.astype(BF16), preferred_element_type=F32)


def _dot_nt(a, b):
    return lax.dot_general(a.astype(BF16), b.astype(BF16), (((1,), (1,)), ((), ())),
                           preferred_element_type=F32)


def _dot_tn(a, b):
    return lax.dot_general(a.astype(BF16), b.astype(BF16), (((0,), (0,)), ((), ())),
                           preferred_element_type=F32)


def _split(x):
    hi = x.astype(BF16)
    lo = (x - hi.astype(F32)).astype(BF16)
    return hi, lo


def _dot3(a, b):
    ah, al = _split(a)
    bh, bl = _split(b)
    return (jnp.dot(ah, bh, preferred_element_type=F32) + jnp.dot(al, bh, preferred_element_type=F32)
            + jnp.dot(ah, bl, preferred_element_type=F32))


def _dot_sel(a, m):
    ah, al = _split(a)
    return jnp.dot(ah, m, preferred_element_type=F32) + jnp.dot(al, m, preferred_element_type=F32)


def _silu(x):
    return x * jax.nn.sigmoid(x)


def _layer_norm(r, g, b):
    mu = jnp.mean(r, axis=-1, keepdims=True)
    d = r - mu
    var = jnp.mean(d * d, axis=-1, keepdims=True)
    return d * lax.rsqrt(var + EPS) * g + b


def _params(sem):
    return pltpu.CompilerParams(dimension_semantics=sem, vmem_limit_bytes=VMEM_LIMIT)


ADA_ROWS = 24
ADA_TILE = 1536


def _ada_kernel(cc_ref, w_ref, b_ref, o_ref):
    s = _silu(cc_ref[...])
    o_ref[0] = _dot3(s, w_ref[0]) + b_ref[0]


def _ada_call(cc, w_ada, b_ada):
    depth = w_ada.shape[0]
    n = w_ada.shape[2]
    return pl.pallas_call(
        _ada_kernel,
        grid=(depth, n // ADA_TILE),
        in_specs=[
            pl.BlockSpec((ADA_ROWS, D_MODEL), lambda l, j: (0, 0)),
            pl.BlockSpec((1, D_MODEL, ADA_TILE), lambda l, j: (l, 0, j)),
            pl.BlockSpec((1, 1, ADA_TILE), lambda l, j: (l, 0, j)),
        ],
        out_specs=pl.BlockSpec((1, ADA_ROWS, ADA_TILE), lambda l, j: (l, 0, j)),
        out_shape=jax.ShapeDtypeStruct((depth, ADA_ROWS, n), F32),
        compiler_params=_params(("arbitrary", "arbitrary")),
        name="ada_mod",
    )(cc, w_ada, b_ada.reshape(depth, 1, n))


def _mod_index(nb):
    return lambda b, t: (jnp.where(t == 0, nb, b), 0, 0)


def _project_in(x, mod, w_ref, out_refs):
    h = (x * (1.0 + mod[1:2]) + mod[0:1]).astype(BF16)
    off = 0
    for (name, _, width, dt), o_ref in zip(IN_GROUPS, out_refs):
        o_ref[0] = jnp.dot(h, w_ref[:, off:off + width], preferred_element_type=F32).astype(dt)
        off += width


def _in_kernel(x_ref, mod_ref, w_ref, *out_refs):
    _project_in(x_ref[0], mod_ref[0], w_ref, out_refs)


def _moe_residual_ln(x1, y_ref, gate, g_ref, b_ref):
    y = jnp.concatenate([y_ref[0, pl.ds(j, TOKEN_TILE, stride=ROW_VREGS), :] for j in range(ROW_VREGS)], axis=-1)
    return _layer_norm(DN_ALPHA * x1 + gate * y, g_ref[...], b_ref[...])


def _ln2_in_kernel(x1_ref, y_ref, mod_prev_ref, g_ref, b_ref, mod_ref, w_ref, xa_ref, *out_refs):
    xa = _moe_residual_ln(x1_ref[0], y_ref, mod_prev_ref[0][5:6], g_ref, b_ref)
    xa_ref[0] = xa
    _project_in(xa, mod_ref[0], w_ref, out_refs)


def _slab_index(tiles, chunk):
    per_chunk = chunk // TOKEN_TILE
    return lambda b, t: ((b * tiles + t) // per_chunk, (b * tiles + t) % per_chunk, 0)


def _ln2_in_call(x1, y_slab, mods_prev, ln_g, ln_b, mods, w_cat, chunk):
    nb, nt, d = x1.shape
    tiles = nt // TOKEN_TILE
    tok = lambda b, t: (b, t, 0)
    row = lambda b, t: (0, 0)
    outs = pl.pallas_call(
        _ln2_in_kernel,
        grid=(nb, tiles),
        in_specs=[
            pl.BlockSpec((1, TOKEN_TILE, d), tok),
            pl.BlockSpec((1, TOKEN_TILE * ROW_VREGS, LANES), _slab_index(tiles, chunk)),
            pl.BlockSpec((1, SUBLANES, d), _mod_index(nb)),
            pl.BlockSpec((1, d), row), pl.BlockSpec((1, d), row),
            pl.BlockSpec((1, SUBLANES, d), _mod_index(nb)),
            pl.BlockSpec((d, IN_TOTAL), row),
        ],
        out_specs=[pl.BlockSpec((1, TOKEN_TILE, d), tok)]
        + [pl.BlockSpec((1, TOKEN_TILE, g[2]), tok) for g in IN_GROUPS],
        out_shape=[jax.ShapeDtypeStruct((nb, nt, d), F32)]
        + [jax.ShapeDtypeStruct((nb, nt, g[2]), g[3]) for g in IN_GROUPS],
        compiler_params=_params(("arbitrary", "arbitrary")),
        name="ln2_in_proj",
    )(x1, y_slab, mods_prev, ln_g, ln_b, mods, w_cat)
    return outs[0], outs[1:]


def _in_call(xa, mods, w_cat):
    nb, nt, _ = xa.shape
    tiles = nt // TOKEN_TILE
    return pl.pallas_call(
        _in_kernel,
        grid=(nb, tiles),
        in_specs=[
            pl.BlockSpec((1, TOKEN_TILE, D_MODEL), lambda b, t: (b, t, 0)),
            pl.BlockSpec((1, SUBLANES, D_MODEL), _mod_index(nb)),
            pl.BlockSpec((D_MODEL, IN_TOTAL), lambda b, t: (0, 0)),
        ],
        out_specs=[pl.BlockSpec((1, TOKEN_TILE, g[2]), lambda b, t: (b, t, 0)) for g in IN_GROUPS],
        out_shape=[jax.ShapeDtypeStruct((nb, nt, g[2]), g[3]) for g in IN_GROUPS],
        compiler_params=_params(("arbitrary", "arbitrary")),
        name="in_proj",
    )(xa, mods, w_cat)


def _head_rms(x, ones_bd, gain):
    ss = _dot_sel(x * x, ones_bd)
    return x * lax.rsqrt(ss * (1.0 / HEAD_DIM) + EPS) * gain


def _rope(x, rot, cos, sin):
    return x * cos + _dot_sel(x, rot) * sin


def _attn_a_kernel(q_ref, kv_ref, cos_ref, sin_ref, qg_ref, kg_ref, bd_ref, rot_ref, o_ref, kp_ref, vt_ref):
    t = pl.program_id(1)
    kvw = A_KV_HEADS * HEAD_DIM
    rep = A_HEADS // A_KV_HEADS
    assert rep * HEAD_DIM == kvw == LANES
    n_tiles = kv_ref.shape[1] // TOKEN_TILE
    scale = HEAD_DIM ** -0.5
    lane = lax.broadcasted_iota(jnp.int32, (TOKEN_TILE, kvw), 1)
    first_half = lane < HEAD_DIM

    @pl.when(t == 0)
    def _prep_keys():
        bd = bd_ref[0:kvw, 0:kvw]
        rot = rot_ref[0:kvw, 0:kvw]
        kg = kg_ref[...]

        def put(i, k_rows, k):
            swapped = pltpu.roll(k, HEAD_DIM, axis=1)
            kp_ref[0, k_rows, :] = jnp.where(first_half, k, swapped).astype(BF16)
            kp_ref[1, k_rows, :] = jnp.where(first_half, swapped, k).astype(BF16)
            vt_ref[i] = kv_ref[0, k_rows, kvw:2 * kvw].astype(F32).T.astype(BF16)

        put(0, pl.ds(0, TOKEN_TILE), _head_rms(kv_ref[0, 0:TOKEN_TILE, 0:kvw].astype(F32), bd, kg))

        def body(i, carry):
            r0 = pl.multiple_of(i * TOKEN_TILE, TOKEN_TILE)
            rows = pl.ds(CTX_LEN + r0, TOKEN_TILE)
            kn = _head_rms(kv_ref[0, rows, 0:kvw].astype(F32), bd, kg)
            put(i + CTX_LEN // TOKEN_TILE, rows,
                _rope(kn, rot, cos_ref[pl.ds(r0, TOKEN_TILE), 0:kvw], sin_ref[pl.ds(r0, TOKEN_TILE), 0:kvw]))
            return carry

        lax.fori_loop(0, n_tiles - CTX_LEN // TOKEN_TILE, body, 0)

    qn = _head_rms(q_ref[0].astype(F32), bd_ref[...], qg_ref[...])

    def attend(qh, key_tiles):
        nk = key_tiles * TOKEN_TILE
        heads = [(g, r) for g in range(A_KV_HEADS) for r in range(rep)]
        qms = [jnp.where(first_half == (r == 0), qh[:, g * kvw:(g + 1) * kvw] * scale, 0.0).astype(BF16)
               for g, r in heads]
        ss = [_dot_nt(kp_ref[g, 0:nk, :], qms[i]) for i, (g, r) in enumerate(heads)]
        es = [jnp.exp(s - jnp.max(s, axis=0, keepdims=True)) for s in ss]
        ls = [jnp.sum(e, axis=0, keepdims=True) for e in es]
        ebs = [e.astype(BF16) for e in es]
        outs = []
        for i, (g, r) in enumerate(heads):
            o = None
            for k in range(key_tiles):
                part = jnp.dot(vt_ref[k, g * HEAD_DIM:(g + 1) * HEAD_DIM, :],
                               ebs[i][k * TOKEN_TILE:(k + 1) * TOKEN_TILE, :], preferred_element_type=F32)
                o = part if o is None else o + part
            outs.append(o / ls[i])
        return jnp.concatenate(outs, axis=0).T

    @pl.when(t == 0)
    def _ctx_queries():
        o_ref[0] = attend(qn, CTX_LEN // TOKEN_TILE).astype(o_ref.dtype)

    @pl.when(t > 0)
    def _latent_queries():
        r0 = pl.multiple_of((t - 1) * TOKEN_TILE, TOKEN_TILE)
        qr = _rope(qn, rot_ref[...], cos_ref[pl.ds(r0, TOKEN_TILE), :], sin_ref[pl.ds(r0, TOKEN_TILE), :])
        o_ref[0] = attend(qr, n_tiles).astype(o_ref.dtype)


def _attn_a_call(a_q, a_kv, cos_t, sin_t, q_gain, k_gain, ones_bd, rot_m):
    nb, nt, _ = a_q.shape
    tiles = nt // TOKEN_TILE
    seq = nt - CTX_LEN
    qw = A_HEADS * HEAD_DIM
    return pl.pallas_call(
        _attn_a_kernel,
        grid=(nb, tiles),
        in_specs=[
            pl.BlockSpec((1, TOKEN_TILE, qw), lambda b, t: (b, t, 0)),
            pl.BlockSpec((1, nt, qw), lambda b, t: (b, 0, 0)),
            pl.BlockSpec((seq, qw), lambda b, t: (0, 0)),
            pl.BlockSpec((seq, qw), lambda b, t: (0, 0)),
            pl.BlockSpec((1, qw), lambda b, t: (0, 0)),
            pl.BlockSpec((1, A_KV_HEADS * HEAD_DIM), lambda b, t: (0, 0)),
            pl.BlockSpec((qw, qw), lambda b, t: (0, 0)),
            pl.BlockSpec((qw, qw), lambda b, t: (0, 0)),
        ],
        out_specs=pl.BlockSpec((1, TOKEN_TILE, qw), lambda b, t: (b, t, 0)),
        out_shape=jax.ShapeDtypeStruct((nb, nt, qw), BF16),
        scratch_shapes=[pltpu.VMEM((A_KV_HEADS, nt, A_KV_HEADS * HEAD_DIM), BF16),
                        pltpu.VMEM((tiles, A_KV_HEADS * HEAD_DIM, TOKEN_TILE), BF16)],
        compiler_params=_params(("arbitrary", "arbitrary")),
        name="mixer_a_gqa",
    )(a_q, a_kv, cos_t, sin_t, q_gain, k_gain, ones_bd, rot_m)


NB_QROWS = TOKEN_TILE // GRID_W
NB_KROWS = 12
NB_INVALID = 2 * WIN_R - 1


def _attn_b_kernel(q_ref, k_ref, v_ref, bt_ref, o_ref):
    t = pl.program_id(1)
    scale = HEAD_DIM ** -0.5
    rows = (k_ref.shape[1] - CTX_LEN) // GRID_W
    wr = min(WIN_R, rows)

    def softmax_pv(parts):
        m = None
        for s, _ in parts:
            mi = jnp.max(s, axis=-1, keepdims=True)
            m = mi if m is None else jnp.maximum(m, mi)
        acc, l = None, None
        for s, v in parts:
            e = jnp.exp(s - m)
            li = jnp.sum(e, axis=-1, keepdims=True)
            oi = jnp.dot(e.astype(BF16), v, preferred_element_type=F32)
            acc = oi if acc is None else acc + oi
            l = li if l is None else l + li
        return acc / l

    @pl.when(t == 0)
    def _ctx_queries():
        outs = []
        for h in range(B_HEADS):
            sl = slice(h * HEAD_DIM, (h + 1) * HEAD_DIM)
            qq = (q_ref[0, :, sl].astype(F32) * scale).astype(BF16)
            s = _dot_nt(qq, k_ref[0, 0:CTX_LEN, sl])
            outs.append(softmax_pv([(s, v_ref[0, 0:CTX_LEN, sl])]))
        o_ref[0] = jnp.concatenate(outs, axis=-1).astype(o_ref.dtype)

    @pl.when(t > 0)
    def _latent_queries():
        r0 = (t - 1) * NB_QROWS
        start = jnp.clip(r0 - wr // 2, 0, rows - NB_KROWS)
        k0 = pl.multiple_of(CTX_LEN + start * GRID_W, GRID_W)
        nk = NB_KROWS * GRID_W
        lane = lax.broadcasted_iota(jnp.int32, (GRID_W, 2 * GRID_W), 1)
        left = lane < GRID_W
        slots = []
        for i in range(NB_QROWS):
            r = r0 + i
            rs = jnp.clip(r - wr // 2, 0, rows - wr)
            row_slots = []
            for j in range(NB_KROWS):
                kr = start + j
                ok = jnp.logical_and(kr >= rs, kr < rs + wr)
                row_slots.append(jnp.where(ok, kr - r + WIN_R - 1, NB_INVALID))
            slots.append(row_slots)
        sls = [slice(h * HEAD_DIM, (h + 1) * HEAD_DIM) for h in range(B_HEADS)]
        qqs = [(q_ref[0, :, sl].astype(F32) * scale).astype(BF16) for sl in sls]
        s_locs = [_dot_nt(qqs[h], k_ref[0, pl.ds(k0, nk), sls[h]]) for h in range(B_HEADS)]
        s_ctxs = [_dot_nt(qqs[h], k_ref[0, 0:CTX_LEN, sls[h]]) for h in range(B_HEADS)]
        probs = []
        for h in range(B_HEADS):
            bias_rows = []
            for i in range(NB_QROWS):
                tiles = []
                for jp in range(NB_KROWS // 2):
                    b0 = bt_ref[h, slots[i][2 * jp]]
                    b1 = bt_ref[h, slots[i][2 * jp + 1]]
                    tiles.append(jnp.where(left, b0, b1))
                bias_rows.append(jnp.concatenate(tiles, axis=-1))
            s_loc = s_locs[h] + jnp.concatenate(bias_rows, axis=0)
            m = jnp.maximum(jnp.max(s_loc, axis=-1, keepdims=True), jnp.max(s_ctxs[h], axis=-1, keepdims=True))
            e_loc = jnp.exp(s_loc - m)
            e_ctx = jnp.exp(s_ctxs[h] - m)
            probs.append((e_loc.astype(BF16), e_ctx.astype(BF16),
                          jnp.sum(e_loc, axis=-1, keepdims=True) + jnp.sum(e_ctx, axis=-1, keepdims=True)))
        outs = [(jnp.dot(probs[h][0], v_ref[0, pl.ds(k0, nk), sls[h]], preferred_element_type=F32)
                 + jnp.dot(probs[h][1], v_ref[0, 0:CTX_LEN, sls[h]], preferred_element_type=F32)) / probs[h][2]
                for h in range(B_HEADS)]
        o_ref[0] = jnp.concatenate(outs, axis=-1).astype(o_ref.dtype)


def _attn_b_call(b_q, b_k, b_v, bias_tab):
    nb, nt, w = b_q.shape
    tiles = nt // TOKEN_TILE
    return pl.pallas_call(
        _attn_b_kernel,
        grid=(nb, tiles),
        in_specs=[
            pl.BlockSpec((1, TOKEN_TILE, w), lambda b, t: (b, t, 0)),
            pl.BlockSpec((1, nt, w), lambda b, t: (b, 0, 0)),
            pl.BlockSpec((1, nt, w), lambda b, t: (b, 0, 0)),
            pl.BlockSpec(bias_tab.shape, lambda b, t: (0, 0, 0, 0)),
        ],
        out_specs=pl.BlockSpec((1, TOKEN_TILE, w), lambda b, t: (b, t, 0)),
        out_shape=jax.ShapeDtypeStruct((nb, nt, w), BF16),
        compiler_params=_params(("arbitrary", "arbitrary")),
        name="mixer_b_neighbourhood",
    )(b_q, b_k, b_v, bias_tab)


def _nb_bias_table(rpb):
    cols = jnp.arange(GRID_W, dtype=jnp.int32)
    col_start = jnp.clip(cols - WIN_C // 2, 0, GRID_W - WIN_C)
    col_ok = (cols[None, :] >= col_start[:, None]) & (cols[None, :] < col_start[:, None] + WIN_C)
    dc_idx = jnp.clip(cols[None, :] - cols[:, None] + WIN_C - 1, 0, 2 * WIN_C - 2)
    tab = rpb.astype(F32)[:, :, dc_idx]
    tab = jnp.where(col_ok[None, None], tab, NEG_BIG)
    tab = jnp.concatenate([tab, jnp.full_like(tab[:, :1], NEG_BIG)], axis=1)
    return jnp.concatenate([tab, tab], axis=-1)


def _order_masks():
    i = lax.broadcasted_iota(jnp.int32, (CHUNK, CHUNK), 0)
    j = lax.broadcasted_iota(jnp.int32, (CHUNK, CHUNK), 1)
    return ((j <= i, j < i), (j >= i, j > i))


def _scan_chunk(step, direction, n_ctx_chunks, n_chunks):
    if direction == 0:
        return step
    return jnp.where(step < n_ctx_chunks, n_ctx_chunks - 1 - step, n_chunks + n_ctx_chunks - 1 - step)


def _sel_dot(m, a):
    ah, al = _split(a)
    return jnp.dot(m, ah, preferred_element_type=F32) + jnp.dot(m, al, preferred_element_type=F32)


def _gated_out_tiles(part_refs, gate_ref, gain_ref, bd_ref, o_ref):
    nt = part_refs[0].shape[0]
    bd = bd_ref[...]
    gain = gain_ref[...]

    def body(i, carry):
        r0 = pl.multiple_of(i * TOKEN_TILE, TOKEN_TILE)
        o = sum(p[pl.ds(r0, TOKEN_TILE), :] for p in part_refs)
        y = _head_rms(o, bd, gain) * _silu(gate_ref[0, pl.ds(r0, TOKEN_TILE), :])
        o_ref[0, pl.ds(r0, TOKEN_TILE), :] = y.astype(o_ref.dtype)
        return carry

    lax.fori_loop(0, nt // TOKEN_TILE, body, 0)


C_BETA_LANE = 0
C_A_LANE = 8
C_T_ROWS = 16


SOLVE_BLOCK = 16
GDN_LOCAL_CHUNKS = 4


def _unit_lower_solve(lmats, rhss, same_block, eye):
    n = range(len(lmats))
    lds = [jnp.where(same_block, l, 0.0) for l in lmats]
    ts = [eye - ld for ld in lds]
    ps = lds
    span = 2
    while span < SOLVE_BLOCK:
        ps = [_dot(p, p) for p in ps]
        ts = [ts[i] + _dot(ts[i], ps[i]) for i in n]
        span *= 2
    width = rhss[0].shape[1]
    mzs = [_dot(ts[i], jnp.concatenate([rhss[i], lmats[i] - lds[i]], axis=-1)) for i in n]
    mmzs = [_dot(mz[:, width:], mz) for mz in mzs]
    zs = [mzs[i][:, 0:width] - mmzs[i][:, 0:width] for i in n]
    ps = [mmz[:, width:] for mmz in mmzs]
    span = 2
    while span < CHUNK // SOLVE_BLOCK:
        zs = [zs[i] + _dot(ps[i], zs[i]) for i in n]
        span *= 2
        if span < CHUNK // SOLVE_BLOCK:
            ps = [_dot(p, p) for p in ps]
    return zs


def _gdn_kernel(qkv_ref, ba_ref, gate_ref, conv_ref, par_ref, gain_ref, bd_ref, o_ref,
                q_s, k_s, v_s, bl_s, g_s, gt_s, a12_s, b2_s, egl_s, o_s, st_s):
    nt = qkv_ref.shape[1]
    n_tiles = nt // TOKEN_TILE
    n_chunks = nt // CHUNK
    n_ctx_chunks = CTX_LEN // CHUNK
    w = C_HEADS * HEAD_DIM
    pack = 2 * SUBLANES
    bd = bd_ref[...]
    lane = lax.broadcasted_iota(jnp.int32, (TOKEN_TILE, LANES), 1)
    lane_c = lax.broadcasted_iota(jnp.int32, (CHUNK, LANES), 1)
    row = lax.broadcasted_iota(jnp.int32, (TOKEN_TILE, 1), 0)
    neg_rate = -jnp.exp(par_ref[0:1, :])
    dt_bias = par_ref[1:2, :]
    w_prev, w_mid, w_next = conv_ref[0:1, :], conv_ref[1:2, :], conv_ref[2:3, :]
    masks = _order_masks()
    mask_bf = [jnp.where(m[0], 1.0, 0.0).astype(BF16) for m in masks]

    def prep(i, carry):
        r0 = pl.multiple_of(i * TOKEN_TILE, TOKEN_TILE)
        x = qkv_ref[0, pl.ds(r0, TOKEN_TILE), :].astype(F32)
        before = qkv_ref[0, pl.ds(pl.multiple_of(jnp.maximum(r0 - pack, 0), pack), pack), :].astype(F32)
        after = qkv_ref[0, pl.ds(pl.multiple_of(jnp.minimum(r0 + TOKEN_TILE, nt - pack), pack), pack), :].astype(F32)
        first_of_seq = jnp.logical_or(i == 0, i == CTX_LEN // TOKEN_TILE)
        last_of_seq = jnp.logical_or(i == CTX_LEN // TOKEN_TILE - 1, i == n_tiles - 1)
        edge_prev = jnp.where(first_of_seq, 0.0, before[pack - 1:pack, :])
        edge_next = jnp.where(last_of_seq, 0.0, after[0:1, :])
        x_prev = jnp.where(row == 0, edge_prev, pltpu.roll(x, 1, axis=0))
        x_next = jnp.where(row == TOKEN_TILE - 1, edge_next, pltpu.roll(x, TOKEN_TILE - 1, axis=0))
        y = _silu(x_prev * w_prev + x * w_mid + x_next * w_next)
        q, k, v = y[:, 0:w], y[:, w:2 * w], y[:, 2 * w:3 * w]
        q_s[pl.ds(r0, TOKEN_TILE), :] = q * lax.rsqrt(_dot_sel(q * q, bd) + EPS) * (HEAD_DIM ** -0.5)
        k_s[pl.ds(r0, TOKEN_TILE), :] = k * lax.rsqrt(_dot_sel(k * k, bd) + EPS)
        v_s[pl.ds(r0, TOKEN_TILE), :] = v
        ba = ba_ref[0, pl.ds(r0, TOKEN_TILE), :]
        sp = ba + dt_bias
        softplus = jnp.maximum(sp, 0.0) + jnp.log1p(jnp.exp(-jnp.abs(sp)))
        bl = jnp.where(lane < C_A_LANE, jax.nn.sigmoid(ba), neg_rate * softplus)
        bl_s[pl.ds(r0, TOKEN_TILE), :] = bl
        for c in range(TOKEN_TILE // CHUNK):
            blc = bl[c * CHUNK:(c + 1) * CHUNK, :]
            g = jnp.where(lane_c < C_A_LANE + C_HEADS, _sel_dot(mask_bf[0], blc), _sel_dot(mask_bf[1], blc))
            g_s[pl.ds(r0 + c * CHUNK, CHUNK), :] = g
            gt_s[i * (TOKEN_TILE // CHUNK) + c] = g.T[0:C_T_ROWS, :]
        return carry

    lax.fori_loop(0, n_tiles, prep, 0)

    ri = lax.broadcasted_iota(jnp.int32, (CHUNK, CHUNK), 0)
    ci = lax.broadcasted_iota(jnp.int32, (CHUNK, CHUNK), 1)
    same_block = (ri // SOLVE_BLOCK) == (ci // SOLVE_BLOCK)
    eye = jnp.where(ri == ci, 1.0, 0.0)

    src = lax.broadcasted_iota(jnp.int32, (LANES, w), 0)
    dst_head = lax.broadcasted_iota(jnp.int32, (LANES, w), 1) // HEAD_DIM
    spread_g = [jnp.where(src == C_A_LANE + d * C_HEADS + dst_head, 1.0, 0.0).astype(BF16) for d in range(2)]
    spread_b = [jnp.where(src == C_BETA_LANE + d * C_HEADS + dst_head, 1.0, 0.0).astype(BF16) for d in range(2)]
    heads = [slice(h * HEAD_DIM, (h + 1) * HEAD_DIM) for h in range(C_HEADS)]
    lane_head = lax.broadcasted_iota(jnp.int32, (CHUNK, w), 1) // HEAD_DIM

    def local(it, carry):
        chunks = [it * GDN_LOCAL_CHUNKS + t for t in range(GDN_LOCAL_CHUNKS)]
        rows = [pl.ds(pl.multiple_of(c * CHUNK, CHUNK), CHUNK) for c in chunks]
        td = [(t, d) for t in range(GDN_LOCAL_CHUNKS) for d in range(2)]
        qcs, kcs, vcs = [q_s[r, :] for r in rows], [k_s[r, :] for r in rows], [v_s[r, :] for r in rows]
        g_alls = [_dot_sel(g_s[rows[t], :], spread_g[d]) for t, d in td]
        b_alls = [_dot_sel(bl_s[rows[t], :], spread_b[d]) for t, d in td]
        qks = [[_dot_nt(jnp.where(lane_head == h, qcs[t], 0.0), kcs[t]) for h in range(C_HEADS)]
               for t in range(GDN_LOCAL_CHUNKS)]
        pre = []
        for i, (t, d) in enumerate(td):
            last = CHUNK - 1 if d == 0 else 0
            e_g = jnp.exp(g_alls[i])
            g_last = g_alls[i][last:last + 1, :]
            kb = kcs[t] * b_alls[i]
            kd_t = (kcs[t] * jnp.exp(g_last - g_alls[i])).T.astype(BF16)
            egl_s[d, chunks[t]] = jnp.broadcast_to(jnp.exp(g_last), (SUBLANES, w))
            pre.append((kb, kb * e_g, vcs[t] * b_alls[i], qcs[t] * e_g, kd_t))
        kbk = [[_dot_nt(jnp.where(lane_head == h, pre[i][0], 0.0), kcs[t]) for h in range(C_HEADS)]
               for i, (t, d) in enumerate(td)]
        lmats, rhss, keep = [], [], []
        for i, (t, d) in enumerate(td):
            incl, strict = masks[d]
            g_rows = gt_s[chunks[t]]
            for h, sl in enumerate(heads):
                la = C_A_LANE + d * C_HEADS + h
                decay = jnp.exp(jnp.where(incl, g_alls[i][:, sl] - g_rows[la:la + 1, :], NEG_BIG))
                lmats.append(jnp.where(strict, kbk[i][h] * decay, 0.0))
                rhss.append(jnp.concatenate([pre[i][1][:, sl], pre[i][2][:, sl]], axis=-1))
                keep.append(((qks[t][h] * decay).astype(BF16), pre[i][3][:, sl], pre[i][4][sl, :]))
        sols = [s.astype(BF16) for s in _unit_lower_solve(lmats, rhss, same_block, eye)]
        qwu = [jnp.dot(keep[i][0], sols[i], preferred_element_type=F32) for i in range(len(sols))]
        kwu = [jnp.dot(keep[i][2], sols[i], preferred_element_type=F32) for i in range(len(sols))]
        i = 0
        for t in range(GDN_LOCAL_CHUNKS):
            c = it * GDN_LOCAL_CHUNKS + t
            r0 = pl.multiple_of(c * CHUNK, CHUNK)
            o_const = None
            for d in range(2):
                u = range(i, i + C_HEADS)
                a12_s[d, c, 0:CHUNK, :] = jnp.concatenate(
                    [keep[j][1] - qwu[j][:, 0:HEAD_DIM] for j in u], axis=-1).astype(BF16)
                a12_s[d, c, CHUNK:2 * CHUNK, :] = jnp.concatenate(
                    [-kwu[j][:, 0:HEAD_DIM] for j in u], axis=-1).astype(BF16)
                b2_s[d, c] = jnp.concatenate([kwu[j][:, HEAD_DIM:] for j in u], axis=-1)
                part = jnp.concatenate([qwu[j][:, HEAD_DIM:] for j in u], axis=-1)
                o_const = part if o_const is None else o_const + part
                i += C_HEADS
            o_s[pl.ds(r0, CHUNK), :] = o_const
        return carry

    lax.fori_loop(0, n_chunks // GDN_LOCAL_CHUNKS, local, 0)

    st_s[...] = jnp.zeros(st_s.shape, F32)

    def scan(step, carry):
        units = [(d, h) for d in range(2) for h in range(C_HEADS)]
        cs = [_scan_chunk(step, d, n_ctx_chunks, n_chunks) for d in range(2)]
        a12 = [a12_s[d, cs[d]] for d in range(2)]
        b2 = [b2_s[d, cs[d]] for d in range(2)]
        egl = [egl_s[d, cs[d]] for d in range(2)]
        ss = [st_s[d, h] for d, h in units]
        rs = [jnp.dot(a12[d][:, heads[h]], ss[i].astype(BF16), preferred_element_type=F32)
              for i, (d, h) in enumerate(units)]
        for i, (d, h) in enumerate(units):
            st_s[d, h] = ss[i] * egl[d][0:1, heads[h]] + rs[i][CHUNK:, :] + b2[d][:, heads[h]]
        for d in range(2):
            r0 = pl.multiple_of(cs[d] * CHUNK, CHUNK)
            o = jnp.concatenate([rs[d * C_HEADS + h][0:CHUNK, :] for h in range(C_HEADS)], axis=-1)
            o_s[pl.ds(r0, CHUNK), :] = o_s[pl.ds(r0, CHUNK), :] + o
        return carry

    lax.fori_loop(0, n_chunks, scan, 0)
    _gated_out_tiles((o_s,), gate_ref, gain_ref, bd_ref, o_ref)


def _gdn_call(c_qkv, c_ba, c_g, conv_w, par, gain, ones_bd):
    nb, nt, _ = c_qkv.shape
    w = C_HEADS * HEAD_DIM
    nc = nt // CHUNK
    return pl.pallas_call(
        _gdn_kernel,
        grid=(nb,),
        in_specs=[
            pl.BlockSpec((1, nt, 3 * w), lambda b: (b, 0, 0)),
            pl.BlockSpec((1, nt, LANES), lambda b: (b, 0, 0)),
            pl.BlockSpec((1, nt, w), lambda b: (b, 0, 0)),
            pl.BlockSpec((SUBLANES, 3 * w), lambda b: (0, 0)),
            pl.BlockSpec((SUBLANES, LANES), lambda b: (0, 0)),
            pl.BlockSpec((1, w), lambda b: (0, 0)),
            pl.BlockSpec((w, w), lambda b: (0, 0)),
        ],
        out_specs=pl.BlockSpec((1, nt, w), lambda b: (b, 0, 0)),
        out_shape=jax.ShapeDtypeStruct((nb, nt, w), BF16),
        scratch_shapes=[
            pltpu.VMEM((nt, w), F32), pltpu.VMEM((nt, w), F32), pltpu.VMEM((nt, w), F32),
            pltpu.VMEM((nt, LANES), F32), pltpu.VMEM((nt, LANES), F32),
            pltpu.VMEM((nc, C_T_ROWS, CHUNK), F32),
            pltpu.VMEM((2, nc, 2 * CHUNK, w), BF16),
            pltpu.VMEM((2, nc, CHUNK, w), F32),
            pltpu.VMEM((2, nc, SUBLANES, w), F32),
            pltpu.VMEM((nt, w), F32),
            pltpu.VMEM((2, C_HEADS, HEAD_DIM, HEAD_DIM), F32),
        ],
        compiler_params=_params(("arbitrary",)),
        name="mixer_c_gated_delta",
    )(c_qkv, c_ba, c_g, conv_w, par, gain, ones_bd)


GLA_EXP_CAP = 80.0
GLA_LOCAL_CHUNKS = 2
GLA_SCAN_STEPS = 4


def _gla_kernel(qk_ref, v_ref, lr_ref, gate_ref, gw_ref, gb_ref, gain_ref, bd_ref, o_ref,
                la_s, qg_s, el_s, ds_s, o_s):
    nt = qk_ref.shape[1]
    n_tiles = nt // TOKEN_TILE
    n_chunks = nt // CHUNK
    n_ctx_chunks = CTX_LEN // CHUNK
    kw = D_HEADS * D_KDIM
    gw = gw_ref[...]
    gb = gb_ref[...]

    def prep(i, carry):
        r0 = pl.multiple_of(i * TOKEN_TILE, TOKEN_TILE)
        z = _dot3(lr_ref[0, pl.ds(r0, TOKEN_TILE), :], gw) + gb
        log_sig = jnp.minimum(z, 0.0) - jnp.log1p(jnp.exp(-jnp.abs(z)))
        la_s[pl.ds(r0, TOKEN_TILE), :] = log_sig * (1.0 / GLA_TAU)
        return carry

    lax.fori_loop(0, n_tiles, prep, 0)

    masks = _order_masks()
    mask_bf = [jnp.where(m[0], 1.0, 0.0).astype(BF16) for m in masks]
    vw = D_HEADS * HEAD_DIM
    k_head = lax.broadcasted_iota(jnp.int32, (CHUNK, kw), 1) // D_KDIM
    v_head = lax.broadcasted_iota(jnp.int32, (CHUNK, vw), 1) // HEAD_DIM
    state_diag = (lax.broadcasted_iota(jnp.int32, (vw, kw), 0) // HEAD_DIM
                  == lax.broadcasted_iota(jnp.int32, (vw, kw), 1) // D_KDIM)

    def local(it, carry):
        units, pre = [], {}
        for t in range(GLA_LOCAL_CHUNKS):
            c = it * GLA_LOCAL_CHUNKS + t
            r0 = pl.multiple_of(c * CHUNK, CHUNK)
            qk = qk_ref[0, pl.ds(r0, CHUNK), :]
            q = qk[:, 0:kw] * (D_KDIM ** -0.5)
            k = qk[:, kw:2 * kw]
            v = v_ref[0, pl.ds(r0, CHUNK), :]
            v_b = v.astype(BF16)
            v_t = v.T.astype(BF16)
            for d in range(2):
                last = CHUNK - 1 if d == 0 else 0
                g = _sel_dot(mask_bf[d], la_s[pl.ds(r0, CHUNK), d * kw:(d + 1) * kw])
                g_mid = g[CHUNK // 2:CHUNK // 2 + 1, :]
                g_last = g[last:last + 1, :]
                q_t = q * jnp.exp(jnp.minimum(g - g_mid, GLA_EXP_CAP))
                pre[(t, d)] = ([jnp.where(k_head == h, q_t, 0.0).astype(BF16) for h in range(D_HEADS)],
                               (k * jnp.exp(jnp.minimum(g_mid - g, GLA_EXP_CAP))).astype(BF16),
                               (k * jnp.exp(g_last - g)).astype(BF16), v_b, v_t)
                qg_s[d, c] = (q * jnp.exp(g)).astype(BF16)
                el_s[d, c] = jnp.broadcast_to(jnp.exp(g_last), (SUBLANES, kw))
                units += [(t, d, h) for h in range(D_HEADS)]
        a_ = [jnp.where(masks[d][0], _dot_nt(pre[(t, d)][0][h], pre[(t, d)][1]), 0.0).astype(BF16)
              for t, d, h in units]
        ds_ = {td: jnp.dot(p[4], p[2], preferred_element_type=F32) for td, p in pre.items()}
        av = [jnp.dot(a_[i], pre[(t, d)][3], preferred_element_type=F32) for i, (t, d, h) in enumerate(units)]
        i = 0
        for t in range(GLA_LOCAL_CHUNKS):
            c = it * GLA_LOCAL_CHUNKS + t
            r0 = pl.multiple_of(c * CHUNK, CHUNK)
            o_const = jnp.zeros((CHUNK, vw), F32)
            for d in range(2):
                ds_s[d, c] = jnp.where(state_diag, ds_[(t, d)], 0.0)
                for h in range(D_HEADS):
                    o_const = o_const + jnp.where(v_head == h, av[i], 0.0)
                    i += 1
            o_s[pl.ds(r0, CHUNK), :] = o_const
        return carry

    lax.fori_loop(0, n_chunks // GLA_LOCAL_CHUNKS, local, 0)

    def scan(it, states):
        states = list(states)
        jobs = []
        for t in range(GLA_SCAN_STEPS):
            step = it * GLA_SCAN_STEPS + t
            for d in range(2):
                c = _scan_chunk(step, d, n_ctx_chunks, n_chunks)
                jobs.append((c, qg_s[d, c], states[d].astype(BF16)))
                states[d] = states[d] * el_s[d, c][0:1, :] + ds_s[d, c]
        outs = [_dot_nt(qg, sb) for _, qg, sb in jobs]
        for (c, _, _), o in zip(jobs, outs):
            r0 = pl.multiple_of(c * CHUNK, CHUNK)
            o_s[pl.ds(r0, CHUNK), :] = o_s[pl.ds(r0, CHUNK), :] + o
        return tuple(states)

    zero = jnp.zeros((vw, kw), F32)
    lax.fori_loop(0, n_chunks // GLA_SCAN_STEPS, scan, (zero, zero))
    _gated_out_tiles((o_s,), gate_ref, gain_ref, bd_ref, o_ref)


def _gla_call(d_qk, d_v, d_lr, d_g, gw_blk, gb_row, gain, ones_bd):
    nb, nt, _ = d_qk.shape
    vw = D_HEADS * HEAD_DIM
    kw2 = 2 * D_HEADS * D_KDIM
    return pl.pallas_call(
        _gla_kernel,
        grid=(nb,),
        in_specs=[
            pl.BlockSpec((1, nt, kw2), lambda b: (b, 0, 0)),
            pl.BlockSpec((1, nt, vw), lambda b: (b, 0, 0)),
            pl.BlockSpec((1, nt, LANES), lambda b: (b, 0, 0)),
            pl.BlockSpec((1, nt, vw), lambda b: (b, 0, 0)),
            pl.BlockSpec((LANES, kw2), lambda b: (0, 0)),
            pl.BlockSpec((1, kw2), lambda b: (0, 0)),
            pl.BlockSpec((1, vw), lambda b: (0, 0)),
            pl.BlockSpec((vw, vw), lambda b: (0, 0)),
        ],
        out_specs=pl.BlockSpec((1, nt, vw), lambda b: (b, 0, 0)),
        out_shape=jax.ShapeDtypeStruct((nb, nt, vw), BF16),
        scratch_shapes=[
            pltpu.VMEM((nt, kw2), F32),
            pltpu.VMEM((2, nt // CHUNK, CHUNK, kw2 // 2), BF16),
            pltpu.VMEM((2, nt // CHUNK, SUBLANES, kw2 // 2), F32),
            pltpu.VMEM((2, nt // CHUNK, vw, kw2 // 2), F32),
            pltpu.VMEM((nt, vw), F32),
        ],
        compiler_params=_params(("arbitrary",)),
        name="mixer_d_gla",
    )(d_qk, d_v, d_lr, d_g, gw_blk, gb_row, gain, ones_bd)


ROUTE_E1, ROUTE_E2, ROUTE_W1, ROUTE_W2 = 0, 1, 2, 3
ROUTER_EXPERT_LANE = N_GROUPS


def _route(logits):
    lane = lax.broadcasted_iota(jnp.int32, logits.shape, 1).astype(F32)
    far = float(LANES)
    in_grp = lane < N_GROUPS
    lg = jnp.where(in_grp, logits, NEG_BIG)
    mg = jnp.max(lg, axis=-1, keepdims=True)
    grp = jnp.min(jnp.where(lg == mg, lane, far), axis=-1, keepdims=True)
    p_grp = 1.0 / jnp.sum(jnp.where(in_grp, jnp.exp(lg - mg), 0.0), axis=-1, keepdims=True)
    lo = ROUTER_EXPERT_LANE + EXP_PER_GROUP * grp
    in_exp = jnp.logical_and(lane >= lo, lane < lo + EXP_PER_GROUP)
    le = jnp.where(in_exp, logits, NEG_BIG)
    m1 = jnp.max(le, axis=-1, keepdims=True)
    i1 = jnp.min(jnp.where(le == m1, lane, far), axis=-1, keepdims=True)
    le2 = jnp.where(lane == i1, NEG_BIG, le)
    m2 = jnp.max(le2, axis=-1, keepdims=True)
    i2 = jnp.min(jnp.where(le2 == m2, lane, far), axis=-1, keepdims=True)
    e2 = jnp.exp(m2 - m1)
    w1 = p_grp / (1.0 + e2)
    w2 = p_grp * e2 / (1.0 + e2)
    out = jnp.where(lane == ROUTE_E1, i1 - ROUTER_EXPERT_LANE, 0.0)
    out = jnp.where(lane == ROUTE_E2, i2 - ROUTER_EXPERT_LANE, out)
    out = jnp.where(lane == ROUTE_W1, w1, out)
    return jnp.where(lane == ROUTE_W2, w2, out)


def _merge_kernel(x_ref, mod_ref, oa_ref, ob_ref, oc_ref, od_ref, wg_ref, wbr_ref, wo_ref,
                  lng_ref, lnb_ref, wr_ref, br_ref, x1_ref, h2_ref, route_ref):
    x = x_ref[0]
    mod = mod_ref[0]
    h = (x * (1.0 + mod[1:2]) + mod[0:1]).astype(BF16)
    m = None
    for z, o_ref in enumerate((oa_ref, ob_ref, oc_ref, od_ref)):
        gate = jax.nn.sigmoid(jnp.dot(h, wg_ref[:, z * D_MODEL:(z + 1) * D_MODEL], preferred_element_type=F32))
        up = jnp.dot(o_ref[0], wbr_ref[z], preferred_element_type=F32)
        m = gate * up if m is None else m + gate * up
    y = jnp.dot(m.astype(BF16), wo_ref[...], preferred_element_type=F32)
    x1 = _layer_norm(DN_ALPHA * x + mod[2:3] * y, lng_ref[...], lnb_ref[...])
    x1_ref[0] = x1
    h2 = x1 * (1.0 + mod[4:5]) + mod[3:4]
    for j in range(ROW_VREGS):
        h2_ref[pl.ds(j, TOKEN_TILE, stride=ROW_VREGS), :] = h2[:, j * LANES:(j + 1) * LANES]
    route_ref[0] = _route(_dot3(h2, wr_ref[...]) + br_ref[...]).T[0:SUBLANES, :]


def _merge_call(xa, mods, oa, ob, oc, od, wg, wbr, wo, ln_g, ln_b, wr, br):
    nb, nt, d = xa.shape
    tiles = nt // TOKEN_TILE
    bw = oa.shape[-1]
    tok = lambda b, t: (b, t, 0)
    const2 = lambda b, t: (0, 0)
    return pl.pallas_call(
        _merge_kernel,
        grid=(nb, tiles),
        in_specs=[
            pl.BlockSpec((1, TOKEN_TILE, d), tok),
            pl.BlockSpec((1, SUBLANES, d), _mod_index(nb)),
            pl.BlockSpec((1, TOKEN_TILE, bw), tok), pl.BlockSpec((1, TOKEN_TILE, bw), tok),
            pl.BlockSpec((1, TOKEN_TILE, bw), tok), pl.BlockSpec((1, TOKEN_TILE, bw), tok),
            pl.BlockSpec((d, N_BRANCH * d), const2),
            pl.BlockSpec((N_BRANCH, bw, d), lambda b, t: (0, 0, 0)),
            pl.BlockSpec((d, d), const2),
            pl.BlockSpec((1, d), const2), pl.BlockSpec((1, d), const2),
            pl.BlockSpec((d, LANES), const2), pl.BlockSpec((1, LANES), const2),
        ],
        out_specs=[pl.BlockSpec((1, TOKEN_TILE, d), tok),
                   pl.BlockSpec((TOKEN_TILE * ROW_VREGS, LANES), lambda b, t: (b * tiles + t, 0)),
                   pl.BlockSpec((1, SUBLANES, TOKEN_TILE), lambda b, t: (b * tiles + t, 0, 0))],
        out_shape=[jax.ShapeDtypeStruct((nb, nt, d), F32),
                   jax.ShapeDtypeStruct((nb * nt * ROW_VREGS, LANES), F32),
                   jax.ShapeDtypeStruct((nb * tiles, SUBLANES, TOKEN_TILE), F32)],
        compiler_params=_params(("arbitrary", "arbitrary")),
        name="merge_out_ln1_router",
    )(xa, mods, oa, ob, oc, od, wg, wbr, wo, ln_g, ln_b, wr, br)


MOE_UNROLL = 4


def _moe_chunk(total):
    return max(c for c in range(TOKEN_TILE, MOE_CHUNK + 1, TOKEN_TILE) if total % c == 0)


MOE_GROUP = 2


def _moe_kernel(cnt_ref, off_ref, off_again_ref, idx_ref, wt_ref, x_ref, *refs):
    w_refs = [refs[3 * g:3 * g + 3] for g in range(MOE_GROUP)]
    y_ref = refs[3 * MOE_GROUP]
    xt_s = refs[3 * MOE_GROUP + 1:3 * MOE_GROUP + 1 + MOE_GROUP]
    ot_s = refs[3 * MOE_GROUP + 1 + MOE_GROUP:]
    c = pl.program_id(0)
    j = pl.program_id(1)
    chunk = x_ref.shape[1] // ROW_VREGS
    plan_rows = 2 * chunk

    @pl.when(j == 0)
    def _zero():
        y_ref[...] = jnp.zeros(y_ref.shape, F32)

    n_rows = [cnt_ref[c * N_EXPERTS + j * MOE_GROUP + g] for g in range(MOE_GROUP)]
    off = [off_ref[c * N_EXPERTS + j * MOE_GROUP + g] for g in range(MOE_GROUP)]
    off_again = [off_again_ref[c * N_EXPERTS + j * MOE_GROUP + g] for g in range(MOE_GROUP)]
    n_tiles = functools.reduce(jnp.maximum, [(n + MOE_ROWS - 1) // MOE_ROWS for n in n_rows])

    def slab(tok):
        return pl.ds(pl.multiple_of(tok * ROW_VREGS, ROW_VREGS), ROW_VREGS)

    def gather(base, xt):
        for mi in range(MOE_ROWS):
            xt[pl.ds(mi, ROW_VREGS, stride=MOE_STRIDE), :] = x_ref[0, slab(idx_ref[0, 0, base + mi]), :]

    def experts():
        xs = [jnp.concatenate([xt[k * MOE_STRIDE:k * MOE_STRIDE + MOE_ROWS, :] for k in range(ROW_VREGS)],
                              axis=-1).astype(BF16) for xt in xt_s]
        gates = [jnp.dot(xs[g], w_refs[g][0][0], preferred_element_type=F32) for g in range(MOE_GROUP)]
        ups = [jnp.dot(xs[g], w_refs[g][1][0], preferred_element_type=F32) for g in range(MOE_GROUP)]
        acts = [(_silu(gates[g]) * ups[g]).astype(BF16) for g in range(MOE_GROUP)]
        outs = [jnp.dot(acts[g], w_refs[g][2][0], preferred_element_type=F32) for g in range(MOE_GROUP)]
        for g in range(MOE_GROUP):
            for k in range(ROW_VREGS):
                ot_s[g][k * MOE_STRIDE:k * MOE_STRIDE + MOE_ROWS, :] = outs[g][:, k * LANES:(k + 1) * LANES]

    def scatter(base, valid, ot):
        for m0 in range(0, MOE_ROWS, MOE_UNROLL):
            pending = []
            for mi in range(m0, m0 + MOE_UNROLL):
                ok = mi < valid
                rows = slab(jnp.where(ok, idx_ref[0, 0, base + mi], chunk))
                wgt = jnp.where(ok, wt_ref[0, 0, base + mi], 0.0)
                upd = y_ref[0, rows, :] + wgt * ot[pl.ds(mi, ROW_VREGS, stride=MOE_STRIDE), :]
                pending.append((rows, upd))
            for rows, upd in pending:
                y_ref[0, rows, :] = upd

    def tile_body(i, carry):
        base = [jnp.minimum(off[g] + i * MOE_ROWS, plan_rows) for g in range(MOE_GROUP)]
        base_s = [jnp.minimum(off_again[g] + i * MOE_ROWS, plan_rows) for g in range(MOE_GROUP)]
        valid = [n_rows[g] - i * MOE_ROWS for g in range(MOE_GROUP)]
        for g in range(MOE_GROUP):
            gather(base[g], xt_s[g])
        experts()
        for g in range(MOE_GROUP):
            scatter(base_s[g], valid[g], ot_s[g])
        return carry

    lax.fori_loop(0, n_tiles, tile_body, 0)


def _moe_call(h2_slab, cnt, off, idx_sorted, w_sorted, wg, wu, wd):
    nch, rows_in, _ = h2_slab.shape
    plan = idx_sorted.shape[2]
    rows_out = rows_in + SUBLANES * ROW_VREGS
    w_specs, w_args = [], []
    for g in range(MOE_GROUP):
        pick = lambda c, j, *_, g=g: (j * MOE_GROUP + g, 0, 0)
        w_specs += [pl.BlockSpec((1, D_MODEL, EXP_HIDDEN), pick), pl.BlockSpec((1, D_MODEL, EXP_HIDDEN), pick),
                    pl.BlockSpec((1, EXP_HIDDEN, D_MODEL), pick)]
        w_args += [wg, wu, wd]
    tile_buf = pltpu.VMEM((ROW_VREGS * MOE_STRIDE, LANES), F32)
    grid_spec = pltpu.PrefetchScalarGridSpec(
        num_scalar_prefetch=3,
        grid=(nch, N_EXPERTS // MOE_GROUP),
        in_specs=[
            pl.BlockSpec((1, 1, plan), lambda c, j, *_: (c, 0, 0), memory_space=pltpu.SMEM),
            pl.BlockSpec((1, 1, plan), lambda c, j, *_: (c, 0, 0), memory_space=pltpu.SMEM),
            pl.BlockSpec((1, rows_in, LANES), lambda c, j, *_: (c, 0, 0)),
        ] + w_specs,
        out_specs=pl.BlockSpec((1, rows_out, LANES), lambda c, j, *_: (c, 0, 0)),
        scratch_shapes=[tile_buf] * (2 * MOE_GROUP),
    )
    return pl.pallas_call(
        _moe_kernel,
        grid_spec=grid_spec,
        out_shape=jax.ShapeDtypeStruct((nch, rows_out, LANES), F32),
        compiler_params=_params(("arbitrary", "arbitrary")),
        name="moe_experts",
    )(cnt, off, off, idx_sorted, w_sorted, h2_slab, *w_args)


def _moe_plan(route, chunk):
    nch = route.shape[0] * TOKEN_TILE // chunk
    per = chunk * 2
    eid = route[:, ROUTE_E1:ROUTE_E2 + 1, :].astype(jnp.int32).reshape(nch, per)
    wts = route[:, ROUTE_W1:ROUTE_W2 + 1, :].reshape(nch, per)
    order = jnp.argsort(eid, axis=1, stable=True).astype(jnp.int32)
    token = (order // (2 * TOKEN_TILE)) * TOKEN_TILE + order % TOKEN_TILE
    idx_sorted = jnp.pad(token, ((0, 0), (0, MOE_ROWS)))
    w_sorted = jnp.pad(jnp.take_along_axis(wts, order, axis=1), ((0, 0), (0, MOE_ROWS)))
    cnt = jnp.sum((eid[..., None] == jnp.arange(N_EXPERTS, dtype=jnp.int32)).astype(jnp.int32), axis=1)
    off = jnp.cumsum(cnt, axis=1) - cnt
    return (cnt.reshape(-1).astype(jnp.int32), off.reshape(-1).astype(jnp.int32),
            idx_sorted.reshape(nch, 1, per + MOE_ROWS), w_sorted.reshape(nch, 1, per + MOE_ROWS))


def _ln2_kernel(x_ref, y_ref, mod_ref, g_ref, b_ref, o_ref):
    o_ref[0] = _moe_residual_ln(x_ref[0], y_ref, mod_ref[0][5:6], g_ref, b_ref)


def _ln2_call(x1, y_slab, mods, ln_g, ln_b, chunk):
    nb, nt, d = x1.shape
    tiles = nt // TOKEN_TILE
    ctx_tiles = CTX_LEN // TOKEN_TILE
    slab = _slab_index(tiles, chunk)
    return pl.pallas_call(
        _ln2_kernel,
        grid=(nb, tiles - ctx_tiles),
        in_specs=[pl.BlockSpec((1, TOKEN_TILE, d), lambda b, t: (b, t + ctx_tiles, 0)),
                  pl.BlockSpec((1, TOKEN_TILE * ROW_VREGS, LANES), lambda b, t: slab(b, t + ctx_tiles)),
                  pl.BlockSpec((1, SUBLANES, d), lambda b, t: (b, 0, 0)),
                  pl.BlockSpec((1, d), lambda b, t: (0, 0)), pl.BlockSpec((1, d), lambda b, t: (0, 0))],
        out_specs=pl.BlockSpec((1, TOKEN_TILE, d), lambda b, t: (b, t, 0)),
        out_shape=jax.ShapeDtypeStruct((nb, nt - CTX_LEN, d), F32),
        compiler_params=_params(("arbitrary", "arbitrary")),
        name="moe_residual_ln2",
    )(x1, y_slab, mods, ln_g, ln_b)


def _head_constants():
    w = A_HEADS * HEAD_DIM
    i = np.arange(w)
    ones_bd = (i[:, None] // HEAD_DIM == i[None, :] // HEAD_DIM).astype(np.float32)
    quarter = HEAD_DIM // 4
    rot = np.zeros((w, w), np.float32)
    first = (i % (2 * quarter)) < quarter
    rot[i[first] + quarter, i[first]] = -1.0
    rot[i[~first] - quarter, i[~first]] = 1.0
    return jnp.asarray(ones_bd, BF16), jnp.asarray(rot, BF16)


def _rope_tables(seq):
    t = jnp.arange(seq, dtype=jnp.int32)
    row = (t // GRID_W).astype(F32)
    col = (t % GRID_W).astype(F32)
    nf = HEAD_DIM // 4
    inv = ROPE_THETA ** (-jnp.arange(nf, dtype=F32) / nf)
    ang_r = row[:, None] * inv
    ang_c = col[:, None] * inv
    cos = jnp.concatenate([jnp.cos(ang_r), jnp.cos(ang_r), jnp.cos(ang_c), jnp.cos(ang_c)], axis=-1)
    sin = jnp.concatenate([jnp.sin(ang_r), jnp.sin(ang_r), jnp.sin(ang_c), jnp.sin(ang_c)], axis=-1)
    return jnp.tile(cos, (1, A_HEADS)), jnp.tile(sin, (1, A_HEADS))


def _in_weight(w_in):
    cols = []
    for _, parts, width, _ in IN_GROUPS:
        got = 0
        for p in parts:
            o, n = _IN_OFFS[p]
            cols.append(w_in[..., o:o + n])
            got += n
        if got < width:
            cols.append(jnp.zeros(w_in.shape[:-1] + (width - got,), w_in.dtype))
    return jnp.concatenate(cols, axis=-1).astype(BF16)


def _lane_row(vec, width, offset=0):
    return jnp.zeros((1, width), F32).at[0, offset:offset + vec.shape[0]].set(vec.astype(F32))


def kernel(x, c, ctx, c_ctx, w_ada, b_ada, w_in, a_q_gain, a_k_gain, b_rpb, c_conv, c_a_log, c_dt_bias, c_out_gain, d_gate_w, d_gate_b, d_out_gain, w_branch, w_out, ln1_g, ln1_b, ln2_g, ln2_b, w_router_g, b_router_g, w_router_e, b_router_e, w_up, w_gate, w_down):
    nb, seq, d = x.shape
    depth = w_ada.shape[0]
    nt = CTX_LEN + seq
    assert d == D_MODEL and ctx.shape[1] == CTX_LEN and nb + 1 <= ADA_ROWS
    assert seq % TOKEN_TILE == 0
    chunk = _moe_chunk(nb * nt)

    xa = jnp.concatenate([ctx, x], axis=1)
    cc = jnp.zeros((ADA_ROWS, d), F32).at[:nb].set(c).at[nb].set(c_ctx)
    mods = _ada_call(cc, w_ada, b_ada).reshape(depth, ADA_ROWS, 6, d)[:, :nb + 1]
    mods = jnp.pad(mods, ((0, 0), (0, 0), (0, SUBLANES - 6), (0, 0)))

    ones_bd, rot_m = _head_constants()
    cos_t, sin_t = _rope_tables(seq)
    gates_off = _IN_OFFS['gates'][0]

    w_mix = _in_weight(w_in)
    w_gates = w_in[..., gates_off:].astype(BF16)
    w_br, w_o = w_branch.astype(BF16), w_out.astype(BF16)
    w_eg, w_eu, w_ed = w_gate.astype(BF16), w_up.astype(BF16), w_down.astype(BF16)

    x1 = y_slab = None
    for l in range(depth):
        if l == 0:
            proj = _in_call(xa, mods[l], w_mix[l])
        else:
            xa, proj = _ln2_in_call(x1, y_slab, mods[l - 1], ln2_g[l - 1][None, :], ln2_b[l - 1][None, :],
                                    mods[l], w_mix[l], chunk)
        proj = dict(zip([g[0] for g in IN_GROUPS], proj))

        oa = _attn_a_call(proj['a_q'], proj['a_kv'], cos_t, sin_t,
                          jnp.tile(a_q_gain[l], A_HEADS)[None, :], jnp.tile(a_k_gain[l], A_KV_HEADS)[None, :],
                          ones_bd, rot_m)
        ob = _attn_b_call(proj['b_q'], proj['b_k'], proj['b_v'], _nb_bias_table(b_rpb[l]))
        conv_w = jnp.pad(c_conv[l], ((0, SUBLANES - c_conv.shape[1]), (0, 0)))
        par = jnp.concatenate([_lane_row(c_a_log[l].reshape(-1), LANES, C_A_LANE),
                               _lane_row(c_dt_bias[l].reshape(-1), LANES, C_A_LANE),
                               jnp.zeros((SUBLANES - 2, LANES), F32)], axis=0)
        oc = _gdn_call(proj['c_qkv'], proj['c_ba'], proj['c_g'], conv_w, par,
                       jnp.tile(c_out_gain[l], C_HEADS)[None, :], ones_bd)
        kw = D_HEADS * D_KDIM
        gw_blk = jnp.zeros((LANES, 2 * kw), F32)
        gw_blk = gw_blk.at[0:D_GATE_RANK, 0:kw].set(d_gate_w[l, 0])
        gw_blk = gw_blk.at[D_GATE_RANK:2 * D_GATE_RANK, kw:2 * kw].set(d_gate_w[l, 1])
        od = _gla_call(proj['d_qk'], proj['d_v'], proj['d_lr'], proj['d_g'], gw_blk,
                       d_gate_b[l].reshape(1, 2 * kw), jnp.tile(d_out_gain[l], D_HEADS)[None, :], ones_bd)

        wr = jnp.concatenate([w_router_g[l], jnp.transpose(w_router_e[l], (1, 0, 2)).reshape(d, N_EXPERTS)], axis=1)
        wr = jnp.pad(wr, ((0, 0), (0, LANES - wr.shape[1])))
        br = _lane_row(jnp.concatenate([b_router_g[l], b_router_e[l].reshape(-1)]), LANES)
        x1, h2, route = _merge_call(
            xa, mods[l], oa, ob, oc, od, w_gates[l], w_br[l], w_o[l], ln1_g[l][None, :], ln1_b[l][None, :], wr, br)

        cnt, off, idx_sorted, w_sorted = _moe_plan(route, chunk)
        h2_slab = h2.reshape((nb * nt) // chunk, chunk * ROW_VREGS, LANES)
        y_slab = _moe_call(h2_slab, cnt, off, idx_sorted, w_sorted, w_eg[l], w_eu[l], w_ed[l])

    last = depth - 1
    return _ln2_call(x1, y_slab, mods[last], ln2_g[last][None, :], ln2_b[last][None, :], chunk)
```

```python
import functools
import math

import numpy as np
import jax
import jax.numpy as jnp
from jax import lax
from jax.experimental import pallas as pl
from jax.experimental.pallas import tpu as pltpu

F32 = jnp.float32
BF16 = jnp.bfloat16

D_MODEL = 1024
DEPTH = 4
GRID_W = 64
CTX_LEN = 256
HEAD_DIM = 64
A_HEADS = 4
A_KV_HEADS = 2
ROPE_THETA = 10000.0
B_HEADS = 4
WIN_R = 8
WIN_C = 16
C_HEADS = 4
D_HEADS = 4
D_KDIM = 32
D_GATE_RANK = 16
GLA_TAU = 16.0
CHUNK = 64
N_BRANCH = 4
N_GROUPS = 4
EXP_PER_GROUP = 8
N_EXPERTS = N_GROUPS * EXP_PER_GROUP
EXP_HIDDEN = 512
EPS = 1e-6
DN_ALPHA = (2.0 * DEPTH) ** 0.25
NEG_BIG = -1e30

LANES = 128
SUBLANES = 8
TOKEN_TILE = 256
VMEM_LIMIT = 56 * 1024 * 1024

_IN_OFFS = {}
_off = 0
for _n, _w in (('a_q', 256), ('a_k', 128), ('a_v', 128), ('b_q', 256), ('b_k', 256), ('b_v', 256),
               ('c_qkv', 768), ('c_beta', 8), ('c_a', 8), ('c_g', 256), ('d_q', 128), ('d_k', 128),
               ('d_v', 256), ('d_lr', 32), ('d_g', 256), ('gates', 4096)):
    _IN_OFFS[_n] = (_off, _w)
    _off += _w
IN_GROUPS = (
    ('a_q', ('a_q',), 256, BF16),
    ('a_kv', ('a_k', 'a_v'), 256, BF16),
    ('b_q', ('b_q',), 256, BF16),
    ('b_k', ('b_k',), 256, BF16),
    ('b_v', ('b_v',), 256, BF16),
    ('c_qkv', ('c_qkv',), 768, BF16),
    ('c_ba', ('c_beta', 'c_a'), 128, F32),
    ('c_g', ('c_g',), 256, F32),
    ('d_qk', ('d_q', 'd_k'), 256, F32),
    ('d_v', ('d_v',), 256, F32),
    ('d_lr', ('d_lr',), 128, F32),
    ('d_g', ('d_g',), 256, F32),
)
IN_TOTAL = sum(g[2] for g in IN_GROUPS)

MOE_CHUNK = 2048
MOE_ROWS = 160
MOE_STRIDE = MOE_ROWS + SUBLANES
ROW_VREGS = D_MODEL // LANES


def _dot(a, b):
    return jnp.dot(a.astype(BF16), b.astype(BF16), preferred_element_type=F32)


def _dot_nt(a, b):
    return lax.dot_general(a.astype(BF16), b.astype(BF16), (((1,), (1,)), ((), ())),
                           preferred_element_type=F32)


def _dot_tn(a, b):
    return lax.dot_general(a.astype(BF16), b.astype(BF16), (((0,), (0,)), ((), ())),
                           preferred_element_type=F32)


def _split(x):
    hi = x.astype(BF16)
    lo = (x - hi.astype(F32)).astype(BF16)
    return hi, lo


def _dot3(a, b):
    ah, al = _split(a)
    bh, bl = _split(b)
    return (jnp.dot(ah, bh, preferred_element_type=F32) + jnp.dot(al, bh, preferred_element_type=F32)
            + jnp.dot(ah, bl, preferred_element_type=F32))


def _dot_sel(a, m):
    ah, al = _split(a)
    return jnp.dot(ah, m, preferred_element_type=F32) + jnp.dot(al, m, preferred_element_type=F32)


def _silu(x):
    return x * jax.nn.sigmoid(x)


def _layer_norm(r, g, b):
    mu = jnp.mean(r, axis=-1, keepdims=True)
    d = r - mu
    var = jnp.mean(d * d, axis=-1, keepdims=True)
    return d * lax.rsqrt(var + EPS) * g + b


def _params(sem):
    return pltpu.CompilerParams(dimension_semantics=sem, vmem_limit_bytes=VMEM_LIMIT)


ADA_ROWS = 24
ADA_TILE = 1536


def _ada_kernel(cc_ref, w_ref, b_ref, o_ref):
    s = _silu(cc_ref[...])
    o_ref[0] = _dot3(s, w_ref[0]) + b_ref[0]


def _ada_call(cc, w_ada, b_ada):
    depth = w_ada.shape[0]
    n = w_ada.shape[2]
    return pl.pallas_call(
        _ada_kernel,
        grid=(depth, n // ADA_TILE),
        in_specs=[
            pl.BlockSpec((ADA_ROWS, D_MODEL), lambda l, j: (0, 0)),
            pl.BlockSpec((1, D_MODEL, ADA_TILE), lambda l, j: (l, 0, j)),
            pl.BlockSpec((1, 1, ADA_TILE), lambda l, j: (l, 0, j)),
        ],
        out_specs=pl.BlockSpec((1, ADA_ROWS, ADA_TILE), lambda l, j: (l, 0, j)),
        out_shape=jax.ShapeDtypeStruct((depth, ADA_ROWS, n), F32),
        compiler_params=_params(("arbitrary", "arbitrary")),
        name="ada_mod",
    )(cc, w_ada, b_ada.reshape(depth, 1, n))


def _mod_index(nb):
    return lambda b, t: (jnp.where(t == 0, nb, b), 0, 0)


def _project_in(x, mod, w_ref, out_refs):
    h = (x * (1.0 + mod[1:2]) + mod[0:1]).astype(BF16)
    off = 0
    for (name, _, width, dt), o_ref in zip(IN_GROUPS, out_refs):
        o_ref[0] = jnp.dot(h, w_ref[:, off:off + width], preferred_element_type=F32).astype(dt)
        off += width


def _in_kernel(x_ref, mod_ref, w_ref, *out_refs):
    _project_in(x_ref[0], mod_ref[0], w_ref, out_refs)


def _moe_residual_ln(x1, y_ref, gate, g_ref, b_ref):
    y = jnp.concatenate([y_ref[0, pl.ds(j, TOKEN_TILE, stride=ROW_VREGS), :] for j in range(ROW_VREGS)], axis=-1)
    return _layer_norm(DN_ALPHA * x1 + gate * y, g_ref[...], b_ref[...])


def _ln2_in_kernel(x1_ref, y_ref, mod_prev_ref, g_ref, b_ref, mod_ref, w_ref, xa_ref, *out_refs):
    xa = _moe_residual_ln(x1_ref[0], y_ref, mod_prev_ref[0][5:6], g_ref, b_ref)
    xa_ref[0] = xa
    _project_in(xa, mod_ref[0], w_ref, out_refs)


def _slab_index(tiles, chunk):
    per_chunk = chunk // TOKEN_TILE
    return lambda b, t: ((b * tiles + t) // per_chunk, (b * tiles + t) % per_chunk, 0)


def _ln2_in_call(x1, y_slab, mods_prev, ln_g, ln_b, mods, w_cat, chunk):
    nb, nt, d = x1.shape
    tiles = nt // TOKEN_TILE
    tok = lambda b, t: (b, t, 0)
    row = lambda b, t: (0, 0)
    outs = pl.pallas_call(
        _ln2_in_kernel,
        grid=(nb, tiles),
        in_specs=[
            pl.BlockSpec((1, TOKEN_TILE, d), tok),
            pl.BlockSpec((1, TOKEN_TILE * ROW_VREGS, LANES), _slab_index(tiles, chunk)),
            pl.BlockSpec((1, SUBLANES, d), _mod_index(nb)),
            pl.BlockSpec((1, d), row), pl.BlockSpec((1, d), row),
            pl.BlockSpec((1, SUBLANES, d), _mod_index(nb)),
            pl.BlockSpec((d, IN_TOTAL), row),
        ],
        out_specs=[pl.BlockSpec((1, TOKEN_TILE, d), tok)]
        + [pl.BlockSpec((1, TOKEN_TILE, g[2]), tok) for g in IN_GROUPS],
        out_shape=[jax.ShapeDtypeStruct((nb, nt, d), F32)]
        + [jax.ShapeDtypeStruct((nb, nt, g[2]), g[3]) for g in IN_GROUPS],
        compiler_params=_params(("arbitrary", "arbitrary")),
        name="ln2_in_proj",
    )(x1, y_slab, mods_prev, ln_g, ln_b, mods, w_cat)
    return outs[0], outs[1:]


def _in_call(xa, mods, w_cat):
    nb, nt, _ = xa.shape
    tiles = nt // TOKEN_TILE
    return pl.pallas_call(
        _in_kernel,
        grid=(nb, tiles),
        in_specs=[
            pl.BlockSpec((1, TOKEN_TILE, D_MODEL), lambda b, t: (b, t, 0)),
            pl.BlockSpec((1, SUBLANES, D_MODEL), _mod_index(nb)),
            pl.BlockSpec((D_MODEL, IN_TOTAL), lambda b, t: (0, 0)),
        ],
        out_specs=[pl.BlockSpec((1, TOKEN_TILE, g[2]), lambda b, t: (b, t, 0)) for g in IN_GROUPS],
        out_shape=[jax.ShapeDtypeStruct((nb, nt, g[2]), g[3]) for g in IN_GROUPS],
        compiler_params=_params(("arbitrary", "arbitrary")),
        name="in_proj",
    )(xa, mods, w_cat)


def _head_rms(x, ones_bd, gain):
    ss = _dot_sel(x * x, ones_bd)
    return x * lax.rsqrt(ss * (1.0 / HEAD_DIM) + EPS) * gain


def _rope(x, rot, cos, sin):
    return x * cos + _dot_sel(x, rot) * sin


def _attn_a_kernel(q_ref, kv_ref, cos_ref, sin_ref, qg_ref, kg_ref, bd_ref, rot_ref, o_ref, kp_ref, vt_ref):
    t = pl.program_id(1)
    kvw = A_KV_HEADS * HEAD_DIM
    rep = A_HEADS // A_KV_HEADS
    assert rep * HEAD_DIM == kvw == LANES
    n_tiles = kv_ref.shape[1] // TOKEN_TILE
    scale = HEAD_DIM ** -0.5
    lane = lax.broadcasted_iota(jnp.int32, (TOKEN_TILE, kvw), 1)
    first_half = lane < HEAD_DIM

    @pl.when(t == 0)
    def _prep_keys():
        bd = bd_ref[0:kvw, 0:kvw]
        rot = rot_ref[0:kvw, 0:kvw]
        kg = kg_ref[...]

        def put(i, k_rows, k):
            swapped = pltpu.roll(k, HEAD_DIM, axis=1)
            kp_ref[0, k_rows, :] = jnp.where(first_half, k, swapped).astype(BF16)
            kp_ref[1, k_rows, :] = jnp.where(first_half, swapped, k).astype(BF16)
            vt_ref[i] = kv_ref[0, k_rows, kvw:2 * kvw].astype(F32).T.astype(BF16)

        put(0, pl.ds(0, TOKEN_TILE), _head_rms(kv_ref[0, 0:TOKEN_TILE, 0:kvw].astype(F32), bd, kg))

        def body(i, carry):
            r0 = pl.multiple_of(i * TOKEN_TILE, TOKEN_TILE)
            rows = pl.ds(CTX_LEN + r0, TOKEN_TILE)
            kn = _head_rms(kv_ref[0, rows, 0:kvw].astype(F32), bd, kg)
            put(i + CTX_LEN // TOKEN_TILE, rows,
                _rope(kn, rot, cos_ref[pl.ds(r0, TOKEN_TILE), 0:kvw], sin_ref[pl.ds(r0, TOKEN_TILE), 0:kvw]))
            return carry

        lax.fori_loop(0, n_tiles - CTX_LEN // TOKEN_TILE, body, 0)

    qn = _head_rms(q_ref[0].astype(F32), bd_ref[...], qg_ref[...])

    def attend(qh, key_tiles):
        nk = key_tiles * TOKEN_TILE
        heads = [(g, r) for g in range(A_KV_HEADS) for r in range(rep)]
        qms = [jnp.where(first_half == (r == 0), qh[:, g * kvw:(g + 1) * kvw] * scale, 0.0).astype(BF16)
               for g, r in heads]
        ss = [_dot_nt(kp_ref[g, 0:nk, :], qms[i]) for i, (g, r) in enumerate(heads)]
        es = [jnp.exp(s - jnp.max(s, axis=0, keepdims=True)) for s in ss]
        ls = [jnp.sum(e, axis=0, keepdims=True) for e in es]
        ebs = [e.astype(BF16) for e in es]
        outs = []
        for i, (g, r) in enumerate(heads):
            o = None
            for k in range(key_tiles):
                part = jnp.dot(vt_ref[k, g * HEAD_DIM:(g + 1) * HEAD_DIM, :],
                               ebs[i][k * TOKEN_TILE:(k + 1) * TOKEN_TILE, :], preferred_element_type=F32)
                o = part if o is None else o + part
            outs.append(o / ls[i])
        return jnp.concatenate(outs, axis=0).T

    @pl.when(t == 0)
    def _ctx_queries():
        o_ref[0] = attend(qn, CTX_LEN // TOKEN_TILE).astype(o_ref.dtype)

    @pl.when(t > 0)
    def _latent_queries():
        r0 = pl.multiple_of((t - 1) * TOKEN_TILE, TOKEN_TILE)
        qr = _rope(qn, rot_ref[...], cos_ref[pl.ds(r0, TOKEN_TILE), :], sin_ref[pl.ds(r0, TOKEN_TILE), :])
        o_ref[0] = attend(qr, n_tiles).astype(o_ref.dtype)


def _attn_a_call(a_q, a_kv, cos_t, sin_t, q_gain, k_gain, ones_bd, rot_m):
    nb, nt, _ = a_q.shape
    tiles = nt // TOKEN_TILE
    seq = nt - CTX_LEN
    qw = A_HEADS * HEAD_DIM
    return pl.pallas_call(
        _attn_a_kernel,
        grid=(nb, tiles),
        in_specs=[
            pl.BlockSpec((1, TOKEN_TILE, qw), lambda b, t: (b, t, 0)),
            pl.BlockSpec((1, nt, qw), lambda b, t: (b, 0, 0)),
            pl.BlockSpec((seq, qw), lambda b, t: (0, 0)),
            pl.BlockSpec((seq, qw), lambda b, t: (0, 0)),
            pl.BlockSpec((1, qw), lambda b, t: (0, 0)),
            pl.BlockSpec((1, A_KV_HEADS * HEAD_DIM), lambda b, t: (0, 0)),
            pl.BlockSpec((qw, qw), lambda b, t: (0, 0)),
            pl.BlockSpec((qw, qw), lambda b, t: (0, 0)),
        ],
        out_specs=pl.BlockSpec((1, TOKEN_TILE, qw), lambda b, t: (b, t, 0)),
        out_shape=jax.ShapeDtypeStruct((nb, nt, qw), BF16),
        scratch_shapes=[pltpu.VMEM((A_KV_HEADS, nt, A_KV_HEADS * HEAD_DIM), BF16),
                        pltpu.VMEM((tiles, A_KV_HEADS * HEAD_DIM, TOKEN_TILE), BF16)],
        compiler_params=_params(("arbitrary", "arbitrary")),
        name="mixer_a_gqa",
    )(a_q, a_kv, cos_t, sin_t, q_gain, k_gain, ones_bd, rot_m)


NB_QROWS = TOKEN_TILE // GRID_W
NB_KROWS = 12
NB_INVALID = 2 * WIN_R - 1


def _attn_b_kernel(q_ref, k_ref, v_ref, bt_ref, o_ref):
    t = pl.program_id(1)
    scale = HEAD_DIM ** -0.5
    rows = (k_ref.shape[1] - CTX_LEN) // GRID_W
    wr = min(WIN_R, rows)

    def softmax_pv(parts):
        m = None
        for s, _ in parts:
            mi = jnp.max(s, axis=-1, keepdims=True)
            m = mi if m is None else jnp.maximum(m, mi)
        acc, l = None, None
        for s, v in parts:
            e = jnp.exp(s - m)
            li = jnp.sum(e, axis=-1, keepdims=True)
            oi = jnp.dot(e.astype(BF16), v, preferred_element_type=F32)
            acc = oi if acc is None else acc + oi
            l = li if l is None else l + li
        return acc / l

    @pl.when(t == 0)
    def _ctx_queries():
        outs = []
        for h in range(B_HEADS):
            sl = slice(h * HEAD_DIM, (h + 1) * HEAD_DIM)
            qq = (q_ref[0, :, sl].astype(F32) * scale).astype(BF16)
            s = _dot_nt(qq, k_ref[0, 0:CTX_LEN, sl])
            outs.append(softmax_pv([(s, v_ref[0, 0:CTX_LEN, sl])]))
        o_ref[0] = jnp.concatenate(outs, axis=-1).astype(o_ref.dtype)

    @pl.when(t > 0)
    def _latent_queries():
        r0 = (t - 1) * NB_QROWS
        start = jnp.clip(r0 - wr // 2, 0, rows - NB_KROWS)
        k0 = pl.multiple_of(CTX_LEN + start * GRID_W, GRID_W)
        nk = NB_KROWS * GRID_W
        lane = lax.broadcasted_iota(jnp.int32, (GRID_W, 2 * GRID_W), 1)
        left = lane < GRID_W
        slots = []
        for i in range(NB_QROWS):
            r = r0 + i
            rs = jnp.clip(r - wr // 2, 0, rows - wr)
            row_slots = []
            for j in range(NB_KROWS):
                kr = start + j
                ok = jnp.logical_and(kr >= rs, kr < rs + wr)
                row_slots.append(jnp.where(ok, kr - r + WIN_R - 1, NB_INVALID))
            slots.append(row_slots)
        sls = [slice(h * HEAD_DIM, (h + 1) * HEAD_DIM) for h in range(B_HEADS)]
        qqs = [(q_ref[0, :, sl].astype(F32) * scale).astype(BF16) for sl in sls]
        s_locs = [_dot_nt(qqs[h], k_ref[0, pl.ds(k0, nk), sls[h]]) for h in range(B_HEADS)]
        s_ctxs = [_dot_nt(qqs[h], k_ref[0, 0:CTX_LEN, sls[h]]) for h in range(B_HEADS)]
        probs = []
        for h in range(B_HEADS):
            bias_rows = []
            for i in range(NB_QROWS):
                tiles = []
                for jp in range(NB_KROWS // 2):
                    b0 = bt_ref[h, slots[i][2 * jp]]
                    b1 = bt_ref[h, slots[i][2 * jp + 1]]
                    tiles.append(jnp.where(left, b0, b1))
                bias_rows.append(jnp.concatenate(tiles, axis=-1))
            s_loc = s_locs[h] + jnp.concatenate(bias_rows, axis=0)
            m = jnp.maximum(jnp.max(s_loc, axis=-1, keepdims=True), jnp.max(s_ctxs[h], axis=-1, keepdims=True))
            e_loc = jnp.exp(s_loc - m)
            e_ctx = jnp.exp(s_ctxs[h] - m)
            probs.append((e_loc.astype(BF16), e_ctx.astype(BF16),
                          jnp.sum(e_loc, axis=-1, keepdims=True) + jnp.sum(e_ctx, axis=-1, keepdims=True)))
        outs = [(jnp.dot(probs[h][0], v_ref[0, pl.ds(k0, nk), sls[h]], preferred_element_type=F32)
                 + jnp.dot(probs[h][1], v_ref[0, 0:CTX_LEN, sls[h]], preferred_element_type=F32)) / probs[h][2]
                for h in range(B_HEADS)]
        o_ref[0] = jnp.concatenate(outs, axis=-1).astype(o_ref.dtype)


def _attn_b_call(b_q, b_k, b_v, bias_tab):
    nb, nt, w = b_q.shape
    tiles = nt // TOKEN_TILE
    return pl.pallas_call(
        _attn_b_kernel,
        grid=(nb, tiles),
        in_specs=[
            pl.BlockSpec((1, TOKEN_TILE, w), lambda b, t: (b, t, 0)),
            pl.BlockSpec((1, nt, w), lambda b, t: (b, 0, 0)),
            pl.BlockSpec((1, nt, w), lambda b, t: (b, 0, 0)),
            pl.BlockSpec(bias_tab.shape, lambda b, t: (0, 0, 0, 0)),
        ],
        out_specs=pl.BlockSpec((1, TOKEN_TILE, w), lambda b, t: (b, t, 0)),
        out_shape=jax.ShapeDtypeStruct((nb, nt, w), BF16),
        compiler_params=_params(("arbitrary", "arbitrary")),
        name="mixer_b_neighbourhood",
    )(b_q, b_k, b_v, bias_tab)


def _nb_bias_table(rpb):
    cols = jnp.arange(GRID_W, dtype=jnp.int32)
    col_start = jnp.clip(cols - WIN_C // 2, 0, GRID_W - WIN_C)
    col_ok = (cols[None, :] >= col_start[:, None]) & (cols[None, :] < col_start[:, None] + WIN_C)
    dc_idx = jnp.clip(cols[None, :] - cols[:, None] + WIN_C - 1, 0, 2 * WIN_C - 2)
    tab = rpb.astype(F32)[..., dc_idx]
    tab = jnp.where(col_ok, tab, NEG_BIG)
    tab = jnp.concatenate([tab, jnp.full_like(tab[..., :1, :, :], NEG_BIG)], axis=-3)
    return jnp.concatenate([tab, tab], axis=-1)


def _order_masks():
    i = lax.broadcasted_iota(jnp.int32, (CHUNK, CHUNK), 0)
    j = lax.broadcasted_iota(jnp.int32, (CHUNK, CHUNK), 1)
    return ((j <= i, j < i), (j >= i, j > i))


def _scan_chunk(step, direction, n_ctx_chunks, n_chunks):
    if direction == 0:
        return step
    return jnp.where(step < n_ctx_chunks, n_ctx_chunks - 1 - step, n_chunks + n_ctx_chunks - 1 - step)


def _sel_dot(m, a):
    ah, al = _split(a)
    return jnp.dot(m, ah, preferred_element_type=F32) + jnp.dot(m, al, preferred_element_type=F32)


def _gated_out_tiles(part_refs, gate_ref, gain_ref, bd_ref, o_ref):
    nt = part_refs[0].shape[0]
    bd = bd_ref[...]
    gain = gain_ref[...]

    def body(i, carry):
        r0 = pl.multiple_of(i * TOKEN_TILE, TOKEN_TILE)
        o = sum(p[pl.ds(r0, TOKEN_TILE), :] for p in part_refs)
        y = _head_rms(o, bd, gain) * _silu(gate_ref[0, pl.ds(r0, TOKEN_TILE), :])
        o_ref[0, pl.ds(r0, TOKEN_TILE), :] = y.astype(o_ref.dtype)
        return carry

    lax.fori_loop(0, nt // TOKEN_TILE, body, 0)


C_BETA_LANE = 0
C_A_LANE = 8
C_T_ROWS = 16


SOLVE_BLOCK = 16
GDN_LOCAL_CHUNKS = 4


def _unit_lower_solve(lmats, rhss, same_block, eye):
    n = range(len(lmats))
    lds = [jnp.where(same_block, l, 0.0) for l in lmats]
    ts = [eye - ld for ld in lds]
    ps = lds
    span = 2
    while span < SOLVE_BLOCK:
        ps = [_dot(p, p) for p in ps]
        ts = [ts[i] + _dot(ts[i], ps[i]) for i in n]
        span *= 2
    width = rhss[0].shape[1]
    mzs = [_dot(ts[i], jnp.concatenate([rhss[i], lmats[i] - lds[i]], axis=-1)) for i in n]
    mmzs = [_dot(mz[:, width:], mz) for mz in mzs]
    zs = [mzs[i][:, 0:width] - mmzs[i][:, 0:width] for i in n]
    ps = [mmz[:, width:] for mmz in mmzs]
    span = 2
    while span < CHUNK // SOLVE_BLOCK:
        zs = [zs[i] + _dot(ps[i], zs[i]) for i in n]
        span *= 2
        if span < CHUNK // SOLVE_BLOCK:
            ps = [_dot(p, p) for p in ps]
    return zs


def _gdn_kernel(qkv_ref, ba_ref, gate_ref, conv_ref, par_ref, gain_ref, bd_ref, o_ref,
                q_s, k_s, v_s, bl_s, g_s, gt_s, a12_s, b2_s, egl_s, o_s, st_s):
    nt = qkv_ref.shape[1]
    n_tiles = nt // TOKEN_TILE
    n_chunks = nt // CHUNK
    n_ctx_chunks = CTX_LEN // CHUNK
    w = C_HEADS * HEAD_DIM
    pack = 2 * SUBLANES
    bd = bd_ref[...]
    lane = lax.broadcasted_iota(jnp.int32, (TOKEN_TILE, LANES), 1)
    lane_c = lax.broadcasted_iota(jnp.int32, (CHUNK, LANES), 1)
    row = lax.broadcasted_iota(jnp.int32, (TOKEN_TILE, 1), 0)
    neg_rate = -jnp.exp(par_ref[0:1, :])
    dt_bias = par_ref[1:2, :]
    w_prev, w_mid, w_next = conv_ref[0:1, :], conv_ref[1:2, :], conv_ref[2:3, :]
    masks = _order_masks()
    mask_bf = [jnp.where(m[0], 1.0, 0.0).astype(BF16) for m in masks]

    def prep(i, carry):
        r0 = pl.multiple_of(i * TOKEN_TILE, TOKEN_TILE)
        x = qkv_ref[0, pl.ds(r0, TOKEN_TILE), :].astype(F32)
        before = qkv_ref[0, pl.ds(pl.multiple_of(jnp.maximum(r0 - pack, 0), pack), pack), :].astype(F32)
        after = qkv_ref[0, pl.ds(pl.multiple_of(jnp.minimum(r0 + TOKEN_TILE, nt - pack), pack), pack), :].astype(F32)
        first_of_seq = jnp.logical_or(i == 0, i == CTX_LEN // TOKEN_TILE)
        last_of_seq = jnp.logical_or(i == CTX_LEN // TOKEN_TILE - 1, i == n_tiles - 1)
        edge_prev = jnp.where(first_of_seq, 0.0, before[pack - 1:pack, :])
        edge_next = jnp.where(last_of_seq, 0.0, after[0:1, :])
        x_prev = jnp.where(row == 0, edge_prev, pltpu.roll(x, 1, axis=0))
        x_next = jnp.where(row == TOKEN_TILE - 1, edge_next, pltpu.roll(x, TOKEN_TILE - 1, axis=0))
        y = _silu(x_prev * w_prev + x * w_mid + x_next * w_next)
        q, k, v = y[:, 0:w], y[:, w:2 * w], y[:, 2 * w:3 * w]
        q_s[pl.ds(r0, TOKEN_TILE), :] = q * lax.rsqrt(_dot_sel(q * q, bd) + EPS) * (HEAD_DIM ** -0.5)
        k_s[pl.ds(r0, TOKEN_TILE), :] = k * lax.rsqrt(_dot_sel(k * k, bd) + EPS)
        v_s[pl.ds(r0, TOKEN_TILE), :] = v
        ba = ba_ref[0, pl.ds(r0, TOKEN_TILE), :]
        sp = ba + dt_bias
        softplus = jnp.maximum(sp, 0.0) + jnp.log1p(jnp.exp(-jnp.abs(sp)))
        bl = jnp.where(lane < C_A_LANE, jax.nn.sigmoid(ba), neg_rate * softplus)
        bl_s[pl.ds(r0, TOKEN_TILE), :] = bl
        for c in range(TOKEN_TILE // CHUNK):
            blc = bl[c * CHUNK:(c + 1) * CHUNK, :]
            g = jnp.where(lane_c < C_A_LANE + C_HEADS, _sel_dot(mask_bf[0], blc), _sel_dot(mask_bf[1], blc))
            g_s[pl.ds(r0 + c * CHUNK, CHUNK), :] = g
            gt_s[i * (TOKEN_TILE // CHUNK) + c] = g.T[0:C_T_ROWS, :]
        return carry

    lax.fori_loop(0, n_tiles, prep, 0)

    ri = lax.broadcasted_iota(jnp.int32, (CHUNK, CHUNK), 0)
    ci = lax.broadcasted_iota(jnp.int32, (CHUNK, CHUNK), 1)
    same_block = (ri // SOLVE_BLOCK) == (ci // SOLVE_BLOCK)
    eye = jnp.where(ri == ci, 1.0, 0.0)

    src = lax.broadcasted_iota(jnp.int32, (LANES, w), 0)
    dst_head = lax.broadcasted_iota(jnp.int32, (LANES, w), 1) // HEAD_DIM
    spread_g = [jnp.where(src == C_A_LANE + d * C_HEADS + dst_head, 1.0, 0.0).astype(BF16) for d in range(2)]
    spread_b = [jnp.where(src == C_BETA_LANE + d * C_HEADS + dst_head, 1.0, 0.0).astype(BF16) for d in range(2)]
    heads = [slice(h * HEAD_DIM, (h + 1) * HEAD_DIM) for h in range(C_HEADS)]
    lane_head = lax.broadcasted_iota(jnp.int32, (CHUNK, w), 1) // HEAD_DIM

    def local(it, carry):
        chunks = [it * GDN_LOCAL_CHUNKS + t for t in range(GDN_LOCAL_CHUNKS)]
        rows = [pl.ds(pl.multiple_of(c * CHUNK, CHUNK), CHUNK) for c in chunks]
        td = [(t, d) for t in range(GDN_LOCAL_CHUNKS) for d in range(2)]
        qcs, kcs, vcs = [q_s[r, :] for r in rows], [k_s[r, :] for r in rows], [v_s[r, :] for r in rows]
        g_alls = [_dot_sel(g_s[rows[t], :], spread_g[d]) for t, d in td]
        b_alls = [_dot_sel(bl_s[rows[t], :], spread_b[d]) for t, d in td]
        qks = [[_dot_nt(jnp.where(lane_head == h, qcs[t], 0.0), kcs[t]) for h in range(C_HEADS)]
               for t in range(GDN_LOCAL_CHUNKS)]
        pre = []
        for i, (t, d) in enumerate(td):
            last = CHUNK - 1 if d == 0 else 0
            e_g = jnp.exp(g_alls[i])
            g_last = g_alls[i][last:last + 1, :]
            kb = kcs[t] * b_alls[i]
            kd_t = (kcs[t] * jnp.exp(g_last - g_alls[i])).T.astype(BF16)
            egl_s[d, chunks[t]] = jnp.broadcast_to(jnp.exp(g_last), (SUBLANES, w))
            pre.append((kb, kb * e_g, vcs[t] * b_alls[i], qcs[t] * e_g, kd_t))
        kbk = [[_dot_nt(jnp.where(lane_head == h, pre[i][0], 0.0), kcs[t]) for h in range(C_HEADS)]
               for i, (t, d) in enumerate(td)]
        lmats, rhss, keep = [], [], []
        for i, (t, d) in enumerate(td):
            incl, strict = masks[d]
            g_rows = gt_s[chunks[t]]
            for h, sl in enumerate(heads):
                la = C_A_LANE + d * C_HEADS + h
                decay = jnp.exp(jnp.where(incl, g_alls[i][:, sl] - g_rows[la:la + 1, :], NEG_BIG))
                lmats.append(jnp.where(strict, kbk[i][h] * decay, 0.0))
                rhss.append(jnp.concatenate([pre[i][1][:, sl], pre[i][2][:, sl]], axis=-1))
                keep.append(((qks[t][h] * decay).astype(BF16), pre[i][3][:, sl], pre[i][4][sl, :]))
        sols = [s.astype(BF16) for s in _unit_lower_solve(lmats, rhss, same_block, eye)]
        qwu = [jnp.dot(keep[i][0], sols[i], preferred_element_type=F32) for i in range(len(sols))]
        kwu = [jnp.dot(keep[i][2], sols[i], preferred_element_type=F32) for i in range(len(sols))]
        i = 0
        for t in range(GDN_LOCAL_CHUNKS):
            c = it * GDN_LOCAL_CHUNKS + t
            r0 = pl.multiple_of(c * CHUNK, CHUNK)
            o_const = None
            for d in range(2):
                u = range(i, i + C_HEADS)
                a12_s[d, c, 0:CHUNK, :] = jnp.concatenate(
                    [keep[j][1] - qwu[j][:, 0:HEAD_DIM] for j in u], axis=-1).astype(BF16)
                a12_s[d, c, CHUNK:2 * CHUNK, :] = jnp.concatenate(
                    [-kwu[j][:, 0:HEAD_DIM] for j in u], axis=-1).astype(BF16)
                b2_s[d, c] = jnp.concatenate([kwu[j][:, HEAD_DIM:] for j in u], axis=-1)
                part = jnp.concatenate([qwu[j][:, HEAD_DIM:] for j in u], axis=-1)
                o_const = part if o_const is None else o_const + part
                i += C_HEADS
            o_s[pl.ds(r0, CHUNK), :] = o_const
        return carry

    lax.fori_loop(0, n_chunks // GDN_LOCAL_CHUNKS, local, 0)

    st_s[...] = jnp.zeros(st_s.shape, F32)

    def scan(step, carry):
        units = [(d, h) for d in range(2) for h in range(C_HEADS)]
        cs = [_scan_chunk(step, d, n_ctx_chunks, n_chunks) for d in range(2)]
        a12 = [a12_s[d, cs[d]] for d in range(2)]
        b2 = [b2_s[d, cs[d]] for d in range(2)]
        egl = [egl_s[d, cs[d]] for d in range(2)]
        ss = [st_s[d, h] for d, h in units]
        rs = [jnp.dot(a12[d][:, heads[h]], ss[i].astype(BF16), preferred_element_type=F32)
              for i, (d, h) in enumerate(units)]
        for i, (d, h) in enumerate(units):
            st_s[d, h] = ss[i] * egl[d][0:1, heads[h]] + rs[i][CHUNK:, :] + b2[d][:, heads[h]]
        for d in range(2):
            r0 = pl.multiple_of(cs[d] * CHUNK, CHUNK)
            o = jnp.concatenate([rs[d * C_HEADS + h][0:CHUNK, :] for h in range(C_HEADS)], axis=-1)
            o_s[pl.ds(r0, CHUNK), :] = o_s[pl.ds(r0, CHUNK), :] + o
        return carry

    lax.fori_loop(0, n_chunks, scan, 0)
    _gated_out_tiles((o_s,), gate_ref, gain_ref, bd_ref, o_ref)


def _gdn_call(c_qkv, c_ba, c_g, conv_w, par, gain, ones_bd):
    nb, nt, _ = c_qkv.shape
    w = C_HEADS * HEAD_DIM
    nc = nt // CHUNK
    return pl.pallas_call(
        _gdn_kernel,
        grid=(nb,),
        in_specs=[
            pl.BlockSpec((1, nt, 3 * w), lambda b: (b, 0, 0)),
            pl.BlockSpec((1, nt, LANES), lambda b: (b, 0, 0)),
            pl.BlockSpec((1, nt, w), lambda b: (b, 0, 0)),
            pl.BlockSpec((SUBLANES, 3 * w), lambda b: (0, 0)),
            pl.BlockSpec((SUBLANES, LANES), lambda b: (0, 0)),
            pl.BlockSpec((1, w), lambda b: (0, 0)),
            pl.BlockSpec((w, w), lambda b: (0, 0)),
        ],
        out_specs=pl.BlockSpec((1, nt, w), lambda b: (b, 0, 0)),
        out_shape=jax.ShapeDtypeStruct((nb, nt, w), BF16),
        scratch_shapes=[
            pltpu.VMEM((nt, w), F32), pltpu.VMEM((nt, w), F32), pltpu.VMEM((nt, w), F32),
            pltpu.VMEM((nt, LANES), F32), pltpu.VMEM((nt, LANES), F32),
            pltpu.VMEM((nc, C_T_ROWS, CHUNK), F32),
            pltpu.VMEM((2, nc, 2 * CHUNK, w), BF16),
            pltpu.VMEM((2, nc, CHUNK, w), F32),
            pltpu.VMEM((2, nc, SUBLANES, w), F32),
            pltpu.VMEM((nt, w), F32),
            pltpu.VMEM((2, C_HEADS, HEAD_DIM, HEAD_DIM), F32),
        ],
        compiler_params=_params(("arbitrary",)),
        name="mixer_c_gated_delta",
    )(c_qkv, c_ba, c_g, conv_w, par, gain, ones_bd)


GLA_EXP_CAP = 80.0
GLA_LOCAL_CHUNKS = 2
GLA_SCAN_STEPS = 4


def _gla_kernel(qk_ref, v_ref, lr_ref, gate_ref, gw_ref, gb_ref, gain_ref, bd_ref, o_ref,
                la_s, qg_s, el_s, ds_s, o_s):
    nt = qk_ref.shape[1]
    n_tiles = nt // TOKEN_TILE
    n_chunks = nt // CHUNK
    n_ctx_chunks = CTX_LEN // CHUNK
    kw = D_HEADS * D_KDIM
    gw = gw_ref[...]
    gb = gb_ref[...]

    def prep(i, carry):
        r0 = pl.multiple_of(i * TOKEN_TILE, TOKEN_TILE)
        z = _dot3(lr_ref[0, pl.ds(r0, TOKEN_TILE), :], gw) + gb
        log_sig = jnp.minimum(z, 0.0) - jnp.log1p(jnp.exp(-jnp.abs(z)))
        la_s[pl.ds(r0, TOKEN_TILE), :] = log_sig * (1.0 / GLA_TAU)
        return carry

    lax.fori_loop(0, n_tiles, prep, 0)

    masks = _order_masks()
    mask_bf = [jnp.where(m[0], 1.0, 0.0).astype(BF16) for m in masks]
    vw = D_HEADS * HEAD_DIM
    k_head = lax.broadcasted_iota(jnp.int32, (CHUNK, kw), 1) // D_KDIM
    v_head = lax.broadcasted_iota(jnp.int32, (CHUNK, vw), 1) // HEAD_DIM
    state_diag = (lax.broadcasted_iota(jnp.int32, (vw, kw), 0) // HEAD_DIM
                  == lax.broadcasted_iota(jnp.int32, (vw, kw), 1) // D_KDIM)

    def local(it, carry):
        units, pre = [], {}
        for t in range(GLA_LOCAL_CHUNKS):
            c = it * GLA_LOCAL_CHUNKS + t
            r0 = pl.multiple_of(c * CHUNK, CHUNK)
            qk = qk_ref[0, pl.ds(r0, CHUNK), :]
            q = qk[:, 0:kw] * (D_KDIM ** -0.5)
            k = qk[:, kw:2 * kw]
            v = v_ref[0, pl.ds(r0, CHUNK), :]
            v_b = v.astype(BF16)
            v_t = v.T.astype(BF16)
            for d in range(2):
                last = CHUNK - 1 if d == 0 else 0
                g = _sel_dot(mask_bf[d], la_s[pl.ds(r0, CHUNK), d * kw:(d + 1) * kw])
                g_mid = g[CHUNK // 2:CHUNK // 2 + 1, :]
                g_last = g[last:last + 1, :]
                q_t = q * jnp.exp(jnp.minimum(g - g_mid, GLA_EXP_CAP))
                pre[(t, d)] = ([jnp.where(k_head == h, q_t, 0.0).astype(BF16) for h in range(D_HEADS)],
                               (k * jnp.exp(jnp.minimum(g_mid - g, GLA_EXP_CAP))).astype(BF16),
                               (k * jnp.exp(g_last - g)).astype(BF16), v_b, v_t)
                qg_s[d, c] = (q * jnp.exp(g)).astype(BF16)
                el_s[d, c] = jnp.broadcast_to(jnp.exp(g_last), (SUBLANES, kw))
                units += [(t, d, h) for h in range(D_HEADS)]
        a_ = [jnp.where(masks[d][0], _dot_nt(pre[(t, d)][0][h], pre[(t, d)][1]), 0.0).astype(BF16)
              for t, d, h in units]
        ds_ = {td: jnp.dot(p[4], p[2], preferred_element_type=F32) for td, p in pre.items()}
        av = [jnp.dot(a_[i], pre[(t, d)][3], preferred_element_type=F32) for i, (t, d, h) in enumerate(units)]
        i = 0
        for t in range(GLA_LOCAL_CHUNKS):
            c = it * GLA_LOCAL_CHUNKS + t
            r0 = pl.multiple_of(c * CHUNK, CHUNK)
            o_const = jnp.zeros((CHUNK, vw), F32)
            for d in range(2):
                ds_s[d, c] = jnp.where(state_diag, ds_[(t, d)], 0.0)
                for h in range(D_HEADS):
                    o_const = o_const + jnp.where(v_head == h, av[i], 0.0)
                    i += 1
            o_s[pl.ds(r0, CHUNK), :] = o_const
        return carry

    lax.fori_loop(0, n_chunks // GLA_LOCAL_CHUNKS, local, 0)

    def scan(it, states):
        states = list(states)
        jobs = []
        for t in range(GLA_SCAN_STEPS):
            step = it * GLA_SCAN_STEPS + t
            for d in range(2):
                c = _scan_chunk(step, d, n_ctx_chunks, n_chunks)
                jobs.append((c, qg_s[d, c], states[d].astype(BF16)))
                states[d] = states[d] * el_s[d, c][0:1, :] + ds_s[d, c]
        outs = [_dot_nt(qg, sb) for _, qg, sb in jobs]
        for (c, _, _), o in zip(jobs, outs):
            r0 = pl.multiple_of(c * CHUNK, CHUNK)
            o_s[pl.ds(r0, CHUNK), :] = o_s[pl.ds(r0, CHUNK), :] + o
        return tuple(states)

    zero = jnp.zeros((vw, kw), F32)
    lax.fori_loop(0, n_chunks // GLA_SCAN_STEPS, scan, (zero, zero))
    _gated_out_tiles((o_s,), gate_ref, gain_ref, bd_ref, o_ref)


def _gla_call(d_qk, d_v, d_lr, d_g, gw_blk, gb_row, gain, ones_bd):
    nb, nt, _ = d_qk.shape
    vw = D_HEADS * HEAD_DIM
    kw2 = 2 * D_HEADS * D_KDIM
    return pl.pallas_call(
        _gla_kernel,
        grid=(nb,),
        in_specs=[
            pl.BlockSpec((1, nt, kw2), lambda b: (b, 0, 0)),
            pl.BlockSpec((1, nt, vw), lambda b: (b, 0, 0)),
            pl.BlockSpec((1, nt, LANES), lambda b: (b, 0, 0)),
            pl.BlockSpec((1, nt, vw), lambda b: (b, 0, 0)),
            pl.BlockSpec((LANES, kw2), lambda b: (0, 0)),
            pl.BlockSpec((1, kw2), lambda b: (0, 0)),
            pl.BlockSpec((1, vw), lambda b: (0, 0)),
            pl.BlockSpec((vw, vw), lambda b: (0, 0)),
        ],
        out_specs=pl.BlockSpec((1, nt, vw), lambda b: (b, 0, 0)),
        out_shape=jax.ShapeDtypeStruct((nb, nt, vw), BF16),
        scratch_shapes=[
            pltpu.VMEM((nt, kw2), F32),
            pltpu.VMEM((2, nt // CHUNK, CHUNK, kw2 // 2), BF16),
            pltpu.VMEM((2, nt // CHUNK, SUBLANES, kw2 // 2), F32),
            pltpu.VMEM((2, nt // CHUNK, vw, kw2 // 2), F32),
            pltpu.VMEM((nt, vw), F32),
        ],
        compiler_params=_params(("arbitrary",)),
        name="mixer_d_gla",
    )(d_qk, d_v, d_lr, d_g, gw_blk, gb_row, gain, ones_bd)


ROUTE_E1, ROUTE_E2, ROUTE_W1, ROUTE_W2 = 0, 1, 2, 3
ROUTER_EXPERT_LANE = N_GROUPS


def _route(logits):
    lane = lax.broadcasted_iota(jnp.int32, logits.shape, 1).astype(F32)
    far = float(LANES)
    in_grp = lane < N_GROUPS
    lg = jnp.where(in_grp, logits, NEG_BIG)
    mg = jnp.max(lg, axis=-1, keepdims=True)
    grp = jnp.min(jnp.where(lg == mg, lane, far), axis=-1, keepdims=True)
    p_grp = 1.0 / jnp.sum(jnp.where(in_grp, jnp.exp(lg - mg), 0.0), axis=-1, keepdims=True)
    lo = ROUTER_EXPERT_LANE + EXP_PER_GROUP * grp
    in_exp = jnp.logical_and(lane >= lo, lane < lo + EXP_PER_GROUP)
    le = jnp.where(in_exp, logits, NEG_BIG)
    m1 = jnp.max(le, axis=-1, keepdims=True)
    i1 = jnp.min(jnp.where(le == m1, lane, far), axis=-1, keepdims=True)
    le2 = jnp.where(lane == i1, NEG_BIG, le)
    m2 = jnp.max(le2, axis=-1, keepdims=True)
    i2 = jnp.min(jnp.where(le2 == m2, lane, far), axis=-1, keepdims=True)
    e2 = jnp.exp(m2 - m1)
    w1 = p_grp / (1.0 + e2)
    w2 = p_grp * e2 / (1.0 + e2)
    out = jnp.where(lane == ROUTE_E1, i1 - ROUTER_EXPERT_LANE, 0.0)
    out = jnp.where(lane == ROUTE_E2, i2 - ROUTER_EXPERT_LANE, out)
    out = jnp.where(lane == ROUTE_W1, w1, out)
    return jnp.where(lane == ROUTE_W2, w2, out)


def _merge_kernel(x_ref, mod_ref, oa_ref, ob_ref, oc_ref, od_ref, wg_ref, wbr_ref, wo_ref,
                  lng_ref, lnb_ref, wr_ref, br_ref, x1_ref, h2_ref, route_ref):
    x = x_ref[0]
    mod = mod_ref[0]
    h = (x * (1.0 + mod[1:2]) + mod[0:1]).astype(BF16)
    m = None
    for z, o_ref in enumerate((oa_ref, ob_ref, oc_ref, od_ref)):
        gate = jax.nn.sigmoid(jnp.dot(h, wg_ref[:, z * D_MODEL:(z + 1) * D_MODEL], preferred_element_type=F32))
        up = jnp.dot(o_ref[0], wbr_ref[z], preferred_element_type=F32)
        m = gate * up if m is None else m + gate * up
    y = jnp.dot(m.astype(BF16), wo_ref[...], preferred_element_type=F32)
    x1 = _layer_norm(DN_ALPHA * x + mod[2:3] * y, lng_ref[...], lnb_ref[...])
    x1_ref[0] = x1
    h2 = x1 * (1.0 + mod[4:5]) + mod[3:4]
    for j in range(ROW_VREGS):
        h2_ref[pl.ds(j, TOKEN_TILE, stride=ROW_VREGS), :] = h2[:, j * LANES:(j + 1) * LANES]
    route_ref[0] = _route(_dot3(h2, wr_ref[...]) + br_ref[...]).T[0:SUBLANES, :]


def _merge_call(xa, mods, oa, ob, oc, od, wg, wbr, wo, ln_g, ln_b, wr, br):
    nb, nt, d = xa.shape
    tiles = nt // TOKEN_TILE
    bw = oa.shape[-1]
    tok = lambda b, t: (b, t, 0)
    const2 = lambda b, t: (0, 0)
    return pl.pallas_call(
        _merge_kernel,
        grid=(nb, tiles),
        in_specs=[
            pl.BlockSpec((1, TOKEN_TILE, d), tok),
            pl.BlockSpec((1, SUBLANES, d), _mod_index(nb)),
            pl.BlockSpec((1, TOKEN_TILE, bw), tok), pl.BlockSpec((1, TOKEN_TILE, bw), tok),
            pl.BlockSpec((1, TOKEN_TILE, bw), tok), pl.BlockSpec((1, TOKEN_TILE, bw), tok),
            pl.BlockSpec((d, N_BRANCH * d), const2),
            pl.BlockSpec((N_BRANCH, bw, d), lambda b, t: (0, 0, 0)),
            pl.BlockSpec((d, d), const2),
            pl.BlockSpec((1, d), const2), pl.BlockSpec((1, d), const2),
            pl.BlockSpec((d, LANES), const2), pl.BlockSpec((1, LANES), const2),
        ],
        out_specs=[pl.BlockSpec((1, TOKEN_TILE, d), tok),
                   pl.BlockSpec((TOKEN_TILE * ROW_VREGS, LANES), lambda b, t: (b * tiles + t, 0)),
                   pl.BlockSpec((1, SUBLANES, TOKEN_TILE), lambda b, t: (b * tiles + t, 0, 0))],
        out_shape=[jax.ShapeDtypeStruct((nb, nt, d), F32),
                   jax.ShapeDtypeStruct((nb * nt * ROW_VREGS, LANES), F32),
                   jax.ShapeDtypeStruct((nb * tiles, SUBLANES, TOKEN_TILE), F32)],
        compiler_params=_params(("arbitrary", "arbitrary")),
        name="merge_out_ln1_router",
    )(xa, mods, oa, ob, oc, od, wg, wbr, wo, ln_g, ln_b, wr, br)


MOE_UNROLL = 4


def _moe_chunk(total):
    return max(c for c in range(TOKEN_TILE, MOE_CHUNK + 1, TOKEN_TILE) if total % c == 0)


MOE_GROUP = 2


def _moe_kernel(cnt_ref, off_ref, off_again_ref, idx_ref, wt_ref, x_ref, *refs):
    w_refs = [refs[3 * g:3 * g + 3] for g in range(MOE_GROUP)]
    y_ref = refs[3 * MOE_GROUP]
    xt_s = refs[3 * MOE_GROUP + 1:3 * MOE_GROUP + 1 + MOE_GROUP]
    ot_s = refs[3 * MOE_GROUP + 1 + MOE_GROUP:]
    c = pl.program_id(0)
    j = pl.program_id(1)
    chunk = x_ref.shape[1] // ROW_VREGS
    plan_rows = 2 * chunk

    @pl.when(j == 0)
    def _zero():
        y_ref[...] = jnp.zeros(y_ref.shape, F32)

    n_rows = [cnt_ref[c * N_EXPERTS + j * MOE_GROUP + g] for g in range(MOE_GROUP)]
    off = [off_ref[c * N_EXPERTS + j * MOE_GROUP + g] for g in range(MOE_GROUP)]
    off_again = [off_again_ref[c * N_EXPERTS + j * MOE_GROUP + g] for g in range(MOE_GROUP)]
    n_tiles = functools.reduce(jnp.maximum, [(n + MOE_ROWS - 1) // MOE_ROWS for n in n_rows])

    def slab(tok):
        return pl.ds(pl.multiple_of(tok * ROW_VREGS, ROW_VREGS), ROW_VREGS)

    def gather(base, xt):
        for mi in range(MOE_ROWS):
            xt[pl.ds(mi, ROW_VREGS, stride=MOE_STRIDE), :] = x_ref[0, slab(idx_ref[0, 0, base + mi]), :]

    def experts(group):
        xs = [jnp.concatenate([xt_s[g][k * MOE_STRIDE:k * MOE_STRIDE + MOE_ROWS, :] for k in range(ROW_VREGS)],
                              axis=-1).astype(BF16) for g in group]
        gates = [jnp.dot(x, w_refs[g][0][0], preferred_element_type=F32) for x, g in zip(xs, group)]
        ups = [jnp.dot(x, w_refs[g][1][0], preferred_element_type=F32) for x, g in zip(xs, group)]
        acts = [(_silu(gate) * up).astype(BF16) for gate, up in zip(gates, ups)]
        outs = [jnp.dot(act, w_refs[g][2][0], preferred_element_type=F32) for act, g in zip(acts, group)]
        for out, g in zip(outs, group):
            for k in range(ROW_VREGS):
                ot_s[g][k * MOE_STRIDE:k * MOE_STRIDE + MOE_ROWS, :] = out[:, k * LANES:(k + 1) * LANES]

    def scatter(base, valid, ot):
        for m0 in range(0, MOE_ROWS, MOE_UNROLL):
            pending = []
            for mi in range(m0, m0 + MOE_UNROLL):
                ok = mi < valid
                rows = slab(jnp.where(ok, idx_ref[0, 0, base + mi], chunk))
                wgt = jnp.where(ok, wt_ref[0, 0, base + mi], 0.0)
                upd = y_ref[0, rows, :] + wgt * ot[pl.ds(mi, ROW_VREGS, stride=MOE_STRIDE), :]
                pending.append((rows, upd))
            for rows, upd in pending:
                y_ref[0, rows, :] = upd

    def tile(i, group):
        base = {g: jnp.minimum(off[g] + i * MOE_ROWS, plan_rows) for g in group}
        base_s = {g: jnp.minimum(off_again[g] + i * MOE_ROWS, plan_rows) for g in group}
        for g in group:
            gather(base[g], xt_s[g])
        experts(group)
        for g in group:
            scatter(base_s[g], n_rows[g] - i * MOE_ROWS, ot_s[g])

    @pl.when(n_tiles > 0)
    def _first_tiles():
        tile(0, tuple(range(MOE_GROUP)))

    def more_tiles(i, carry):
        for g in range(MOE_GROUP):
            @pl.when(n_rows[g] > i * MOE_ROWS)
            def _one():
                tile(i, (g,))
        return carry

    lax.fori_loop(1, n_tiles, more_tiles, 0)


def _moe_call(h2_slab, cnt, off, idx_sorted, w_sorted, wg, wu, wd):
    nch, rows_in, _ = h2_slab.shape
    plan = idx_sorted.shape[2]
    rows_out = rows_in + SUBLANES * ROW_VREGS
    w_specs, w_args = [], []
    for g in range(MOE_GROUP):
        pick = lambda c, j, *_, g=g: (j * MOE_GROUP + g, 0, 0)
        w_specs += [pl.BlockSpec((1, D_MODEL, EXP_HIDDEN), pick), pl.BlockSpec((1, D_MODEL, EXP_HIDDEN), pick),
                    pl.BlockSpec((1, EXP_HIDDEN, D_MODEL), pick)]
        w_args += [wg, wu, wd]
    tile_buf = pltpu.VMEM((ROW_VREGS * MOE_STRIDE, LANES), F32)
    grid_spec = pltpu.PrefetchScalarGridSpec(
        num_scalar_prefetch=3,
        grid=(nch, N_EXPERTS // MOE_GROUP),
        in_specs=[
            pl.BlockSpec((1, 1, plan), lambda c, j, *_: (c, 0, 0), memory_space=pltpu.SMEM),
            pl.BlockSpec((1, 1, plan), lambda c, j, *_: (c, 0, 0), memory_space=pltpu.SMEM),
            pl.BlockSpec((1, rows_in, LANES), lambda c, j, *_: (c, 0, 0)),
        ] + w_specs,
        out_specs=pl.BlockSpec((1, rows_out, LANES), lambda c, j, *_: (c, 0, 0)),
        scratch_shapes=[tile_buf] * (2 * MOE_GROUP),
    )
    return pl.pallas_call(
        _moe_kernel,
        grid_spec=grid_spec,
        out_shape=jax.ShapeDtypeStruct((nch, rows_out, LANES), F32),
        compiler_params=_params(("arbitrary", "arbitrary")),
        name="moe_experts",
    )(cnt, off, off, idx_sorted, w_sorted, h2_slab, *w_args)


def _moe_plan(route, chunk):
    nch = route.shape[0] * TOKEN_TILE // chunk
    per = chunk * 2
    eid = route[:, ROUTE_E1:ROUTE_E2 + 1, :].astype(jnp.int32).reshape(nch, per)
    wts = route[:, ROUTE_W1:ROUTE_W2 + 1, :].reshape(nch, per)
    order = jnp.argsort(eid, axis=1, stable=True).astype(jnp.int32)
    token = (order // (2 * TOKEN_TILE)) * TOKEN_TILE + order % TOKEN_TILE
    idx_sorted = jnp.pad(token, ((0, 0), (0, MOE_ROWS)))
    w_sorted = jnp.pad(jnp.take_along_axis(wts, order, axis=1), ((0, 0), (0, MOE_ROWS)))
    cnt = jnp.sum((eid[..., None] == jnp.arange(N_EXPERTS, dtype=jnp.int32)).astype(jnp.int32), axis=1)
    off = jnp.cumsum(cnt, axis=1) - cnt
    return (cnt.reshape(-1).astype(jnp.int32), off.reshape(-1).astype(jnp.int32),
            idx_sorted.reshape(nch, 1, per + MOE_ROWS), w_sorted.reshape(nch, 1, per + MOE_ROWS))


def _ln2_kernel(x_ref, y_ref, mod_ref, g_ref, b_ref, o_ref):
    o_ref[0] = _moe_residual_ln(x_ref[0], y_ref, mod_ref[0][5:6], g_ref, b_ref)


def _ln2_call(x1, y_slab, mods, ln_g, ln_b, chunk):
    nb, nt, d = x1.shape
    tiles = nt // TOKEN_TILE
    ctx_tiles = CTX_LEN // TOKEN_TILE
    slab = _slab_index(tiles, chunk)
    return pl.pallas_call(
        _ln2_kernel,
        grid=(nb, tiles - ctx_tiles),
        in_specs=[pl.BlockSpec((1, TOKEN_TILE, d), lambda b, t: (b, t + ctx_tiles, 0)),
                  pl.BlockSpec((1, TOKEN_TILE * ROW_VREGS, LANES), lambda b, t: slab(b, t + ctx_tiles)),
                  pl.BlockSpec((1, SUBLANES, d), lambda b, t: (b, 0, 0)),
                  pl.BlockSpec((1, d), lambda b, t: (0, 0)), pl.BlockSpec((1, d), lambda b, t: (0, 0))],
        out_specs=pl.BlockSpec((1, TOKEN_TILE, d), lambda b, t: (b, t, 0)),
        out_shape=jax.ShapeDtypeStruct((nb, nt - CTX_LEN, d), F32),
        compiler_params=_params(("arbitrary", "arbitrary")),
        name="moe_residual_ln2",
    )(x1, y_slab, mods, ln_g, ln_b)


def _head_constants():
    w = A_HEADS * HEAD_DIM
    i = np.arange(w)
    ones_bd = (i[:, None] // HEAD_DIM == i[None, :] // HEAD_DIM).astype(np.float32)
    quarter = HEAD_DIM // 4
    rot = np.zeros((w, w), np.float32)
    first = (i % (2 * quarter)) < quarter
    rot[i[first] + quarter, i[first]] = -1.0
    rot[i[~first] - quarter, i[~first]] = 1.0
    return jnp.asarray(ones_bd, BF16), jnp.asarray(rot, BF16)


def _rope_tables(seq):
    t = jnp.arange(seq, dtype=jnp.int32)
    row = (t // GRID_W).astype(F32)
    col = (t % GRID_W).astype(F32)
    nf = HEAD_DIM // 4
    inv = ROPE_THETA ** (-jnp.arange(nf, dtype=F32) / nf)
    ang_r = row[:, None] * inv
    ang_c = col[:, None] * inv
    cos = jnp.concatenate([jnp.cos(ang_r), jnp.cos(ang_r), jnp.cos(ang_c), jnp.cos(ang_c)], axis=-1)
    sin = jnp.concatenate([jnp.sin(ang_r), jnp.sin(ang_r), jnp.sin(ang_c), jnp.sin(ang_c)], axis=-1)
    return jnp.tile(cos, (1, A_HEADS)), jnp.tile(sin, (1, A_HEADS))


def _in_weight(w_in):
    cols = []
    for _, parts, width, _ in IN_GROUPS:
        got = 0
        for p in parts:
            o, n = _IN_OFFS[p]
            cols.append(w_in[..., o:o + n])
            got += n
        if got < width:
            cols.append(jnp.zeros(w_in.shape[:-1] + (width - got,), w_in.dtype))
    return jnp.concatenate(cols, axis=-1).astype(BF16)


def _lane_row(vec, width, offset=0):
    return jnp.zeros((1, width), F32).at[0, offset:offset + vec.shape[0]].set(vec.astype(F32))


def kernel(x, c, ctx, c_ctx, w_ada, b_ada, w_in, a_q_gain, a_k_gain, b_rpb, c_conv, c_a_log, c_dt_bias, c_out_gain, d_gate_w, d_gate_b, d_out_gain, w_branch, w_out, ln1_g, ln1_b, ln2_g, ln2_b, w_router_g, b_router_g, w_router_e, b_router_e, w_up, w_gate, w_down):
    nb, seq, d = x.shape
    depth = w_ada.shape[0]
    nt = CTX_LEN + seq
    assert d == D_MODEL and ctx.shape[1] == CTX_LEN and nb + 1 <= ADA_ROWS
    assert seq % TOKEN_TILE == 0
    chunk = _moe_chunk(nb * nt)

    xa = jnp.concatenate([ctx, x], axis=1)
    cc = jnp.zeros((ADA_ROWS, d), F32).at[:nb].set(c).at[nb].set(c_ctx)
    mods = _ada_call(cc, w_ada, b_ada).reshape(depth, ADA_ROWS, 6, d)[:, :nb + 1]
    mods = jnp.pad(mods, ((0, 0), (0, 0), (0, SUBLANES - 6), (0, 0)))

    ones_bd, rot_m = _head_constants()
    cos_t, sin_t = _rope_tables(seq)
    gates_off = _IN_OFFS['gates'][0]

    w_mix = _in_weight(w_in)
    w_gates = w_in[..., gates_off:].astype(BF16)
    w_br, w_o = w_branch.astype(BF16), w_out.astype(BF16)
    w_eg, w_eu, w_ed = w_gate.astype(BF16), w_up.astype(BF16), w_down.astype(BF16)
    nb_bias = _nb_bias_table(b_rpb)

    x1 = y_slab = None
    for l in range(depth):
        if l == 0:
            proj = _in_call(xa, mods[l], w_mix[l])
        else:
            xa, proj = _ln2_in_call(x1, y_slab, mods[l - 1], ln2_g[l - 1][None, :], ln2_b[l - 1][None, :],
                                    mods[l], w_mix[l], chunk)
        proj = dict(zip([g[0] for g in IN_GROUPS], proj))

        oa = _attn_a_call(proj['a_q'], proj['a_kv'], cos_t, sin_t,
                          jnp.tile(a_q_gain[l], A_HEADS)[None, :], jnp.tile(a_k_gain[l], A_KV_HEADS)[None, :],
                          ones_bd, rot_m)
        ob = _attn_b_call(proj['b_q'], proj['b_k'], proj['b_v'], nb_bias[l])
        conv_w = jnp.pad(c_conv[l], ((0, SUBLANES - c_conv.shape[1]), (0, 0)))
        par = jnp.concatenate([_lane_row(c_a_log[l].reshape(-1), LANES, C_A_LANE),
                               _lane_row(c_dt_bias[l].reshape(-1), LANES, C_A_LANE),
                               jnp.zeros((SUBLANES - 2, LANES), F32)], axis=0)
        oc = _gdn_call(proj['c_qkv'], proj['c_ba'], proj['c_g'], conv_w, par,
                       jnp.tile(c_out_gain[l], C_HEADS)[None, :], ones_bd)
        kw = D_HEADS * D_KDIM
        gw_blk = jnp.zeros((LANES, 2 * kw), F32)
        gw_blk = gw_blk.at[0:D_GATE_RANK, 0:kw].set(d_gate_w[l, 0])
        gw_blk = gw_blk.at[D_GATE_RANK:2 * D_GATE_RANK, kw:2 * kw].set(d_gate_w[l, 1])
        od = _gla_call(proj['d_qk'], proj['d_v'], proj['d_lr'], proj['d_g'], gw_blk,
                       d_gate_b[l].reshape(1, 2 * kw), jnp.tile(d_out_gain[l], D_HEADS)[None, :], ones_bd)

        wr = jnp.concatenate([w_router_g[l], jnp.transpose(w_router_e[l], (1, 0, 2)).reshape(d, N_EXPERTS)], axis=1)
        wr = jnp.pad(wr, ((0, 0), (0, LANES - wr.shape[1])))
        br = _lane_row(jnp.concatenate([b_router_g[l], b_router_e[l].reshape(-1)]), LANES)
        x1, h2, route = _merge_call(
            xa, mods[l], oa, ob, oc, od, w_gates[l], w_br[l], w_o[l], ln1_g[l][None, :], ln1_b[l][None, :], wr, br)

        cnt, off, idx_sorted, w_sorted = _moe_plan(route, chunk)
        h2_slab = h2.reshape((nb * nt) // chunk, chunk * ROW_VREGS, LANES)
        y_slab = _moe_call(h2_slab, cnt, off, idx_sorted, w_sorted, w_eg[l], w_eu[l], w_ed[l])

    last = depth - 1
    return _ln2_call(x1, y_slab, mods[last], ln2_g[last][None, :], ln2_b[last][None, :], chunk)
```

```python
import functools
import math

import numpy as np
import jax
import jax.numpy as jnp
from jax import lax
from jax.experimental import pallas as pl
from jax.experimental.pallas import tpu as pltpu

F32 = jnp.float32
BF16 = jnp.bfloat16

D_MODEL = 1024
DEPTH = 4
GRID_W = 64
CTX_LEN = 256
HEAD_DIM = 64
A_HEADS = 4
A_KV_HEADS = 2
ROPE_THETA = 10000.0
B_HEADS = 4
WIN_R = 8
WIN_C = 16
C_HEADS = 4
D_HEADS = 4
D_KDIM = 32
D_GATE_RANK = 16
GLA_TAU = 16.0
CHUNK = 64
N_BRANCH = 4
N_GROUPS = 4
EXP_PER_GROUP = 8
N_EXPERTS = N_GROUPS * EXP_PER_GROUP
EXP_HIDDEN = 512
EPS = 1e-6
DN_ALPHA = (2.0 * DEPTH) ** 0.25
NEG_BIG = -1e30

LANES = 128
SUBLANES = 8
TOKEN_TILE = 256
VMEM_LIMIT = 56 * 1024 * 1024

_IN_OFFS = {}
_off = 0
for _n, _w in (('a_q', 256), ('a_k', 128), ('a_v', 128), ('b_q', 256), ('b_k', 256), ('b_v', 256),
               ('c_qkv', 768), ('c_beta', 8), ('c_a', 8), ('c_g', 256), ('d_q', 128), ('d_k', 128),
               ('d_v', 256), ('d_lr', 32), ('d_g', 256), ('gates', 4096)):
    _IN_OFFS[_n] = (_off, _w)
    _off += _w
IN_GROUPS = (
    ('a_q', ('a_q',), 256, BF16),
    ('a_kv', ('a_k', 'a_v'), 256, BF16),
    ('b_q', ('b_q',), 256, BF16),
    ('b_k', ('b_k',), 256, BF16),
    ('b_v', ('b_v',), 256, BF16),
    ('c_qkv', ('c_qkv',), 768, BF16),
    ('c_ba', ('c_beta', 'c_a'), 128, F32),
    ('c_g', ('c_g',), 256, F32),
    ('d_qk', ('d_q', 'd_k'), 256, F32),
    ('d_v', ('d_v',), 256, F32),
    ('d_lr', ('d_lr',), 128, F32),
    ('d_g', ('d_g',), 256, F32),
)
IN_TOTAL = sum(g[2] for g in IN_GROUPS)

MOE_CHUNK = 2048
MOE_ROWS = 160
MOE_STRIDE = MOE_ROWS + SUBLANES
ROW_VREGS = D_MODEL // LANES


def _dot(a, b):
    return jnp.dot(a.astype(BF16), b.astype(BF16), preferred_element_type=F32)


def _dot_nt(a, b):
    return lax.dot_general(a.astype(BF16), b.astype(BF16), (((1,), (1,)), ((), ())),
                           preferred_element_type=F32)


def _dot_tn(a, b):
    return lax.dot_general(a.astype(BF16), b.astype(BF16), (((0,), (0,)), ((), ())),
                           preferred_element_type=F32)


def _split(x):
    hi = x.astype(BF16)
    lo = (x - hi.astype(F32)).astype(BF16)
    return hi, lo


def _dot3(a, b):
    ah, al = _split(a)
    bh, bl = _split(b)
    return (jnp.dot(ah, bh, preferred_element_type=F32) + jnp.dot(al, bh, preferred_element_type=F32)
            + jnp.dot(ah, bl, preferred_element_type=F32))


def _dot_sel(a, m):
    ah, al = _split(a)
    return jnp.dot(ah, m, preferred_element_type=F32) + jnp.dot(al, m, preferred_element_type=F32)


def _silu(x):
    return x * jax.nn.sigmoid(x)


def _layer_norm(r, g, b):
    mu = jnp.mean(r, axis=-1, keepdims=True)
    d = r - mu
    var = jnp.mean(d * d, axis=-1, keepdims=True)
    return d * lax.rsqrt(var + EPS) * g + b


def _params(sem):
    return pltpu.CompilerParams(dimension_semantics=sem, vmem_limit_bytes=VMEM_LIMIT)


ADA_ROWS = 24
ADA_TILE = 1536


def _ada_kernel(cc_ref, w_ref, b_ref, o_ref):
    s = _silu(cc_ref[...])
    o_ref[0] = _dot3(s, w_ref[0]) + b_ref[0]


def _ada_call(cc, w_ada, b_ada):
    depth = w_ada.shape[0]
    n = w_ada.shape[2]
    return pl.pallas_call(
        _ada_kernel,
        grid=(depth, n // ADA_TILE),
        in_specs=[
            pl.BlockSpec((ADA_ROWS, D_MODEL), lambda l, j: (0, 0)),
            pl.BlockSpec((1, D_MODEL, ADA_TILE), lambda l, j: (l, 0, j)),
            pl.BlockSpec((1, 1, ADA_TILE), lambda l, j: (l, 0, j)),
        ],
        out_specs=pl.BlockSpec((1, ADA_ROWS, ADA_TILE), lambda l, j: (l, 0, j)),
        out_shape=jax.ShapeDtypeStruct((depth, ADA_ROWS, n), F32),
        compiler_params=_params(("arbitrary", "arbitrary")),
        name="ada_mod",
    )(cc, w_ada, b_ada.reshape(depth, 1, n))


def _mod_index(nb):
    return lambda b, t: (jnp.where(t == 0, nb, b), 0, 0)


def _project_in(x, mod, w_ref, out_refs):
    h = (x * (1.0 + mod[1:2]) + mod[0:1]).astype(BF16)
    off = 0
    for (name, _, width, dt), o_ref in zip(IN_GROUPS, out_refs):
        o_ref[0] = jnp.dot(h, w_ref[:, off:off + width], preferred_element_type=F32).astype(dt)
        off += width


def _in_kernel(x_ref, mod_ref, w_ref, *out_refs):
    _project_in(x_ref[0], mod_ref[0], w_ref, out_refs)


def _moe_residual_ln(x1, y_ref, gate, g_ref, b_ref):
    y = jnp.concatenate([y_ref[0, pl.ds(j, TOKEN_TILE, stride=ROW_VREGS), :] for j in range(ROW_VREGS)], axis=-1)
    return _layer_norm(DN_ALPHA * x1 + gate * y, g_ref[...], b_ref[...])


def _ln2_in_kernel(x1_ref, y_ref, mod_prev_ref, g_ref, b_ref, mod_ref, w_ref, xa_ref, *out_refs):
    xa = _moe_residual_ln(x1_ref[0], y_ref, mod_prev_ref[0][5:6], g_ref, b_ref)
    xa_ref[0] = xa
    _project_in(xa, mod_ref[0], w_ref, out_refs)


def _slab_index(tiles, chunk):
    per_chunk = chunk // TOKEN_TILE
    return lambda b, t: ((b * tiles + t) // per_chunk, (b * tiles + t) % per_chunk, 0)


def _ln2_in_call(x1, y_slab, mods_prev, ln_g, ln_b, mods, w_cat, chunk):
    nb, nt, d = x1.shape
    tiles = nt // TOKEN_TILE
    tok = lambda b, t: (b, t, 0)
    row = lambda b, t: (0, 0)
    outs = pl.pallas_call(
        _ln2_in_kernel,
        grid=(nb, tiles),
        in_specs=[
            pl.BlockSpec((1, TOKEN_TILE, d), tok),
            pl.BlockSpec((1, TOKEN_TILE * ROW_VREGS, LANES), _slab_index(tiles, chunk)),
            pl.BlockSpec((1, SUBLANES, d), _mod_index(nb)),
            pl.BlockSpec((1, d), row), pl.BlockSpec((1, d), row),
            pl.BlockSpec((1, SUBLANES, d), _mod_index(nb)),
            pl.BlockSpec((d, IN_TOTAL), row),
        ],
        out_specs=[pl.BlockSpec((1, TOKEN_TILE, d), tok)]
        + [pl.BlockSpec((1, TOKEN_TILE, g[2]), tok) for g in IN_GROUPS],
        out_shape=[jax.ShapeDtypeStruct((nb, nt, d), F32)]
        + [jax.ShapeDtypeStruct((nb, nt, g[2]), g[3]) for g in IN_GROUPS],
        compiler_params=_params(("arbitrary", "arbitrary")),
        name="ln2_in_proj",
    )(x1, y_slab, mods_prev, ln_g, ln_b, mods, w_cat)
    return outs[0], outs[1:]


def _in_call(xa, mods, w_cat):
    nb, nt, _ = xa.shape
    tiles = nt // TOKEN_TILE
    return pl.pallas_call(
        _in_kernel,
        grid=(nb, tiles),
        in_specs=[
            pl.BlockSpec((1, TOKEN_TILE, D_MODEL), lambda b, t: (b, t, 0)),
            pl.BlockSpec((1, SUBLANES, D_MODEL), _mod_index(nb)),
            pl.BlockSpec((D_MODEL, IN_TOTAL), lambda b, t: (0, 0)),
        ],
        out_specs=[pl.BlockSpec((1, TOKEN_TILE, g[2]), lambda b, t: (b, t, 0)) for g in IN_GROUPS],
        out_shape=[jax.ShapeDtypeStruct((nb, nt, g[2]), g[3]) for g in IN_GROUPS],
        compiler_params=_params(("arbitrary", "arbitrary")),
        name="in_proj",
    )(xa, mods, w_cat)


def _head_rms(x, ones_bd, gain):
    ss = _dot_sel(x * x, ones_bd)
    return x * lax.rsqrt(ss * (1.0 / HEAD_DIM) + EPS) * gain


def _rope(x, rot, cos, sin):
    return x * cos + _dot_sel(x, rot) * sin


def _attn_a_kernel(q_ref, kv_ref, cos_ref, sin_ref, qg_ref, kg_ref, bd_ref, rot_ref, o_ref, kp_ref, vt_ref):
    t = pl.program_id(1)
    kvw = A_KV_HEADS * HEAD_DIM
    rep = A_HEADS // A_KV_HEADS
    assert rep * HEAD_DIM == kvw == LANES
    n_tiles = kv_ref.shape[1] // TOKEN_TILE
    scale = HEAD_DIM ** -0.5
    lane = lax.broadcasted_iota(jnp.int32, (TOKEN_TILE, kvw), 1)
    first_half = lane < HEAD_DIM

    @pl.when(t == 0)
    def _prep_keys():
        bd = bd_ref[0:kvw, 0:kvw]
        rot = rot_ref[0:kvw, 0:kvw]
        kg = kg_ref[...]

        def put(i, k_rows, k):
            swapped = pltpu.roll(k, HEAD_DIM, axis=1)
            kp_ref[0, k_rows, :] = jnp.where(first_half, k, swapped).astype(BF16)
            kp_ref[1, k_rows, :] = jnp.where(first_half, swapped, k).astype(BF16)
            vt_ref[i] = kv_ref[0, k_rows, kvw:2 * kvw].astype(F32).T.astype(BF16)

        put(0, pl.ds(0, TOKEN_TILE), _head_rms(kv_ref[0, 0:TOKEN_TILE, 0:kvw].astype(F32), bd, kg))

        def body(i, carry):
            r0 = pl.multiple_of(i * TOKEN_TILE, TOKEN_TILE)
            rows = pl.ds(CTX_LEN + r0, TOKEN_TILE)
            kn = _head_rms(kv_ref[0, rows, 0:kvw].astype(F32), bd, kg)
            put(i + CTX_LEN // TOKEN_TILE, rows,
                _rope(kn, rot, cos_ref[pl.ds(r0, TOKEN_TILE), 0:kvw], sin_ref[pl.ds(r0, TOKEN_TILE), 0:kvw]))
            return carry

        lax.fori_loop(0, n_tiles - CTX_LEN // TOKEN_TILE, body, 0)

    qn = _head_rms(q_ref[0].astype(F32), bd_ref[...], qg_ref[...])

    def attend(qh, key_tiles):
        nk = key_tiles * TOKEN_TILE
        heads = [(g, r) for g in range(A_KV_HEADS) for r in range(rep)]
        qms = [jnp.where(first_half == (r == 0), qh[:, g * kvw:(g + 1) * kvw] * scale, 0.0).astype(BF16)
               for g, r in heads]
        ss = [_dot_nt(kp_ref[g, 0:nk, :], qms[i]) for i, (g, r) in enumerate(heads)]
        es = [jnp.exp(s - jnp.max(s, axis=0, keepdims=True)) for s in ss]
        ls = [jnp.sum(e, axis=0, keepdims=True) for e in es]
        ebs = [e.astype(BF16) for e in es]
        outs = []
        for i, (g, r) in enumerate(heads):
            o = None
            for k in range(key_tiles):
                part = jnp.dot(vt_ref[k, g * HEAD_DIM:(g + 1) * HEAD_DIM, :],
                               ebs[i][k * TOKEN_TILE:(k + 1) * TOKEN_TILE, :], preferred_element_type=F32)
                o = part if o is None else o + part
            outs.append(o / ls[i])
        return jnp.concatenate(outs, axis=0).T

    @pl.when(t == 0)
    def _ctx_queries():
        o_ref[0] = attend(qn, CTX_LEN // TOKEN_TILE).astype(o_ref.dtype)

    @pl.when(t > 0)
    def _latent_queries():
        r0 = pl.multiple_of((t - 1) * TOKEN_TILE, TOKEN_TILE)
        qr = _rope(qn, rot_ref[...], cos_ref[pl.ds(r0, TOKEN_TILE), :], sin_ref[pl.ds(r0, TOKEN_TILE), :])
        o_ref[0] = attend(qr, n_tiles).astype(o_ref.dtype)


def _attn_a_call(a_q, a_kv, cos_t, sin_t, q_gain, k_gain, ones_bd, rot_m):
    nb, nt, _ = a_q.shape
    tiles = nt // TOKEN_TILE
    seq = nt - CTX_LEN
    qw = A_HEADS * HEAD_DIM
    return pl.pallas_call(
        _attn_a_kernel,
        grid=(nb, tiles),
        in_specs=[
            pl.BlockSpec((1, TOKEN_TILE, qw), lambda b, t: (b, t, 0)),
            pl.BlockSpec((1, nt, qw), lambda b, t: (b, 0, 0)),
            pl.BlockSpec((seq, qw), lambda b, t: (0, 0)),
            pl.BlockSpec((seq, qw), lambda b, t: (0, 0)),
            pl.BlockSpec((1, qw), lambda b, t: (0, 0)),
            pl.BlockSpec((1, A_KV_HEADS * HEAD_DIM), lambda b, t: (0, 0)),
            pl.BlockSpec((qw, qw), lambda b, t: (0, 0)),
            pl.BlockSpec((qw, qw), lambda b, t: (0, 0)),
        ],
        out_specs=pl.BlockSpec((1, TOKEN_TILE, qw), lambda b, t: (b, t, 0)),
        out_shape=jax.ShapeDtypeStruct((nb, nt, qw), BF16),
        scratch_shapes=[pltpu.VMEM((A_KV_HEADS, nt, A_KV_HEADS * HEAD_DIM), BF16),
                        pltpu.VMEM((tiles, A_KV_HEADS * HEAD_DIM, TOKEN_TILE), BF16)],
        compiler_params=_params(("arbitrary", "arbitrary")),
        name="mixer_a_gqa",
    )(a_q, a_kv, cos_t, sin_t, q_gain, k_gain, ones_bd, rot_m)


NB_QROWS = TOKEN_TILE // GRID_W
NB_KROWS = 12
NB_INVALID = 2 * WIN_R - 1


def _attn_b_kernel(q_ref, k_ref, v_ref, bt_ref, o_ref):
    t = pl.program_id(1)
    scale = HEAD_DIM ** -0.5
    rows = (k_ref.shape[1] - CTX_LEN) // GRID_W
    wr = min(WIN_R, rows)

    def softmax_pv(parts):
        m = None
        for s, _ in parts:
            mi = jnp.max(s, axis=-1, keepdims=True)
            m = mi if m is None else jnp.maximum(m, mi)
        acc, l = None, None
        for s, v in parts:
            e = jnp.exp(s - m)
            li = jnp.sum(e, axis=-1, keepdims=True)
            oi = jnp.dot(e.astype(BF16), v, preferred_element_type=F32)
            acc = oi if acc is None else acc + oi
            l = li if l is None else l + li
        return acc / l

    @pl.when(t == 0)
    def _ctx_queries():
        outs = []
        for h in range(B_HEADS):
            sl = slice(h * HEAD_DIM, (h + 1) * HEAD_DIM)
            qq = (q_ref[0, :, sl].astype(F32) * scale).astype(BF16)
            s = _dot_nt(qq, k_ref[0, 0:CTX_LEN, sl])
            outs.append(softmax_pv([(s, v_ref[0, 0:CTX_LEN, sl])]))
        o_ref[0] = jnp.concatenate(outs, axis=-1).astype(o_ref.dtype)

    @pl.when(t > 0)
    def _latent_queries():
        r0 = (t - 1) * NB_QROWS
        start = jnp.clip(r0 - wr // 2, 0, rows - NB_KROWS)
        k0 = pl.multiple_of(CTX_LEN + start * GRID_W, GRID_W)
        nk = NB_KROWS * GRID_W
        lane = lax.broadcasted_iota(jnp.int32, (GRID_W, 2 * GRID_W), 1)
        left = lane < GRID_W
        slots = []
        for i in range(NB_QROWS):
            r = r0 + i
            rs = jnp.clip(r - wr // 2, 0, rows - wr)
            row_slots = []
            for j in range(NB_KROWS):
                kr = start + j
                ok = jnp.logical_and(kr >= rs, kr < rs + wr)
                row_slots.append(jnp.where(ok, kr - r + WIN_R - 1, NB_INVALID))
            slots.append(row_slots)
        sls = [slice(h * HEAD_DIM, (h + 1) * HEAD_DIM) for h in range(B_HEADS)]
        qqs = [(q_ref[0, :, sl].astype(F32) * scale).astype(BF16) for sl in sls]
        s_locs = [_dot_nt(qqs[h], k_ref[0, pl.ds(k0, nk), sls[h]]) for h in range(B_HEADS)]
        s_ctxs = [_dot_nt(qqs[h], k_ref[0, 0:CTX_LEN, sls[h]]) for h in range(B_HEADS)]
        probs = []
        for h in range(B_HEADS):
            bias_rows = []
            for i in range(NB_QROWS):
                tiles = []
                for jp in range(NB_KROWS // 2):
                    b0 = bt_ref[h, slots[i][2 * jp]]
                    b1 = bt_ref[h, slots[i][2 * jp + 1]]
                    tiles.append(jnp.where(left, b0, b1))
                bias_rows.append(jnp.concatenate(tiles, axis=-1))
            s_loc = s_locs[h] + jnp.concatenate(bias_rows, axis=0)
            m = jnp.maximum(jnp.max(s_loc, axis=-1, keepdims=True), jnp.max(s_ctxs[h], axis=-1, keepdims=True))
            e_loc = jnp.exp(s_loc - m)
            e_ctx = jnp.exp(s_ctxs[h] - m)
            probs.append((e_loc.astype(BF16), e_ctx.astype(BF16),
                          jnp.sum(e_loc, axis=-1, keepdims=True) + jnp.sum(e_ctx, axis=-1, keepdims=True)))
        outs = [(jnp.dot(probs[h][0], v_ref[0, pl.ds(k0, nk), sls[h]], preferred_element_type=F32)
                 + jnp.dot(probs[h][1], v_ref[0, 0:CTX_LEN, sls[h]], preferred_element_type=F32)) / probs[h][2]
                for h in range(B_HEADS)]
        o_ref[0] = jnp.concatenate(outs, axis=-1).astype(o_ref.dtype)


def _attn_b_call(b_q, b_k, b_v, bias_tab):
    nb, nt, w = b_q.shape
    tiles = nt // TOKEN_TILE
    return pl.pallas_call(
        _attn_b_kernel,
        grid=(nb, tiles),
        in_specs=[
            pl.BlockSpec((1, TOKEN_TILE, w), lambda b, t: (b, t, 0)),
            pl.BlockSpec((1, nt, w), lambda b, t: (b, 0, 0)),
            pl.BlockSpec((1, nt, w), lambda b, t: (b, 0, 0)),
            pl.BlockSpec(bias_tab.shape, lambda b, t: (0, 0, 0, 0)),
        ],
        out_specs=pl.BlockSpec((1, TOKEN_TILE, w), lambda b, t: (b, t, 0)),
        out_shape=jax.ShapeDtypeStruct((nb, nt, w), BF16),
        compiler_params=_params(("arbitrary", "arbitrary")),
        name="mixer_b_neighbourhood",
    )(b_q, b_k, b_v, bias_tab)


def _nb_bias_table(rpb):
    cols = jnp.arange(GRID_W, dtype=jnp.int32)
    col_start = jnp.clip(cols - WIN_C // 2, 0, GRID_W - WIN_C)
    col_ok = (cols[None, :] >= col_start[:, None]) & (cols[None, :] < col_start[:, None] + WIN_C)
    dc_idx = jnp.clip(cols[None, :] - cols[:, None] + WIN_C - 1, 0, 2 * WIN_C - 2)
    tab = rpb.astype(F32)[..., dc_idx]
    tab = jnp.where(col_ok, tab, NEG_BIG)
    tab = jnp.concatenate([tab, jnp.full_like(tab[..., :1, :, :], NEG_BIG)], axis=-3)
    return jnp.concatenate([tab, tab], axis=-1)


def _order_masks():
    i = lax.broadcasted_iota(jnp.int32, (CHUNK, CHUNK), 0)
    j = lax.broadcasted_iota(jnp.int32, (CHUNK, CHUNK), 1)
    return ((j <= i, j < i), (j >= i, j > i))


def _scan_chunk(step, direction, n_ctx_chunks, n_chunks):
    if direction == 0:
        return step
    return jnp.where(step < n_ctx_chunks, n_ctx_chunks - 1 - step, n_chunks + n_ctx_chunks - 1 - step)


def _sel_dot(m, a):
    ah, al = _split(a)
    return jnp.dot(m, ah, preferred_element_type=F32) + jnp.dot(m, al, preferred_element_type=F32)


def _gated_out_tiles(part_refs, gate_ref, gain_ref, bd_ref, o_ref):
    nt = part_refs[0].shape[0]
    bd = bd_ref[...]
    gain = gain_ref[...]

    def body(i, carry):
        r0 = pl.multiple_of(i * TOKEN_TILE, TOKEN_TILE)
        o = sum(p[pl.ds(r0, TOKEN_TILE), :] for p in part_refs)
        y = _head_rms(o, bd, gain) * _silu(gate_ref[0, pl.ds(r0, TOKEN_TILE), :])
        o_ref[0, pl.ds(r0, TOKEN_TILE), :] = y.astype(o_ref.dtype)
        return carry

    lax.fori_loop(0, nt // TOKEN_TILE, body, 0)


C_BETA_LANE = 0
C_A_LANE = 8
C_T_ROWS = 16


SOLVE_BLOCK = 16
GDN_LOCAL_CHUNKS = 4


def _unit_lower_solve(lmats, rhss, same_block, eye):
    n = range(len(lmats))
    lds = [jnp.where(same_block, l, 0.0) for l in lmats]
    ts = [eye - ld for ld in lds]
    ps = lds
    span = 2
    while span < SOLVE_BLOCK:
        ps = [_dot(p, p) for p in ps]
        ts = [ts[i] + _dot(ts[i], ps[i]) for i in n]
        span *= 2
    width = rhss[0].shape[1]
    mzs = [_dot(ts[i], jnp.concatenate([rhss[i], lmats[i] - lds[i]], axis=-1)) for i in n]
    mmzs = [_dot(mz[:, width:], mz) for mz in mzs]
    zs = [mzs[i][:, 0:width] - mmzs[i][:, 0:width] for i in n]
    ps = [mmz[:, width:] for mmz in mmzs]
    span = 2
    while span < CHUNK // SOLVE_BLOCK:
        zs = [zs[i] + _dot(ps[i], zs[i]) for i in n]
        span *= 2
        if span < CHUNK // SOLVE_BLOCK:
            ps = [_dot(p, p) for p in ps]
    return zs


def _gdn_kernel(qkv_ref, ba_ref, gate_ref, conv_ref, par_ref, gain_ref, bd_ref, o_ref,
                q_s, k_s, v_s, bl_s, g_s, gt_s, a12_s, b2_s, egl_s, o_s, st_s):
    nt = qkv_ref.shape[1]
    n_tiles = nt // TOKEN_TILE
    n_chunks = nt // CHUNK
    n_ctx_chunks = CTX_LEN // CHUNK
    w = C_HEADS * HEAD_DIM
    pack = 2 * SUBLANES
    bd = bd_ref[...]
    lane = lax.broadcasted_iota(jnp.int32, (TOKEN_TILE, LANES), 1)
    lane_c = lax.broadcasted_iota(jnp.int32, (CHUNK, LANES), 1)
    row = lax.broadcasted_iota(jnp.int32, (TOKEN_TILE, 1), 0)
    neg_rate = -jnp.exp(par_ref[0:1, :])
    dt_bias = par_ref[1:2, :]
    w_prev, w_mid, w_next = conv_ref[0:1, :], conv_ref[1:2, :], conv_ref[2:3, :]
    masks = _order_masks()
    mask_bf = [jnp.where(m[0], 1.0, 0.0).astype(BF16) for m in masks]

    def prep(i, carry):
        r0 = pl.multiple_of(i * TOKEN_TILE, TOKEN_TILE)
        x = qkv_ref[0, pl.ds(r0, TOKEN_TILE), :].astype(F32)
        before = qkv_ref[0, pl.ds(pl.multiple_of(jnp.maximum(r0 - pack, 0), pack), pack), :].astype(F32)
        after = qkv_ref[0, pl.ds(pl.multiple_of(jnp.minimum(r0 + TOKEN_TILE, nt - pack), pack), pack), :].astype(F32)
        first_of_seq = jnp.logical_or(i == 0, i == CTX_LEN // TOKEN_TILE)
        last_of_seq = jnp.logical_or(i == CTX_LEN // TOKEN_TILE - 1, i == n_tiles - 1)
        edge_prev = jnp.where(first_of_seq, 0.0, before[pack - 1:pack, :])
        edge_next = jnp.where(last_of_seq, 0.0, after[0:1, :])
        x_prev = jnp.where(row == 0, edge_prev, pltpu.roll(x, 1, axis=0))
        x_next = jnp.where(row == TOKEN_TILE - 1, edge_next, pltpu.roll(x, TOKEN_TILE - 1, axis=0))
        y = _silu(x_prev * w_prev + x * w_mid + x_next * w_next)
        q, k, v = y[:, 0:w], y[:, w:2 * w], y[:, 2 * w:3 * w]
        q_s[pl.ds(r0, TOKEN_TILE), :] = q * lax.rsqrt(_dot_sel(q * q, bd) + EPS) * (HEAD_DIM ** -0.5)
        k_s[pl.ds(r0, TOKEN_TILE), :] = k * lax.rsqrt(_dot_sel(k * k, bd) + EPS)
        v_s[pl.ds(r0, TOKEN_TILE), :] = v
        ba = ba_ref[0, pl.ds(r0, TOKEN_TILE), :]
        sp = ba + dt_bias
        softplus = jnp.maximum(sp, 0.0) + jnp.log1p(jnp.exp(-jnp.abs(sp)))
        bl = jnp.where(lane < C_A_LANE, jax.nn.sigmoid(ba), neg_rate * softplus)
        bl_s[pl.ds(r0, TOKEN_TILE), :] = bl
        for c in range(TOKEN_TILE // CHUNK):
            blc = bl[c * CHUNK:(c + 1) * CHUNK, :]
            g = jnp.where(lane_c < C_A_LANE + C_HEADS, _sel_dot(mask_bf[0], blc), _sel_dot(mask_bf[1], blc))
            g_s[pl.ds(r0 + c * CHUNK, CHUNK), :] = g
            gt_s[i * (TOKEN_TILE // CHUNK) + c] = g.T[0:C_T_ROWS, :]
        return carry

    lax.fori_loop(0, n_tiles, prep, 0)

    ri = lax.broadcasted_iota(jnp.int32, (CHUNK, CHUNK), 0)
    ci = lax.broadcasted_iota(jnp.int32, (CHUNK, CHUNK), 1)
    same_block = (ri // SOLVE_BLOCK) == (ci // SOLVE_BLOCK)
    eye = jnp.where(ri == ci, 1.0, 0.0)

    src = lax.broadcasted_iota(jnp.int32, (LANES, w), 0)
    dst_head = lax.broadcasted_iota(jnp.int32, (LANES, w), 1) // HEAD_DIM
    spread_g = [jnp.where(src == C_A_LANE + d * C_HEADS + dst_head, 1.0, 0.0).astype(BF16) for d in range(2)]
    spread_b = [jnp.where(src == C_BETA_LANE + d * C_HEADS + dst_head, 1.0, 0.0).astype(BF16) for d in range(2)]
    heads = [slice(h * HEAD_DIM, (h + 1) * HEAD_DIM) for h in range(C_HEADS)]
    lane_head = lax.broadcasted_iota(jnp.int32, (CHUNK, w), 1) // HEAD_DIM

    def local(it, carry):
        chunks = [it * GDN_LOCAL_CHUNKS + t for t in range(GDN_LOCAL_CHUNKS)]
        rows = [pl.ds(pl.multiple_of(c * CHUNK, CHUNK), CHUNK) for c in chunks]
        td = [(t, d) for t in range(GDN_LOCAL_CHUNKS) for d in range(2)]
        qcs, kcs, vcs = [q_s[r, :] for r in rows], [k_s[r, :] for r in rows], [v_s[r, :] for r in rows]
        g_alls = [_dot_sel(g_s[rows[t], :], spread_g[d]) for t, d in td]
        b_alls = [_dot_sel(bl_s[rows[t], :], spread_b[d]) for t, d in td]
        qks = [[_dot_nt(jnp.where(lane_head == h, qcs[t], 0.0), kcs[t]) for h in range(C_HEADS)]
               for t in range(GDN_LOCAL_CHUNKS)]
        pre = []
        for i, (t, d) in enumerate(td):
            last = CHUNK - 1 if d == 0 else 0
            e_g = jnp.exp(g_alls[i])
            g_last = g_alls[i][last:last + 1, :]
            kb = kcs[t] * b_alls[i]
            kd_t = (kcs[t] * jnp.exp(g_last - g_alls[i])).T.astype(BF16)
            egl_s[d, chunks[t]] = jnp.broadcast_to(jnp.exp(g_last), (SUBLANES, w))
            pre.append((kb, kb * e_g, vcs[t] * b_alls[i], qcs[t] * e_g, kd_t))
        kbk = [[_dot_nt(jnp.where(lane_head == h, pre[i][0], 0.0), kcs[t]) for h in range(C_HEADS)]
               for i, (t, d) in enumerate(td)]
        lmats, rhss, keep = [], [], []
        for i, (t, d) in enumerate(td):
            incl, strict = masks[d]
            g_rows = gt_s[chunks[t]]
            for h, sl in enumerate(heads):
                la = C_A_LANE + d * C_HEADS + h
                decay = jnp.exp(jnp.where(incl, g_alls[i][:, sl] - g_rows[la:la + 1, :], NEG_BIG))
                lmats.append(jnp.where(strict, kbk[i][h] * decay, 0.0))
                rhss.append(jnp.concatenate([pre[i][1][:, sl], pre[i][2][:, sl]], axis=-1))
                keep.append(((qks[t][h] * decay).astype(BF16), pre[i][3][:, sl], pre[i][4][sl, :]))
        sols = [s.astype(BF16) for s in _unit_lower_solve(lmats, rhss, same_block, eye)]
        qwu = [jnp.dot(keep[i][0], sols[i], preferred_element_type=F32) for i in range(len(sols))]
        kwu = [jnp.dot(keep[i][2], sols[i], preferred_element_type=F32) for i in range(len(sols))]
        i = 0
        for t in range(GDN_LOCAL_CHUNKS):
            c = it * GDN_LOCAL_CHUNKS + t
            r0 = pl.multiple_of(c * CHUNK, CHUNK)
            o_const = None
            for d in range(2):
                u = range(i, i + C_HEADS)
                a12_s[d, c, 0:CHUNK, :] = jnp.concatenate(
                    [keep[j][1] - qwu[j][:, 0:HEAD_DIM] for j in u], axis=-1).astype(BF16)
                a12_s[d, c, CHUNK:2 * CHUNK, :] = jnp.concatenate(
                    [-kwu[j][:, 0:HEAD_DIM] for j in u], axis=-1).astype(BF16)
                b2_s[d, c] = jnp.concatenate([kwu[j][:, HEAD_DIM:] for j in u], axis=-1)
                part = jnp.concatenate([qwu[j][:, HEAD_DIM:] for j in u], axis=-1)
                o_const = part if o_const is None else o_const + part
                i += C_HEADS
            o_s[pl.ds(r0, CHUNK), :] = o_const
        return carry

    lax.fori_loop(0, n_chunks // GDN_LOCAL_CHUNKS, local, 0)

    st_s[...] = jnp.zeros(st_s.shape, F32)

    def scan(step, carry):
        units = [(d, h) for d in range(2) for h in range(C_HEADS)]
        cs = [_scan_chunk(step, d, n_ctx_chunks, n_chunks) for d in range(2)]
        a12 = [a12_s[d, cs[d]] for d in range(2)]
        b2 = [b2_s[d, cs[d]] for d in range(2)]
        egl = [egl_s[d, cs[d]] for d in range(2)]
        ss = [st_s[d, h] for d, h in units]
        rs = [jnp.dot(a12[d][:, heads[h]], ss[i].astype(BF16), preferred_element_type=F32)
              for i, (d, h) in enumerate(units)]
        for i, (d, h) in enumerate(units):
            st_s[d, h] = ss[i] * egl[d][0:1, heads[h]] + rs[i][CHUNK:, :] + b2[d][:, heads[h]]
        for d in range(2):
            r0 = pl.multiple_of(cs[d] * CHUNK, CHUNK)
            o = jnp.concatenate([rs[d * C_HEADS + h][0:CHUNK, :] for h in range(C_HEADS)], axis=-1)
            o_s[pl.ds(r0, CHUNK), :] = o_s[pl.ds(r0, CHUNK), :] + o
        return carry

    lax.fori_loop(0, n_chunks, scan, 0)
    _gated_out_tiles((o_s,), gate_ref, gain_ref, bd_ref, o_ref)


def _gdn_call(c_qkv, c_ba, c_g, conv_w, par, gain, ones_bd):
    nb, nt, _ = c_qkv.shape
    w = C_HEADS * HEAD_DIM
    nc = nt // CHUNK
    return pl.pallas_call(
        _gdn_kernel,
        grid=(nb,),
        in_specs=[
            pl.BlockSpec((1, nt, 3 * w), lambda b: (b, 0, 0)),
            pl.BlockSpec((1, nt, LANES), lambda b: (b, 0, 0)),
            pl.BlockSpec((1, nt, w), lambda b: (b, 0, 0)),
            pl.BlockSpec((SUBLANES, 3 * w), lambda b: (0, 0)),
            pl.BlockSpec((SUBLANES, LANES), lambda b: (0, 0)),
            pl.BlockSpec((1, w), lambda b: (0, 0)),
            pl.BlockSpec((w, w), lambda b: (0, 0)),
        ],
        out_specs=pl.BlockSpec((1, nt, w), lambda b: (b, 0, 0)),
        out_shape=jax.ShapeDtypeStruct((nb, nt, w), BF16),
        scratch_shapes=[
            pltpu.VMEM((nt, w), F32), pltpu.VMEM((nt, w), F32), pltpu.VMEM((nt, w), F32),
            pltpu.VMEM((nt, LANES), F32), pltpu.VMEM((nt, LANES), F32),
            pltpu.VMEM((nc, C_T_ROWS, CHUNK), F32),
            pltpu.VMEM((2, nc, 2 * CHUNK, w), BF16),
            pltpu.VMEM((2, nc, CHUNK, w), F32),
            pltpu.VMEM((2, nc, SUBLANES, w), F32),
            pltpu.VMEM((nt, w), F32),
            pltpu.VMEM((2, C_HEADS, HEAD_DIM, HEAD_DIM), F32),
        ],
        compiler_params=_params(("arbitrary",)),
        name="mixer_c_gated_delta",
    )(c_qkv, c_ba, c_g, conv_w, par, gain, ones_bd)


GLA_EXP_CAP = 80.0
GLA_LOCAL_CHUNKS = 4
GLA_SCAN_STEPS = 4


def _gla_kernel(qk_ref, v_ref, lr_ref, gate_ref, gw_ref, gb_ref, gain_ref, bd_ref, o_ref,
                la_s, qg_s, el_s, ds_s, o_s):
    nt = qk_ref.shape[1]
    n_tiles = nt // TOKEN_TILE
    n_chunks = nt // CHUNK
    n_ctx_chunks = CTX_LEN // CHUNK
    kw = D_HEADS * D_KDIM
    gw = gw_ref[...]
    gb = gb_ref[...]

    def prep(i, carry):
        r0 = pl.multiple_of(i * TOKEN_TILE, TOKEN_TILE)
        z = _dot3(lr_ref[0, pl.ds(r0, TOKEN_TILE), :], gw) + gb
        log_sig = jnp.minimum(z, 0.0) - jnp.log1p(jnp.exp(-jnp.abs(z)))
        la_s[pl.ds(r0, TOKEN_TILE), :] = log_sig * (1.0 / GLA_TAU)
        return carry

    lax.fori_loop(0, n_tiles, prep, 0)

    masks = _order_masks()
    mask_bf = [jnp.where(m[0], 1.0, 0.0).astype(BF16) for m in masks]
    vw = D_HEADS * HEAD_DIM
    k_head = lax.broadcasted_iota(jnp.int32, (CHUNK, kw), 1) // D_KDIM
    v_head = lax.broadcasted_iota(jnp.int32, (CHUNK, vw), 1) // HEAD_DIM
    state_diag = (lax.broadcasted_iota(jnp.int32, (vw, kw), 0) // HEAD_DIM
                  == lax.broadcasted_iota(jnp.int32, (vw, kw), 1) // D_KDIM)

    def local(it, carry):
        units, pre = [], {}
        for t in range(GLA_LOCAL_CHUNKS):
            c = it * GLA_LOCAL_CHUNKS + t
            r0 = pl.multiple_of(c * CHUNK, CHUNK)
            qk = qk_ref[0, pl.ds(r0, CHUNK), :]
            q = qk[:, 0:kw] * (D_KDIM ** -0.5)
            k = qk[:, kw:2 * kw]
            v = v_ref[0, pl.ds(r0, CHUNK), :]
            v_b = v.astype(BF16)
            v_t = v.T.astype(BF16)
            for d in range(2):
                last = CHUNK - 1 if d == 0 else 0
                g = _sel_dot(mask_bf[d], la_s[pl.ds(r0, CHUNK), d * kw:(d + 1) * kw])
                g_mid = g[CHUNK // 2:CHUNK // 2 + 1, :]
                g_last = g[last:last + 1, :]
                q_t = q * jnp.exp(jnp.minimum(g - g_mid, GLA_EXP_CAP))
                pre[(t, d)] = ([jnp.where(k_head == h, q_t, 0.0).astype(BF16) for h in range(D_HEADS)],
                               (k * jnp.exp(jnp.minimum(g_mid - g, GLA_EXP_CAP))).astype(BF16),
                               (k * jnp.exp(g_last - g)).astype(BF16), v_b, v_t)
                qg_s[d, c] = (q * jnp.exp(g)).astype(BF16)
                el_s[d, c] = jnp.broadcast_to(jnp.exp(g_last), (SUBLANES, kw))
                units += [(t, d, h) for h in range(D_HEADS)]
        a_ = [jnp.where(masks[d][0], _dot_nt(pre[(t, d)][0][h], pre[(t, d)][1]), 0.0).astype(BF16)
              for t, d, h in units]
        ds_ = {td: jnp.dot(p[4], p[2], preferred_element_type=F32) for td, p in pre.items()}
        av = [jnp.dot(a_[i], pre[(t, d)][3], preferred_element_type=F32) for i, (t, d, h) in enumerate(units)]
        i = 0
        for t in range(GLA_LOCAL_CHUNKS):
            c = it * GLA_LOCAL_CHUNKS + t
            r0 = pl.multiple_of(c * CHUNK, CHUNK)
            o_const = jnp.zeros((CHUNK, vw), F32)
            for d in range(2):
                ds_s[d, c] = jnp.where(state_diag, ds_[(t, d)], 0.0)
                for h in range(D_HEADS):
                    o_const = o_const + jnp.where(v_head == h, av[i], 0.0)
                    i += 1
            o_s[pl.ds(r0, CHUNK), :] = o_const
        return carry

    lax.fori_loop(0, n_chunks // GLA_LOCAL_CHUNKS, local, 0)

    def scan(it, states):
        states = list(states)
        jobs = []
        for t in range(GLA_SCAN_STEPS):
            step = it * GLA_SCAN_STEPS + t
            for d in range(2):
                c = _scan_chunk(step, d, n_ctx_chunks, n_chunks)
                jobs.append((c, qg_s[d, c], states[d].astype(BF16)))
                states[d] = states[d] * el_s[d, c][0:1, :] + ds_s[d, c]
        outs = [_dot_nt(qg, sb) for _, qg, sb in jobs]
        for (c, _, _), o in zip(jobs, outs):
            r0 = pl.multiple_of(c * CHUNK, CHUNK)
            o_s[pl.ds(r0, CHUNK), :] = o_s[pl.ds(r0, CHUNK), :] + o
        return tuple(states)

    zero = jnp.zeros((vw, kw), F32)
    lax.fori_loop(0, n_chunks // GLA_SCAN_STEPS, scan, (zero, zero))
    _gated_out_tiles((o_s,), gate_ref, gain_ref, bd_ref, o_ref)


def _gla_call(d_qk, d_v, d_lr, d_g, gw_blk, gb_row, gain, ones_bd):
    nb, nt, _ = d_qk.shape
    vw = D_HEADS * HEAD_DIM
    kw2 = 2 * D_HEADS * D_KDIM
    return pl.pallas_call(
        _gla_kernel,
        grid=(nb,),
        in_specs=[
            pl.BlockSpec((1, nt, kw2), lambda b: (b, 0, 0)),
            pl.BlockSpec((1, nt, vw), lambda b: (b, 0, 0)),
            pl.BlockSpec((1, nt, LANES), lambda b: (b, 0, 0)),
            pl.BlockSpec((1, nt, vw), lambda b: (b, 0, 0)),
            pl.BlockSpec((LANES, kw2), lambda b: (0, 0)),
            pl.BlockSpec((1, kw2), lambda b: (0, 0)),
            pl.BlockSpec((1, vw), lambda b: (0, 0)),
            pl.BlockSpec((vw, vw), lambda b: (0, 0)),
        ],
        out_specs=pl.BlockSpec((1, nt, vw), lambda b: (b, 0, 0)),
        out_shape=jax.ShapeDtypeStruct((nb, nt, vw), BF16),
        scratch_shapes=[
            pltpu.VMEM((nt, kw2), F32),
            pltpu.VMEM((2, nt // CHUNK, CHUNK, kw2 // 2), BF16),
            pltpu.VMEM((2, nt // CHUNK, SUBLANES, kw2 // 2), F32),
            pltpu.VMEM((2, nt // CHUNK, vw, kw2 // 2), F32),
            pltpu.VMEM((nt, vw), F32),
        ],
        compiler_params=_params(("arbitrary",)),
        name="mixer_d_gla",
    )(d_qk, d_v, d_lr, d_g, gw_blk, gb_row, gain, ones_bd)


ROUTE_E1, ROUTE_E2, ROUTE_W1, ROUTE_W2 = 0, 1, 2, 3
ROUTER_EXPERT_LANE = N_GROUPS


def _route(logits):
    lane = lax.broadcasted_iota(jnp.int32, logits.shape, 1).astype(F32)
    far = float(LANES)
    in_grp = lane < N_GROUPS
    lg = jnp.where(in_grp, logits, NEG_BIG)
    mg = jnp.max(lg, axis=-1, keepdims=True)
    grp = jnp.min(jnp.where(lg == mg, lane, far), axis=-1, keepdims=True)
    p_grp = 1.0 / jnp.sum(jnp.where(in_grp, jnp.exp(lg - mg), 0.0), axis=-1, keepdims=True)
    lo = ROUTER_EXPERT_LANE + EXP_PER_GROUP * grp
    in_exp = jnp.logical_and(lane >= lo, lane < lo + EXP_PER_GROUP)
    le = jnp.where(in_exp, logits, NEG_BIG)
    m1 = jnp.max(le, axis=-1, keepdims=True)
    i1 = jnp.min(jnp.where(le == m1, lane, far), axis=-1, keepdims=True)
    le2 = jnp.where(lane == i1, NEG_BIG, le)
    m2 = jnp.max(le2, axis=-1, keepdims=True)
    i2 = jnp.min(jnp.where(le2 == m2, lane, far), axis=-1, keepdims=True)
    e2 = jnp.exp(m2 - m1)
    w1 = p_grp / (1.0 + e2)
    w2 = p_grp * e2 / (1.0 + e2)
    out = jnp.where(lane == ROUTE_E1, i1 - ROUTER_EXPERT_LANE, 0.0)
    out = jnp.where(lane == ROUTE_E2, i2 - ROUTER_EXPERT_LANE, out)
    out = jnp.where(lane == ROUTE_W1, w1, out)
    return jnp.where(lane == ROUTE_W2, w2, out)


def _merge_kernel(x_ref, mod_ref, oa_ref, ob_ref, oc_ref, od_ref, wg_ref, wbr_ref, wo_ref,
                  lng_ref, lnb_ref, wr_ref, br_ref, x1_ref, h2_ref, route_ref):
    x = x_ref[0]
    mod = mod_ref[0]
    h = (x * (1.0 + mod[1:2]) + mod[0:1]).astype(BF16)
    m = None
    for z, o_ref in enumerate((oa_ref, ob_ref, oc_ref, od_ref)):
        gate = jax.nn.sigmoid(jnp.dot(h, wg_ref[:, z * D_MODEL:(z + 1) * D_MODEL], preferred_element_type=F32))
        up = jnp.dot(o_ref[0], wbr_ref[z], preferred_element_type=F32)
        m = gate * up if m is None else m + gate * up
    y = jnp.dot(m.astype(BF16), wo_ref[...], preferred_element_type=F32)
    x1 = _layer_norm(DN_ALPHA * x + mod[2:3] * y, lng_ref[...], lnb_ref[...])
    x1_ref[0] = x1
    h2 = x1 * (1.0 + mod[4:5]) + mod[3:4]
    for j in range(ROW_VREGS):
        h2_ref[pl.ds(j, TOKEN_TILE, stride=ROW_VREGS), :] = h2[:, j * LANES:(j + 1) * LANES]
    route_ref[0] = _route(_dot3(h2, wr_ref[...]) + br_ref[...]).T[0:SUBLANES, :]


def _merge_call(xa, mods, oa, ob, oc, od, wg, wbr, wo, ln_g, ln_b, wr, br):
    nb, nt, d = xa.shape
    tiles = nt // TOKEN_TILE
    bw = oa.shape[-1]
    tok = lambda b, t: (b, t, 0)
    const2 = lambda b, t: (0, 0)
    return pl.pallas_call(
        _merge_kernel,
        grid=(nb, tiles),
        in_specs=[
            pl.BlockSpec((1, TOKEN_TILE, d), tok),
            pl.BlockSpec((1, SUBLANES, d), _mod_index(nb)),
            pl.BlockSpec((1, TOKEN_TILE, bw), tok), pl.BlockSpec((1, TOKEN_TILE, bw), tok),
            pl.BlockSpec((1, TOKEN_TILE, bw), tok), pl.BlockSpec((1, TOKEN_TILE, bw), tok),
            pl.BlockSpec((d, N_BRANCH * d), const2),
            pl.BlockSpec((N_BRANCH, bw, d), lambda b, t: (0, 0, 0)),
            pl.BlockSpec((d, d), const2),
            pl.BlockSpec((1, d), const2), pl.BlockSpec((1, d), const2),
            pl.BlockSpec((d, LANES), const2), pl.BlockSpec((1, LANES), const2),
        ],
        out_specs=[pl.BlockSpec((1, TOKEN_TILE, d), tok),
                   pl.BlockSpec((TOKEN_TILE * ROW_VREGS, LANES), lambda b, t: (b * tiles + t, 0)),
                   pl.BlockSpec((1, SUBLANES, TOKEN_TILE), lambda b, t: (b * tiles + t, 0, 0))],
        out_shape=[jax.ShapeDtypeStruct((nb, nt, d), F32),
                   jax.ShapeDtypeStruct((nb * nt * ROW_VREGS, LANES), F32),
                   jax.ShapeDtypeStruct((nb * tiles, SUBLANES, TOKEN_TILE), F32)],
        compiler_params=_params(("arbitrary", "arbitrary")),
        name="merge_out_ln1_router",
    )(xa, mods, oa, ob, oc, od, wg, wbr, wo, ln_g, ln_b, wr, br)


MOE_UNROLL = 4


def _moe_chunk(total):
    return max(c for c in range(TOKEN_TILE, MOE_CHUNK + 1, TOKEN_TILE) if total % c == 0)


MOE_GROUP = 2


def _moe_kernel(cnt_ref, off_ref, off_again_ref, idx_ref, wt_ref, x_ref, *refs):
    w_refs = [refs[3 * g:3 * g + 3] for g in range(MOE_GROUP)]
    y_ref = refs[3 * MOE_GROUP]
    xt_s = refs[3 * MOE_GROUP + 1:3 * MOE_GROUP + 1 + MOE_GROUP]
    ot_s = refs[3 * MOE_GROUP + 1 + MOE_GROUP:]
    c = pl.program_id(0)
    j = pl.program_id(1)
    chunk = x_ref.shape[1] // ROW_VREGS
    plan_rows = 2 * chunk

    @pl.when(j == 0)
    def _zero():
        y_ref[...] = jnp.zeros(y_ref.shape, F32)

    n_rows = [cnt_ref[c * N_EXPERTS + j * MOE_GROUP + g] for g in range(MOE_GROUP)]
    off = [off_ref[c * N_EXPERTS + j * MOE_GROUP + g] for g in range(MOE_GROUP)]
    off_again = [off_again_ref[c * N_EXPERTS + j * MOE_GROUP + g] for g in range(MOE_GROUP)]
    n_tiles = functools.reduce(jnp.maximum, [(n + MOE_ROWS - 1) // MOE_ROWS for n in n_rows])

    def slab(tok):
        return pl.ds(pl.multiple_of(tok * ROW_VREGS, ROW_VREGS), ROW_VREGS)

    def gather(base, xt):
        for mi in range(MOE_ROWS):
            xt[pl.ds(mi, ROW_VREGS, stride=MOE_STRIDE), :] = x_ref[0, slab(idx_ref[0, 0, base + mi]), :]

    def experts(group):
        xs = [jnp.concatenate([xt_s[g][k * MOE_STRIDE:k * MOE_STRIDE + MOE_ROWS, :] for k in range(ROW_VREGS)],
                              axis=-1).astype(BF16) for g in group]
        gates = [jnp.dot(x, w_refs[g][0][0], preferred_element_type=F32) for x, g in zip(xs, group)]
        ups = [jnp.dot(x, w_refs[g][1][0], preferred_element_type=F32) for x, g in zip(xs, group)]
        acts = [(_silu(gate) * up).astype(BF16) for gate, up in zip(gates, ups)]
        outs = [jnp.dot(act, w_refs[g][2][0], preferred_element_type=F32) for act, g in zip(acts, group)]
        for out, g in zip(outs, group):
            for k in range(ROW_VREGS):
                ot_s[g][k * MOE_STRIDE:k * MOE_STRIDE + MOE_ROWS, :] = out[:, k * LANES:(k + 1) * LANES]

    def scatter(base, valid, ot):
        for m0 in range(0, MOE_ROWS, MOE_UNROLL):
            pending = []
            for mi in range(m0, m0 + MOE_UNROLL):
                ok = mi < valid
                rows = slab(jnp.where(ok, idx_ref[0, 0, base + mi], chunk))
                wgt = jnp.where(ok, wt_ref[0, 0, base + mi], 0.0)
                upd = y_ref[0, rows, :] + wgt * ot[pl.ds(mi, ROW_VREGS, stride=MOE_STRIDE), :]
                pending.append((rows, upd))
            for rows, upd in pending:
                y_ref[0, rows, :] = upd

    def tile(i, group):
        base = {g: jnp.minimum(off[g] + i * MOE_ROWS, plan_rows) for g in group}
        base_s = {g: jnp.minimum(off_again[g] + i * MOE_ROWS, plan_rows) for g in group}
        for g in group:
            gather(base[g], xt_s[g])
        experts(group)
        for g in group:
            scatter(base_s[g], n_rows[g] - i * MOE_ROWS, ot_s[g])

    @pl.when(n_tiles > 0)
    def _first_tiles():
        tile(0, tuple(range(MOE_GROUP)))

    def more_tiles(i, carry):
        for g in range(MOE_GROUP):
            @pl.when(n_rows[g] > i * MOE_ROWS)
            def _one():
                tile(i, (g,))
        return carry

    lax.fori_loop(1, n_tiles, more_tiles, 0)


def _moe_call(h2_slab, cnt, off, idx_sorted, w_sorted, wg, wu, wd):
    nch, rows_in, _ = h2_slab.shape
    plan = idx_sorted.shape[2]
    rows_out = rows_in + SUBLANES * ROW_VREGS
    w_specs, w_args = [], []
    for g in range(MOE_GROUP):
        pick = lambda c, j, *_, g=g: (j * MOE_GROUP + g, 0, 0)
        w_specs += [pl.BlockSpec((1, D_MODEL, EXP_HIDDEN), pick), pl.BlockSpec((1, D_MODEL, EXP_HIDDEN), pick),
                    pl.BlockSpec((1, EXP_HIDDEN, D_MODEL), pick)]
        w_args += [wg, wu, wd]
    tile_buf = pltpu.VMEM((ROW_VREGS * MOE_STRIDE, LANES), F32)
    grid_spec = pltpu.PrefetchScalarGridSpec(
        num_scalar_prefetch=3,
        grid=(nch, N_EXPERTS // MOE_GROUP),
        in_specs=[
            pl.BlockSpec((1, 1, plan), lambda c, j, *_: (c, 0, 0), memory_space=pltpu.SMEM),
            pl.BlockSpec((1, 1, plan), lambda c, j, *_: (c, 0, 0), memory_space=pltpu.SMEM),
            pl.BlockSpec((1, rows_in, LANES), lambda c, j, *_: (c, 0, 0)),
        ] + w_specs,
        out_specs=pl.BlockSpec((1, rows_out, LANES), lambda c, j, *_: (c, 0, 0)),
        scratch_shapes=[tile_buf] * (2 * MOE_GROUP),
    )
    return pl.pallas_call(
        _moe_kernel,
        grid_spec=grid_spec,
        out_shape=jax.ShapeDtypeStruct((nch, rows_out, LANES), F32),
        compiler_params=_params(("arbitrary", "arbitrary")),
        name="moe_experts",
    )(cnt, off, off, idx_sorted, w_sorted, h2_slab, *w_args)


def _moe_plan(route, chunk):
    nch = route.shape[0] * TOKEN_TILE // chunk
    per = chunk * 2
    eid = route[:, ROUTE_E1:ROUTE_E2 + 1, :].astype(jnp.int32).reshape(nch, per)
    wts = route[:, ROUTE_W1:ROUTE_W2 + 1, :].reshape(nch, per)
    order = jnp.argsort(eid, axis=1, stable=True).astype(jnp.int32)
    token = (order // (2 * TOKEN_TILE)) * TOKEN_TILE + order % TOKEN_TILE
    idx_sorted = jnp.pad(token, ((0, 0), (0, MOE_ROWS)))
    w_sorted = jnp.pad(jnp.take_along_axis(wts, order, axis=1), ((0, 0), (0, MOE_ROWS)))
    cnt = jnp.sum((eid[..., None] == jnp.arange(N_EXPERTS, dtype=jnp.int32)).astype(jnp.int32), axis=1)
    off = jnp.cumsum(cnt, axis=1) - cnt
    return (cnt.reshape(-1).astype(jnp.int32), off.reshape(-1).astype(jnp.int32),
            idx_sorted.reshape(nch, 1, per + MOE_ROWS), w_sorted.reshape(nch, 1, per + MOE_ROWS))


def _ln2_kernel(x_ref, y_ref, mod_ref, g_ref, b_ref, o_ref):
    o_ref[0] = _moe_residual_ln(x_ref[0], y_ref, mod_ref[0][5:6], g_ref, b_ref)


def _ln2_call(x1, y_slab, mods, ln_g, ln_b, chunk):
    nb, nt, d = x1.shape
    tiles = nt // TOKEN_TILE
    ctx_tiles = CTX_LEN // TOKEN_TILE
    slab = _slab_index(tiles, chunk)
    return pl.pallas_call(
        _ln2_kernel,
        grid=(nb, tiles - ctx_tiles),
        in_specs=[pl.BlockSpec((1, TOKEN_TILE, d), lambda b, t: (b, t + ctx_tiles, 0)),
                  pl.BlockSpec((1, TOKEN_TILE * ROW_VREGS, LANES), lambda b, t: slab(b, t + ctx_tiles)),
                  pl.BlockSpec((1, SUBLANES, d), lambda b, t: (b, 0, 0)),
                  pl.BlockSpec((1, d), lambda b, t: (0, 0)), pl.BlockSpec((1, d), lambda b, t: (0, 0))],
        out_specs=pl.BlockSpec((1, TOKEN_TILE, d), lambda b, t: (b, t, 0)),
        out_shape=jax.ShapeDtypeStruct((nb, nt - CTX_LEN, d), F32),
        compiler_params=_params(("arbitrary", "arbitrary")),
        name="moe_residual_ln2",
    )(x1, y_slab, mods, ln_g, ln_b)


def _head_constants():
    w = A_HEADS * HEAD_DIM
    i = np.arange(w)
    ones_bd = (i[:, None] // HEAD_DIM == i[None, :] // HEAD_DIM).astype(np.float32)
    quarter = HEAD_DIM // 4
    rot = np.zeros((w, w), np.float32)
    first = (i % (2 * quarter)) < quarter
    rot[i[first] + quarter, i[first]] = -1.0
    rot[i[~first] - quarter, i[~first]] = 1.0
    return jnp.asarray(ones_bd, BF16), jnp.asarray(rot, BF16)


def _rope_tables(seq):
    t = jnp.arange(seq, dtype=jnp.int32)
    row = (t // GRID_W).astype(F32)
    col = (t % GRID_W).astype(F32)
    nf = HEAD_DIM // 4
    inv = ROPE_THETA ** (-jnp.arange(nf, dtype=F32) / nf)
    ang_r = row[:, None] * inv
    ang_c = col[:, None] * inv
    cos = jnp.concatenate([jnp.cos(ang_r), jnp.cos(ang_r), jnp.cos(ang_c), jnp.cos(ang_c)], axis=-1)
    sin = jnp.concatenate([jnp.sin(ang_r), jnp.sin(ang_r), jnp.sin(ang_c), jnp.sin(ang_c)], axis=-1)
    return jnp.tile(cos, (1, A_HEADS)), jnp.tile(sin, (1, A_HEADS))


def _in_weight(w_in):
    cols = []
    for _, parts, width, _ in IN_GROUPS:
        got = 0
        for p in parts:
            o, n = _IN_OFFS[p]
            cols.append(w_in[..., o:o + n])
            got += n
        if got < width:
            cols.append(jnp.zeros(w_in.shape[:-1] + (width - got,), w_in.dtype))
    return jnp.concatenate(cols, axis=-1).astype(BF16)


def _lane_row(vec, width, offset=0):
    return jnp.zeros((1, width), F32).at[0, offset:offset + vec.shape[0]].set(vec.astype(F32))


def kernel(x, c, ctx, c_ctx, w_ada, b_ada, w_in, a_q_gain, a_k_gain, b_rpb, c_conv, c_a_log, c_dt_bias, c_out_gain, d_gate_w, d_gate_b, d_out_gain, w_branch, w_out, ln1_g, ln1_b, ln2_g, ln2_b, w_router_g, b_router_g, w_router_e, b_router_e, w_up, w_gate, w_down):
    nb, seq, d = x.shape
    depth = w_ada.shape[0]
    nt = CTX_LEN + seq
    assert d == D_MODEL and ctx.shape[1] == CTX_LEN and nb + 1 <= ADA_ROWS
    assert seq % TOKEN_TILE == 0
    chunk = _moe_chunk(nb * nt)

    xa = jnp.concatenate([ctx, x], axis=1)
    cc = jnp.zeros((ADA_ROWS, d), F32).at[:nb].set(c).at[nb].set(c_ctx)
    mods = _ada_call(cc, w_ada, b_ada).reshape(depth, ADA_ROWS, 6, d)[:, :nb + 1]
    mods = jnp.pad(mods, ((0, 0), (0, 0), (0, SUBLANES - 6), (0, 0)))

    ones_bd, rot_m = _head_constants()
    cos_t, sin_t = _rope_tables(seq)
    gates_off = _IN_OFFS['gates'][0]

    w_in_bf = w_in.astype(BF16)
    w_mix = _in_weight(w_in_bf)
    w_gates = w_in_bf[..., gates_off:]
    w_br, w_o = w_branch.astype(BF16), w_out.astype(BF16)
    w_eg, w_eu, w_ed = w_gate.astype(BF16), w_up.astype(BF16), w_down.astype(BF16)
    nb_bias = _nb_bias_table(b_rpb)

    x1 = y_slab = None
    for l in range(depth):
        if l == 0:
            proj = _in_call(xa, mods[l], w_mix[l])
        else:
            xa, proj = _ln2_in_call(x1, y_slab, mods[l - 1], ln2_g[l - 1][None, :], ln2_b[l - 1][None, :],
                                    mods[l], w_mix[l], chunk)
        proj = dict(zip([g[0] for g in IN_GROUPS], proj))

        oa = _attn_a_call(proj['a_q'], proj['a_kv'], cos_t, sin_t,
                          jnp.tile(a_q_gain[l], A_HEADS)[None, :], jnp.tile(a_k_gain[l], A_KV_HEADS)[None, :],
                          ones_bd, rot_m)
        ob = _attn_b_call(proj['b_q'], proj['b_k'], proj['b_v'], nb_bias[l])
        conv_w = jnp.pad(c_conv[l], ((0, SUBLANES - c_conv.shape[1]), (0, 0)))
        par = jnp.concatenate([_lane_row(c_a_log[l].reshape(-1), LANES, C_A_LANE),
                               _lane_row(c_dt_bias[l].reshape(-1), LANES, C_A_LANE),
                               jnp.zeros((SUBLANES - 2, LANES), F32)], axis=0)
        oc = _gdn_call(proj['c_qkv'], proj['c_ba'], proj['c_g'], conv_w, par,
                       jnp.tile(c_out_gain[l], C_HEADS)[None, :], ones_bd)
        kw = D_HEADS * D_KDIM
        gw_blk = jnp.zeros((LANES, 2 * kw), F32)
        gw_blk = gw_blk.at[0:D_GATE_RANK, 0:kw].set(d_gate_w[l, 0])
        gw_blk = gw_blk.at[D_GATE_RANK:2 * D_GATE_RANK, kw:2 * kw].set(d_gate_w[l, 1])
        od = _gla_call(proj['d_qk'], proj['d_v'], proj['d_lr'], proj['d_g'], gw_blk,
                       d_gate_b[l].reshape(1, 2 * kw), jnp.tile(d_out_gain[l], D_HEADS)[None, :], ones_bd)

        wr = jnp.concatenate([w_router_g[l], jnp.transpose(w_router_e[l], (1, 0, 2)).reshape(d, N_EXPERTS)], axis=1)
        wr = jnp.pad(wr, ((0, 0), (0, LANES - wr.shape[1])))
        br = _lane_row(jnp.concatenate([b_router_g[l], b_router_e[l].reshape(-1)]), LANES)
        x1, h2, route = _merge_call(
            xa, mods[l], oa, ob, oc, od, w_gates[l], w_br[l], w_o[l], ln1_g[l][None, :], ln1_b[l][None, :], wr, br)

        cnt, off, idx_sorted, w_sorted = _moe_plan(route, chunk)
        h2_slab = h2.reshape((nb * nt) // chunk, chunk * ROW_VREGS, LANES)
        y_slab = _moe_call(h2_slab, cnt, off, idx_sorted, w_sorted, w_eg[l], w_eu[l], w_ed[l])

    last = depth - 1
    return _ln2_call(x1, y_slab, mods[last], ln2_g[last][None, :], ln2_b[last][None, :], chunk)
```
